```python
import math
import jax
import jax.numpy as jnp
from jax import lax
import numpy as np

D_MODEL = 1024
BATCH = 1
SEQ = 16384
DEPTH = 4

GRID_W = 64
CTX_LEN = 256
D_FF = 2816
N_MOD = 9
N_BRANCH = 3
FOURIER_GROUPS = 4
FOURIER_CH = 64
RET_HEADS = 6
RET_QK = 32
RET_V = 64
RET_CHUNK = 128
DIFF_HEADS = 6
DIFF_QK = 32
DIFF_V = 64
Q_BLOCK = 128
ROPE_BASE = 10000.0
EPS = 1e-6
F_W = FOURIER_GROUPS * FOURIER_CH
RET_QW = RET_HEADS * RET_QK
RET_VW = RET_HEADS * RET_V
DIFF_QW = DIFF_HEADS * 2 * DIFF_QK
DIFF_VW = DIFF_HEADS * DIFF_V
D_IN = F_W + 2 * RET_QW + 2 * RET_VW + 2 * DIFF_QW + DIFF_VW + N_BRANCH * D_MODEL

kernel_name = "hybrid_fourier_retention_diffattn_dit"


def split_points():
    sizes = (F_W, RET_QW, RET_QW, RET_VW, RET_VW, DIFF_QW, DIFF_QW, DIFF_VW)
    out, acc = [], 0
    for s in sizes:
        acc += s
        out.append(acc)
    return out


def rms_norm(x, g=None):
    xf = x.astype(jnp.float32)
    y = xf * lax.rsqrt(jnp.mean(xf * xf, axis=-1, keepdims=True) + EPS)
    if g is not None:
        y = y * g.astype(jnp.float32)
    return y.astype(x.dtype)


def modulate(h, shift, scale):
    return h * (1.0 + scale) + shift


def swiglu(h, w1, w3, w2):
    return (jax.nn.silu(h @ w1) * (h @ w3)) @ w2


def rope_angles(pos, dim):
    inv = ROPE_BASE ** (-jnp.arange(0, dim, 2, dtype=jnp.float32) / dim)
    return pos[:, None] * inv[None, :]


def apply_rope(x, ang):
    L, k = ang.shape
    shape = (1, L) + (1,) * (x.ndim - 3) + (k,)
    cos = jnp.cos(ang).reshape(shape)
    sin = jnp.sin(ang).reshape(shape)
    xf = x.astype(jnp.float32)
    x1, x2 = xf[..., :k], xf[..., k:]
    return jnp.concatenate([x1 * cos - x2 * sin, x1 * sin + x2 * cos], axis=-1).astype(x.dtype)


def axial_rope(x, ang_row, ang_col):
    half = x.shape[-1] // 2
    return jnp.concatenate([apply_rope(x[..., :half], ang_row), apply_rope(x[..., half:], ang_col)], axis=-1)


def fourier_mix(u):
    B, L, _ = u.shape
    ug = u.astype(jnp.float32).reshape(B, L, FOURIER_GROUPS, FOURIER_CH)
    y = jnp.fft.fft2(ug, axes=(1, 3), norm="ortho").real
    return y.reshape(B, L, F_W).astype(u.dtype)


def retention_scan(q, k, v, log_g, s0):
    B, L, H, dk = q.shape
    dv = v.shape[-1]
    C = RET_CHUNK
    n = L // C
    qc = q.astype(jnp.float32).reshape(B, n, C, H, dk).transpose(1, 0, 3, 2, 4)
    kc = k.astype(jnp.float32).reshape(B, n, C, H, dk).transpose(1, 0, 3, 2, 4)
    vc = v.astype(jnp.float32).reshape(B, n, C, H, dv).transpose(1, 0, 3, 2, 4)
    idx = jnp.arange(C, dtype=jnp.float32)
    diff = idx[:, None] - idx[None, :]
    mask = diff >= 0
    dmat = jnp.where(mask, jnp.exp(log_g[:, None, None] * jnp.where(mask, diff, 0.0)), 0.0)
    q_dec = jnp.exp(log_g[:, None] * (idx + 1.0))[None, :, :, None]
    k_dec = jnp.exp(log_g[:, None] * (C - 1.0 - idx))[None, :, :, None]
    chunk_dec = jnp.exp(log_g * C)[None, :, None, None]

    def step(s, inp):
        qi, ki, vi = inp
        inner = jnp.einsum('bhid,bhjd->bhij', qi, ki) * dmat
        o = jnp.einsum('bhij,bhjv->bhiv', inner, vi) + jnp.einsum('bhid,bhdv->bhiv', qi, s) * q_dec
        s = s * chunk_dec + jnp.einsum('bhjd,bhjv->bhdv', ki * k_dec, vi)
        return s, o

    _, o = lax.scan(step, s0.astype(jnp.float32), (qc, kc, vc))
    return o.transpose(1, 0, 3, 2, 4).reshape(B, L, H, dv)


def bidir_retention(q, k, v, log_g2, s0_f, s0_b):
    fwd = retention_scan(q, k, v, log_g2[0], s0_f)
    bwd = retention_scan(q[:, ::-1], k[:, ::-1], v[:, ::-1], log_g2[1], s0_b)[:, ::-1]
    return fwd + bwd


def context_states(k, v, log_g2):
    Lc = k.shape[1]
    j = jnp.arange(Lc, dtype=jnp.float32)
    w_f = jnp.exp(log_g2[0][:, None] * (Lc - 1.0 - j))
    w_b = jnp.exp(log_g2[1][:, None] * j)
    kf = k.astype(jnp.float32)
    vf = v.astype(jnp.float32)
    s_f = jnp.einsum('bjhd,hj,bjhv->bhdv', kf, w_f, vf)
    s_b = jnp.einsum('bjhd,hj,bjhv->bhdv', kf, w_b, vf)
    return s_f, s_b


def retention_readout(r, g):
    B, L, H, dv = r.shape
    gf = g.astype(jnp.float32).reshape(B, L, H, dv)
    y = rms_norm(r) * jax.nn.silu(gf)
    return y.reshape(B, L, H * dv).astype(g.dtype)


def diff_attend(q, k, v, lam):
    s = jnp.einsum('bqhmd,bkhmd->bhmqk', q, k).astype(jnp.float32) * (DIFF_QK ** -0.5)
    p = jax.nn.softmax(s, axis=-1)
    a = p[:, :, 0] - lam * p[:, :, 1]
    return jnp.einsum('bhqk,bkhv->bqhv', a.astype(v.dtype), v)


def diff_attend_blocked(q, k, v, lam):
    B, L = q.shape[:2]
    nb = L // Q_BLOCK
    qb = q.reshape((B, nb, Q_BLOCK) + q.shape[2:]).swapaxes(0, 1)
    o = lax.map(lambda blk: diff_attend(blk, k, v, lam), qb)
    return o.swapaxes(0, 1).reshape((B, L) + o.shape[3:])


def diff_readout(o, subln_w, lam_init):
    B, L, H, dv = o.shape
    y = rms_norm(o, subln_w) * (1.0 - lam_init)
    return y.reshape(B, L, H * dv)


def merge_branches(gate_logits, f_out, r_out, d_out, wbf, wbr, wbd, wo):
    g = jax.nn.sigmoid(gate_logits.astype(jnp.float32)).astype(f_out.dtype)
    gf, gr, gd = jnp.split(g, N_BRANCH, axis=-1)
    mixed = gf * (f_out @ wbf) + gr * (r_out @ wbr) + gd * (d_out @ wbd)
    return mixed @ wo


def token_mixer(h, hc, w_in_l, dec_logit, lam_vec, subln_w, wbf, wbr, wbd, wo, lam_init,
                ang_row, ang_col, ang_ret, with_ctx_out):
    B, L, _ = h.shape
    Lc = hc.shape[1]
    cuts = split_points()
    wf, wrq, wrk, wrv, wrg, wdq, wdk, wdv, wgt = jnp.split(w_in_l, cuts, axis=1)
    log_g2 = jax.nn.log_sigmoid(dec_logit.astype(jnp.float32))
    lv = lam_vec.astype(jnp.float32)
    lam = jnp.exp(jnp.sum(lv[0] * lv[1])) - jnp.exp(jnp.sum(lv[2] * lv[3])) + lam_init

    rk_c = (hc @ wrk).reshape(B, Lc, RET_HEADS, RET_QK) * (RET_QK ** -0.5)
    rv_c = (hc @ wrv).reshape(B, Lc, RET_HEADS, RET_V)
    s_f, s_b = context_states(rk_c, rv_c, log_g2)
    dk_c = (hc @ wdk).reshape(B, Lc, DIFF_HEADS, 2, DIFF_QK)
    dv_c = (hc @ wdv).reshape(B, Lc, DIFF_HEADS, DIFF_V)

    p = h @ w_in_l
    u_f, rq, rk, rv, rg, dq, dk, dv, gl = jnp.split(p, cuts, axis=-1)
    f_out = fourier_mix(u_f)
    rq = apply_rope(rq.reshape(B, L, RET_HEADS, RET_QK), ang_ret)
    rk = apply_rope(rk.reshape(B, L, RET_HEADS, RET_QK), ang_ret) * (RET_QK ** -0.5)
    rv = rv.reshape(B, L, RET_HEADS, RET_V)
    r_out = retention_readout(bidir_retention(rq, rk, rv, log_g2, s_f, s_b), rg)
    dq = axial_rope(dq.reshape(B, L, DIFF_HEADS, 2, DIFF_QK), ang_row, ang_col)
    dk = axial_rope(dk.reshape(B, L, DIFF_HEADS, 2, DIFF_QK), ang_row, ang_col)
    dv = dv.reshape(B, L, DIFF_HEADS, DIFF_V)
    k_all = jnp.concatenate([dk_c, dk], axis=1)
    v_all = jnp.concatenate([dv_c, dv], axis=1)
    d_out = diff_readout(diff_attend_blocked(dq, k_all, v_all, lam), subln_w, lam_init)
    y = merge_branches(gl, f_out, r_out, d_out, wbf, wbr, wbd, wo)

    yc = None
    if with_ctx_out:
        f_c = fourier_mix(hc @ wf)
        rq_c = (hc @ wrq).reshape(B, Lc, RET_HEADS, RET_QK)
        zeros = jnp.zeros((B, RET_HEADS, RET_QK, RET_V), jnp.float32)
        r_c = retention_readout(bidir_retention(rq_c, rk_c, rv_c, log_g2, zeros, zeros), hc @ wrg)
        dq_c = (hc @ wdq).reshape(B, Lc, DIFF_HEADS, 2, DIFF_QK)
        d_c = diff_readout(diff_attend(dq_c, dk_c, dv_c, lam), subln_w, lam_init)
        yc = merge_branches(hc @ wgt, f_c, r_c, d_c, wbf, wbr, wbd, wo)
    return y, yc


def setup_inputs(seed: int = 0) -> dict:
    key = jax.random.key(seed)
    ks = jax.random.split(key, 20)
    f32 = jnp.float32

    def nrm(k, shape, scale):
        return jax.random.normal(k, shape, f32) * scale

    base_logit = jnp.log(2.0 ** (5.0 + jnp.arange(RET_HEADS, dtype=f32)) - 1.0)
    return {
        "x": nrm(ks[0], (BATCH, SEQ, D_MODEL), 1.0),
        "c": nrm(ks[1], (BATCH, D_MODEL), 1.0),
        "ctx": nrm(ks[2], (BATCH, CTX_LEN, D_MODEL), 1.0),
        "c_ctx": nrm(ks[3], (D_MODEL,), 1.0),
        "w_ada": nrm(ks[4], (DEPTH, D_MODEL, N_MOD * D_MODEL), 0.5 * D_MODEL ** -0.5),
        "b_ada": nrm(ks[5], (DEPTH, N_MOD * D_MODEL), 0.02),
        "norm_g": 1.0 + nrm(ks[6], (DEPTH, 3, D_MODEL), 0.02),
        "ffn_w1": nrm(ks[7], (DEPTH, 2, D_MODEL, D_FF), D_MODEL ** -0.5),
        "ffn_w3": nrm(ks[8], (DEPTH, 2, D_MODEL, D_FF), D_MODEL ** -0.5),
        "ffn_w2": nrm(ks[9], (DEPTH, 2, D_FF, D_MODEL), D_FF ** -0.5),
        "w_in": nrm(ks[10], (DEPTH, D_MODEL, D_IN), D_MODEL ** -0.5),
        "ret_decay_logit": base_logit + nrm(ks[11], (DEPTH, 2, RET_HEADS), 0.01),
        "diff_lambda": nrm(ks[12], (DEPTH, 4, DIFF_QK), 0.1),
        "diff_subln": 1.0 + nrm(ks[13], (DEPTH, DIFF_V), 0.02),
        "w_branch_f": nrm(ks[14], (DEPTH, F_W, D_MODEL), F_W ** -0.5),
        "w_branch_r": nrm(ks[15], (DEPTH, RET_VW, D_MODEL), RET_VW ** -0.5),
        "w_branch_d": nrm(ks[16], (DEPTH, DIFF_VW, D_MODEL), DIFF_VW ** -0.5),
        "w_out": nrm(ks[17], (DEPTH, D_MODEL, D_MODEL), D_MODEL ** -0.5),
        "final_g": 1.0 + nrm(ks[18], (D_MODEL,), 0.02),
    }


def reference(x, c, ctx, c_ctx, w_ada, b_ada, norm_g, ffn_w1, ffn_w3, ffn_w2, w_in, ret_decay_logit,
              diff_lambda, diff_subln, w_branch_f, w_branch_r, w_branch_d, w_out, final_g):
    B, L, D = x.shape
    n_rows = L // GRID_W
    rows = jnp.repeat(jnp.arange(n_rows, dtype=jnp.float32), GRID_W)
    cols = jnp.tile(jnp.arange(GRID_W, dtype=jnp.float32), n_rows)
    ang_row = rope_angles(rows, DIFF_QK // 2)
    ang_col = rope_angles(cols, DIFF_QK // 2)
    ang_ret = rope_angles(jnp.arange(L, dtype=jnp.float32), RET_QK)
    sc = jax.nn.silu(c)
    scc = jax.nn.silu(c_ctx)
    xc = ctx
    for l in range(DEPTH):
        last = l == DEPTH - 1
        lam_init = 0.8 - 0.6 * math.exp(-0.3 * l)
        m = (sc @ w_ada[l] + b_ada[l]).reshape(B, N_MOD, 1, D)
        mc = (scc @ w_ada[l] + b_ada[l]).reshape(N_MOD, D)
        x = x + 0.5 * m[:, 2] * swiglu(modulate(rms_norm(x, norm_g[l, 0]), m[:, 0], m[:, 1]),
                                       ffn_w1[l, 0], ffn_w3[l, 0], ffn_w2[l, 0])
        xc = xc + 0.5 * mc[2] * swiglu(modulate(rms_norm(xc, norm_g[l, 0]), mc[0], mc[1]),
                                       ffn_w1[l, 0], ffn_w3[l, 0], ffn_w2[l, 0])
        h = modulate(rms_norm(x, norm_g[l, 1]), m[:, 3], m[:, 4])
        hc = modulate(rms_norm(xc, norm_g[l, 1]), mc[3], mc[4])
        y, yc = token_mixer(h, hc, w_in[l], ret_decay_logit[l], diff_lambda[l], diff_subln[l],
                            w_branch_f[l], w_branch_r[l], w_branch_d[l], w_out[l], lam_init,
                            ang_row, ang_col, ang_ret, not last)
        x = x + m[:, 5] * y
        x = x + 0.5 * m[:, 8] * swiglu(modulate(rms_norm(x, norm_g[l, 2]), m[:, 6], m[:, 7]),
                                       ffn_w1[l, 1], ffn_w3[l, 1], ffn_w2[l, 1])
        if not last:
            xc = xc + mc[5] * yc
            xc = xc + 0.5 * mc[8] * swiglu(modulate(rms_norm(xc, norm_g[l, 2]), mc[6], mc[7]),
                                           ffn_w1[l, 1], ffn_w3[l, 1], ffn_w2[l, 1])
    return rms_norm(x, final_g)
```

```python
import functools
import math

import numpy as np
import jax
import jax.numpy as jnp
from jax import lax
from jax.experimental import pallas as pl
from jax.experimental.pallas import tpu as pltpu

D_MODEL = 1024
DEPTH = 4
GRID_W = 64
D_FF = 2816
N_MOD = 9
FOURIER_GROUPS = 4
FOURIER_CH = 64
RET_HEADS = 6
RET_QK = 32
RET_V = 64
RET_CHUNK = 128
DIFF_HEADS = 6
DIFF_QK = 32
DIFF_V = 64
ROPE_BASE = 10000.0
EPS = 1e-6
F_W = FOURIER_GROUPS * FOURIER_CH
RET_QW = RET_HEADS * RET_QK
RET_VW = RET_HEADS * RET_V
DIFF_QW = DIFF_HEADS * 2 * DIFF_QK
DIFF_VW = DIFF_HEADS * DIFF_V
GATE_W = 3 * D_MODEL

LANES = 128
VMEM_LIMIT_BYTES = 56 * 1024 * 1024

TOKEN_TILE = 256
DFT_N1 = 128

BF16 = jnp.bfloat16
F32 = jnp.float32
LOG2E = math.log2(math.e)


def _dot(a, b):
    return jnp.dot(a, b, preferred_element_type=F32)


def _split_bf16(x):
    hi = x.astype(BF16)
    lo = (x - hi.astype(F32)).astype(BF16)
    return hi, lo


def _dot3(a, b):
    ah, al = _split_bf16(a)
    bh, bl = _split_bf16(b)
    return _dot(ah, bh) + _dot(al, bh) + _dot(ah, bl)


def _norm_mod(x, g, shift, scale):
    ms = jnp.mean(x * x, axis=-1, keepdims=True)
    y = x * lax.rsqrt(ms + EPS) * g
    return y * (1.0 + scale) + shift


def _sigmoid(x):
    return 1.0 / (1.0 + jnp.exp(-x))


def _const_spec(shape):
    nd = len(shape)
    return pl.BlockSpec(shape, lambda *_: (0,) * nd, pipeline_mode=pl.Buffered(1))


def _params(sem):
    return pltpu.CompilerParams(dimension_semantics=sem, vmem_limit_bytes=VMEM_LIMIT_BYTES)


def _ada_kernel(cc_ref, w_ref, b_ref, o_ref):
    cc = cc_ref[...]
    s = cc * _sigmoid(cc)
    o_ref[0] = _dot3(s, w_ref[0]) + b_ref[0]


def _ada_call(cc, w_ada, b_ada):
    depth, d, n = w_ada.shape
    tn = 1152
    return pl.pallas_call(
        _ada_kernel,
        out_shape=jax.ShapeDtypeStruct((depth, 8, n), F32),
        grid=(depth, n // tn),
        in_specs=[
            pl.BlockSpec((8, d), lambda l, j: (0, 0)),
            pl.BlockSpec((1, d, tn), lambda l, j: (l, 0, j)),
            pl.BlockSpec((1, 1, tn), lambda l, j: (l, 0, j)),
        ],
        out_specs=pl.BlockSpec((1, 8, tn), lambda l, j: (l, 0, j)),
        compiler_params=_params(("arbitrary", "arbitrary")),
        name="adaln",
    )(cc, w_ada, b_ada.reshape(depth, 1, n))


def _mod_spec(n_lat_tiles):
    return pl.BlockSpec((1, N_MOD, D_MODEL), lambda i: (jnp.where(i >= n_lat_tiles, 1, 0), 0, 0))


def _ffn_kernel(x_ref, mod_ref, g_ref, w1_ref, w3_ref, w2_ref, o_ref, *, base):
    x = x_ref[...]
    shift = mod_ref[0, base:base + 1, :]
    scale = mod_ref[0, base + 1:base + 2, :]
    gate = mod_ref[0, base + 2:base + 3, :]
    hb = _norm_mod(x, g_ref[...], shift, scale).astype(BF16)
    a = _dot(hb, w1_ref[...])
    b = _dot(hb, w3_ref[...])
    u = (a * _sigmoid(a) * b).astype(BF16)
    o_ref[...] = x + (0.5 * gate) * _dot(u, w2_ref[...])


def _ffn_call(x, mods, g, w1, w3, w2, base, n_lat_tiles, n_tiles):
    tm = TOKEN_TILE
    return pl.pallas_call(
        functools.partial(_ffn_kernel, base=base),
        out_shape=jax.ShapeDtypeStruct((n_tiles * tm, D_MODEL), F32),
        grid=(n_tiles,),
        in_specs=[
            pl.BlockSpec((tm, D_MODEL), lambda i: (i, 0)),
            _mod_spec(n_lat_tiles),
            _const_spec((1, D_MODEL)),
            _const_spec((D_MODEL, D_FF)),
            _const_spec((D_MODEL, D_FF)),
            _const_spec((D_FF, D_MODEL)),
        ],
        out_specs=pl.BlockSpec((tm, D_MODEL), lambda i: (i, 0)),
        compiler_params=_params(("arbitrary",)),
        name="ffn",
    )(x, mods, g.reshape(1, D_MODEL), w1, w3, w2)


_C_F, _C_RQ, _C_RK, _C_RV, _C_RG = 0, 256, 512, 768, 1152
_C_DQ, _C_DK, _C_DV = 1536, 1920, 2304
_C_RQR, _C_RKR, _C_DQR, _C_DKR = 2688, 2944, 3200, 3584
PROJ_W = 3968


def _rot_perm(width, block):
    half = block // 2
    idx = np.arange(width)
    within = idx % block
    perm = np.where(within < half, idx + half, idx - half)
    sign = np.where(within < half, -1.0, 1.0).astype(np.float32)
    return perm, sign


def _build_proj_weight(w_in_l):
    cuts = np.cumsum([F_W, RET_QW, RET_QW, RET_VW, RET_VW, DIFF_QW, DIFF_QW, DIFF_VW])
    wf, wrq, wrk, wrv, wrg, wdq, wdk, wdv, wgt = jnp.split(w_in_l, cuts, axis=1)
    pr, sr = _rot_perm(RET_QW, RET_QK)
    pd, sd = _rot_perm(DIFF_QW, DIFF_QK // 2)
    z64 = jnp.zeros((D_MODEL, 64), w_in_l.dtype)
    parts = [wf, wrq, z64, wrk, z64, wrv, wrg, wdq, wdk, wdv,
             wrq[:, pr] * sr, z64, wrk[:, pr] * sr, z64, wdq[:, pd] * sd, wdk[:, pd] * sd]
    return jnp.concatenate(parts, axis=1).astype(BF16), wgt.astype(BF16)


def _rope_tables(seq, ctx_len):
    pos = jnp.arange(seq, dtype=F32)
    inv_r = ROPE_BASE ** (-jnp.arange(0, RET_QK, 2, dtype=F32) / RET_QK)
    ang_r = pos[:, None] * inv_r[None, :]
    cos_r = jnp.tile(jnp.cos(ang_r), (1, 2 * RET_HEADS))
    sin_r = jnp.tile(jnp.sin(ang_r), (1, 2 * RET_HEADS))
    rows = jnp.repeat(jnp.arange(seq // GRID_W, dtype=F32), GRID_W)
    cols = jnp.tile(jnp.arange(GRID_W, dtype=F32), seq // GRID_W)
    dim = DIFF_QK // 2
    inv_d = ROPE_BASE ** (-jnp.arange(0, dim, 2, dtype=F32) / dim)
    a_row = rows[:, None] * inv_d[None, :]
    a_col = cols[:, None] * inv_d[None, :]
    cos_hm = jnp.concatenate([jnp.cos(a_row)] * 2 + [jnp.cos(a_col)] * 2, axis=1)
    sin_hm = jnp.concatenate([jnp.sin(a_row)] * 2 + [jnp.sin(a_col)] * 2, axis=1)
    cos_d = jnp.tile(cos_hm, (1, 2 * DIFF_HEADS))
    sin_d = jnp.tile(sin_hm, (1, 2 * DIFF_HEADS))

    def with_ctx(t, fill):
        return jnp.concatenate([t, jnp.full((ctx_len, t.shape[1]), fill, F32)], axis=0)

    return with_ctx(cos_r, 1.0), with_ctx(sin_r, 0.0), with_ctx(cos_d, 1.0), with_ctx(sin_d, 0.0)


def _proj_kernel(x_ref, mod_ref, g_ref, w_ref, cr_ref, sr_ref, cd_ref, sd_ref,
                 uf_ref, rq_ref, rk_ref, rv_ref, rg_ref, dq_ref, dk_ref, dv_ref):
    x = x_ref[...]
    hb = _norm_mod(x, g_ref[...], mod_ref[0, 3:4, :], mod_ref[0, 4:5, :]).astype(BF16)
    p = _dot(hb, w_ref[...])
    cr, sr, cd, sd = cr_ref[...], sr_ref[...], cd_ref[...], sd_ref[...]
    uf_ref[...] = p[:, _C_F:_C_F + F_W]
    rq = p[:, _C_RQ:_C_RQ + RET_QW] * cr + p[:, _C_RQR:_C_RQR + RET_QW] * sr
    rk = p[:, _C_RK:_C_RK + RET_QW] * cr + p[:, _C_RKR:_C_RKR + RET_QW] * sr
    rq_ref[...] = rq.astype(BF16)
    rk_ref[...] = (rk * (RET_QK ** -0.5)).astype(BF16)
    rv_ref[...] = p[:, _C_RV:_C_RV + RET_VW].astype(BF16)
    rg_ref[...] = p[:, _C_RG:_C_RG + RET_VW]
    dq = p[:, _C_DQ:_C_DQ + DIFF_QW] * cd + p[:, _C_DQR:_C_DQR + DIFF_QW] * sd
    dk = p[:, _C_DK:_C_DK + DIFF_QW] * cd + p[:, _C_DKR:_C_DKR + DIFF_QW] * sd
    dq_ref[...] = (dq * ((DIFF_QK ** -0.5) * LOG2E)).astype(BF16)
    dk_ref[...] = dk.astype(BF16)
    dv_ref[...] = p[:, _C_DV:_C_DV + DIFF_VW].astype(BF16)


def _proj_call(x, mods, g, w_aug, tables, n_lat_tiles, n_tiles):
    tm = TOKEN_TILE
    t = n_tiles * tm

    def row_spec(w):
        return pl.BlockSpec((tm, w), lambda i: (i, 0))

    out_shape = (
        jax.ShapeDtypeStruct((t, F_W), F32),
        jax.ShapeDtypeStruct((t, RET_QW), BF16),
        jax.ShapeDtypeStruct((t, RET_QW), BF16),
        jax.ShapeDtypeStruct((t, RET_VW), BF16),
        jax.ShapeDtypeStruct((t, RET_VW), F32),
        jax.ShapeDtypeStruct((t, DIFF_QW), BF16),
        jax.ShapeDtypeStruct((t, DIFF_QW), BF16),
        jax.ShapeDtypeStruct((t, DIFF_VW), BF16),
    )
    return pl.pallas_call(
        _proj_kernel,
        out_shape=out_shape,
        grid=(n_tiles,),
        in_specs=[
            row_spec(D_MODEL), _mod_spec(n_lat_tiles), _const_spec((1, D_MODEL)),
            _const_spec((D_MODEL, PROJ_W)),
            row_spec(RET_QW), row_spec(RET_QW), row_spec(DIFF_QW), row_spec(DIFF_QW),
        ],
        out_specs=tuple(row_spec(s.shape[1]) for s in out_shape),
        compiler_params=_params(("arbitrary",)),
        name="mixer_proj",
    )(x, mods, g.reshape(1, D_MODEL), w_aug, *tables)


def _dft_mats(n):
    k = np.arange(n)
    ang = 2.0 * np.pi * ((k[:, None] * k[None, :]) % n) / n
    return np.cos(ang).astype(np.float32), np.sin(ang).astype(np.float32)


def _channel_dft_mats():
    c, s = _dft_mats(FOURIER_CH)
    eye = np.eye(FOURIER_GROUPS, dtype=np.float32)
    return np.kron(eye, c), np.kron(eye, s)


def _fourier_stage1_kernel(x_ref, cc_ref, sc_ref, c1_ref, s1_ref, twc_ref, tws_ref, tr_ref, ti_ref, *, nb):
    cc, sc = cc_ref[...], sc_ref[...]
    zr, zi = [], []
    for j in range(nb):
        u = x_ref[:, j * F_W:(j + 1) * F_W]
        zr.append(_dot3(u, cc))
        zi.append(-_dot3(u, sc))
    zr = jnp.concatenate(zr, axis=1) if nb > 1 else zr[0]
    zi = jnp.concatenate(zi, axis=1) if nb > 1 else zi[0]
    c1, s1 = c1_ref[...], s1_ref[...]
    tr = _dot3(c1, zr) + _dot3(s1, zi)
    ti = _dot3(c1, zi) - _dot3(s1, zr)
    twc, tws = twc_ref[...], tws_ref[...]
    tr_ref[...] = tr * twc + ti * tws
    ti_ref[...] = ti * twc - tr * tws


def _fourier_stage2_kernel(tr_ref, ti_ref, c2_ref, s2_ref, o_ref, *, kb):
    c2, s2 = c2_ref[...], s2_ref[...]
    for j in range(kb):
        o_ref[j] = _dot3(c2, tr_ref[j]) + _dot3(s2, ti_ref[j])


def _fourier_latent(u, twc, tws):
    seq = u.shape[0]
    n1, n2 = DFT_N1, seq // DFT_N1
    nb = min(8, n2)
    kb = 8
    cc, sc = _channel_dft_mats()
    c1, s1 = _dft_mats(n1)
    c2, s2 = _dft_mats(n2)
    x2 = u.reshape(n1, n2 * F_W)
    blk = pl.BlockSpec((n1, nb * F_W), lambda i: (0, i))
    tr, ti = pl.pallas_call(
        functools.partial(_fourier_stage1_kernel, nb=nb),
        out_shape=(jax.ShapeDtypeStruct((n1, n2 * F_W), F32),) * 2,
        grid=(n2 // nb,),
        in_specs=[blk, _const_spec((F_W, F_W)), _const_spec((F_W, F_W)),
                  _const_spec((n1, n1)), _const_spec((n1, n1)), blk, blk],
        out_specs=(blk, blk),
        compiler_params=_params(("arbitrary",)),
        name="fourier_stage1",
    )(x2, cc, sc, c1, s1, twc, tws)
    tr3 = tr.reshape(n1, n2, F_W)
    ti3 = ti.reshape(n1, n2, F_W)
    blk3 = pl.BlockSpec((kb, n2, F_W), lambda i: (i, 0, 0))
    o3 = pl.pallas_call(
        functools.partial(_fourier_stage2_kernel, kb=kb),
        out_shape=jax.ShapeDtypeStruct((n1, n2, F_W), F32),
        grid=(n1 // kb,),
        in_specs=[blk3, blk3, _const_spec((n2, n2)), _const_spec((n2, n2))],
        out_specs=blk3,
        compiler_params=_params(("arbitrary",)),
        name="fourier_stage2",
    )(tr3, ti3, c2, s2)
    return jnp.transpose(o3, (1, 0, 2)).reshape(seq, F_W)


def _twiddles(seq):
    n1, n2 = DFT_N1, seq // DFT_N1
    k1 = jnp.arange(n1, dtype=jnp.int32)[:, None]
    m2 = jnp.arange(n2, dtype=jnp.int32)[None, :]
    ang = (2.0 * math.pi / seq) * ((k1 * m2) % seq).astype(F32)
    scale = 1.0 / math.sqrt(seq * FOURIER_CH)
    twc = jnp.repeat(jnp.cos(ang) * scale, F_W, axis=1)
    tws = jnp.repeat(jnp.sin(ang) * scale, F_W, axis=1)
    return twc, tws


def _fourier_ctx_kernel(u_ref, cc_ref, sc_ref, cl_ref, sl_ref, o_ref, *, scale):
    u = u_ref[...]
    a = _dot3(u, cc_ref[...])
    b = _dot3(u, sc_ref[...])
    o_ref[...] = (_dot3(cl_ref[...], a) - _dot3(sl_ref[...], b)) * scale


def _fourier_ctx(u):
    n = u.shape[0]
    cc, sc = _channel_dft_mats()
    cl, sl = _dft_mats(n)
    return pl.pallas_call(
        functools.partial(_fourier_ctx_kernel, scale=1.0 / math.sqrt(n * FOURIER_CH)),
        out_shape=jax.ShapeDtypeStruct((n, F_W), F32),
        name="fourier_ctx",
    )(u, cc, sc, cl, sl)


def _ret_dir(q, kt, v, s_ref, d_ref, qdec, kdec, cd, bdmask, hmask_ref, vmask_ref):
    pieces = []
    vparts = []
    for h in range(RET_HEADS):
        qh = q * hmask_ref[h]
        pieces.append((_dot(qh, kt) * d_ref[h]).astype(BF16))
        vparts.append(v * vmask_ref[h])
    inner = jnp.concatenate(pieces, axis=1)
    vbd = jnp.concatenate(vparts, axis=0)
    s = s_ref[...]
    o = _dot(inner, vbd) + _dot(q, s.astype(BF16)) * qdec
    kd = (kt.astype(F32) * kdec).astype(BF16)
    s_ref[...] = s * cd + bdmask * _dot(kd, v)
    return o


def _ret_kernel(logg_ref, lgv_ref, lgk_ref, bdmask_ref, hmask_ref, vmask_ref,
                qf_ref, ktf_ref, vf_ref, qb_ref, ktb_ref, vb_ref,
                of_ref, ob_ref,
                sf_ref, sb_ref, df_ref, db_ref, qdf_ref, qdb_ref, kdf_ref, kdb_ref, cdf_ref, cdb_ref):
    c = RET_CHUNK

    @pl.when(pl.program_id(0) == 0)
    def _():
        sf_ref[...] = jnp.zeros_like(sf_ref)
        sb_ref[...] = jnp.zeros_like(sb_ref)
        ii = lax.broadcasted_iota(jnp.int32, (c, c), 0).astype(F32)
        jj = lax.broadcasted_iota(jnp.int32, (c, c), 1).astype(F32)
        for h in range(RET_HEADS):
            df_ref[h] = jnp.where(ii >= jj, jnp.exp(logg_ref[0, h] * jnp.maximum(ii - jj, 0.0)), 0.0)
            db_ref[h] = jnp.where(jj >= ii, jnp.exp(logg_ref[1, h] * jnp.maximum(jj - ii, 0.0)), 0.0)
        ri = lax.broadcasted_iota(jnp.int32, (c, RET_VW), 0).astype(F32)
        qdf_ref[...] = jnp.exp(lgv_ref[0] * (ri + 1.0))
        qdb_ref[...] = jnp.exp(lgv_ref[1] * (c - ri))
        cj = lax.broadcasted_iota(jnp.int32, (RET_QW, c), 1).astype(F32)
        kdf_ref[...] = jnp.exp(lgk_ref[0] * (c - 1.0 - cj))
        kdb_ref[...] = jnp.exp(lgk_ref[1] * cj)
        cdf_ref[...] = jnp.exp(lgv_ref[0] * float(c))
        cdb_ref[...] = jnp.exp(lgv_ref[1] * float(c))

    bdmask = bdmask_ref[...]
    of_ref[...] = _ret_dir(qf_ref[...], ktf_ref[...], vf_ref[...], sf_ref, df_ref,
                           qdf_ref[...], kdf_ref[...], cdf_ref[...], bdmask, hmask_ref, vmask_ref)
    ob_ref[...] = _ret_dir(qb_ref[...], ktb_ref[...], vb_ref[...], sb_ref, db_ref,
                           qdb_ref[...], kdb_ref[...], cdb_ref[...], bdmask, hmask_ref, vmask_ref)


def _ret_masks():
    hm = np.zeros((RET_HEADS, 1, RET_QW), np.float32)
    vm = np.zeros((RET_HEADS, 1, RET_VW), np.float32)
    bd = np.zeros((RET_QW, RET_VW), np.float32)
    for h in range(RET_HEADS):
        hm[h, 0, h * RET_QK:(h + 1) * RET_QK] = 1.0
        vm[h, 0, h * RET_V:(h + 1) * RET_V] = 1.0
        bd[h * RET_QK:(h + 1) * RET_QK, h * RET_V:(h + 1) * RET_V] = 1.0
    return jnp.asarray(bd), jnp.asarray(hm, BF16), jnp.asarray(vm, BF16)


def _retention_call(rq, rkt, rv, log_g2, n_lat, n_ctx):
    c = RET_CHUNK
    n = n_lat + n_ctx
    t = n * c

    def fwd(i):
        return jnp.where(i < n_ctx, n_lat + i, i - n_ctx)

    def bwd(i):
        return jnp.where(i < n_ctx, n - 1 - i, n_lat + n_ctx - 1 - i)

    lgv = jnp.repeat(log_g2, RET_V, axis=1).reshape(2, 1, RET_VW)
    lgk = jnp.broadcast_to(jnp.repeat(log_g2, RET_QK, axis=1)[:, :, None], (2, RET_QW, c))
    bd, hm, vm = _ret_masks()

    def specs(ix):
        return [pl.BlockSpec((c, RET_QW), lambda i: (ix(i), 0)),
                pl.BlockSpec((RET_QW, c), lambda i: (0, ix(i))),
                pl.BlockSpec((c, RET_VW), lambda i: (ix(i), 0))]

    vm_f32 = pltpu.VMEM
    return pl.pallas_call(
        _ret_kernel,
        out_shape=(jax.ShapeDtypeStruct((t, RET_VW), F32),) * 2,
        grid=(n,),
        in_specs=[pl.BlockSpec(memory_space=pltpu.SMEM),
                  _const_spec((2, 1, RET_VW)), _const_spec((2, RET_QW, c)),
                  _const_spec((RET_QW, RET_VW)), _const_spec((RET_HEADS, 1, RET_QW)),
                  _const_spec((RET_HEADS, 1, RET_VW))] + specs(fwd) + specs(bwd),
        out_specs=(pl.BlockSpec((c, RET_VW), lambda i: (fwd(i), 0)),
                   pl.BlockSpec((c, RET_VW), lambda i: (bwd(i), 0))),
        scratch_shapes=[vm_f32((RET_QW, RET_VW), F32), vm_f32((RET_QW, RET_VW), F32),
                        vm_f32((RET_HEADS, c, c), F32), vm_f32((RET_HEADS, c, c), F32),
                        vm_f32((c, RET_VW), F32), vm_f32((c, RET_VW), F32),
                        vm_f32((RET_QW, c), F32), vm_f32((RET_QW, c), F32),
                        vm_f32((1, RET_VW), F32), vm_f32((1, RET_VW), F32)],
        compiler_params=_params(("arbitrary",)),
        name="retention",
    )(log_g2, lgv, lgk, bd, hm, vm, rq, rkt, rv, rq, rkt, rv)


def _attn_kernel(lam_ref, q_ref, kt_ref, v_ref, subln_ref, o_ref, qm_ref, m_ref, acc_ref, *, out_scale):
    ki = pl.program_id(2)

    @pl.when(ki == 0)
    def _():
        q = q_ref[...]
        lane = lax.broadcasted_iota(jnp.int32, q.shape, 1)
        for j in range(4):
            lo = j * DIFF_QK
            qm_ref[j] = jnp.where((lane >= lo) & (lane < lo + DIFF_QK), q, jnp.zeros_like(q))
        m_ref[...] = jnp.full(m_ref.shape, -jnp.inf, F32)
        acc_ref[...] = jnp.zeros_like(acc_ref)

    kt = kt_ref[...]
    for j in range(4):
        s = _dot(qm_ref[j], kt)
        m_prev = m_ref[j]
        m_new = jnp.maximum(m_prev, jnp.max(s, axis=1, keepdims=True))
        p = jnp.exp2(s - m_new)
        alpha = jnp.exp2(m_prev - m_new)
        acc_ref[j] = alpha * acc_ref[j] + _dot(p.astype(BF16), v_ref[j // 2])
        m_ref[j] = m_new

    @pl.when(ki == pl.num_programs(2) - 1)
    def _():
        lam = lam_ref[0]
        outs = []
        for hl in range(2):
            a1 = acc_ref[2 * hl]
            a2 = acc_ref[2 * hl + 1]
            o = (a1[:, :DIFF_V] / a1[:, DIFF_V:DIFF_V + 1]
                 - lam * (a2[:, :DIFF_V] / a2[:, DIFF_V:DIFF_V + 1]))
            ms = jnp.mean(o * o, axis=-1, keepdims=True)
            outs.append(o * lax.rsqrt(ms + EPS) * subln_ref[...] * out_scale)
        o_ref[...] = jnp.concatenate(outs, axis=1)


def _attn_call(dq, kt, v_aug, lam, subln, lam_init, n_q, tq, tk):
    t_k = kt.shape[1]
    return pl.pallas_call(
        functools.partial(_attn_kernel, out_scale=1.0 - lam_init),
        out_shape=jax.ShapeDtypeStruct((n_q, DIFF_VW), F32),
        grid=(DIFF_HEADS // 2, n_q // tq, t_k // tk),
        in_specs=[
            pl.BlockSpec(memory_space=pltpu.SMEM),
            pl.BlockSpec((tq, LANES), lambda h, i, k: (i, h)),
            pl.BlockSpec((LANES, tk), lambda h, i, k: (h, k)),
            pl.BlockSpec((2, tk, LANES), lambda h, i, k: (h, k, 0)),
            pl.BlockSpec((1, DIFF_V), lambda h, i, k: (0, 0)),
        ],
        out_specs=pl.BlockSpec((tq, LANES), lambda h, i, k: (i, h)),
        scratch_shapes=[pltpu.VMEM((4, tq, LANES), BF16),
                        pltpu.VMEM((4, tq, 1), F32),
                        pltpu.VMEM((4, tq, LANES), F32)],
        compiler_params=_params(("arbitrary", "arbitrary", "arbitrary")),
        name="diff_attn",
    )(lam, dq, kt, v_aug, subln.reshape(1, DIFF_V))


def _merge_kernel(x_ref, mod_ref, g_ref, f_ref, of_ref, ob_ref, rg_ref, d_ref,
                  wgt_ref, wbf_ref, wbr_ref, wbd_ref, wo_ref, bd_ref, o_ref):
    x = x_ref[...]
    hb = _norm_mod(x, g_ref[...], mod_ref[0, 3:4, :], mod_ref[0, 4:5, :]).astype(BF16)
    gates = _sigmoid(_dot(hb, wgt_ref[...]))
    r = of_ref[...] + ob_ref[...]
    rr_hi, rr_lo = _split_bf16(r * r)
    ms = _dot(rr_hi, bd_ref[...]) + _dot(rr_lo, bd_ref[...])
    rg = rg_ref[...]
    yr = r * lax.rsqrt(ms + EPS) * (rg * _sigmoid(rg))
    mixed = (gates[:, :D_MODEL] * _dot(f_ref[...].astype(BF16), wbf_ref[...])
             + gates[:, D_MODEL:2 * D_MODEL] * _dot(yr.astype(BF16), wbr_ref[...])
             + gates[:, 2 * D_MODEL:] * _dot(d_ref[...].astype(BF16), wbd_ref[...]))
    y = _dot(mixed.astype(BF16), wo_ref[...])
    o_ref[...] = x + mod_ref[0, 5:6, :] * y


def _merge_call(x, mods, g, f_out, o_f, o_b, rg, d_out, wgt, wbf, wbr, wbd, wo, n_lat_tiles, n_tiles):
    tm = TOKEN_TILE
    bd = np.kron(np.eye(RET_HEADS, dtype=np.float32), np.full((RET_V, RET_V), 1.0 / RET_V, np.float32))

    def row_spec(w):
        return pl.BlockSpec((tm, w), lambda i: (i, 0))

    return pl.pallas_call(
        _merge_kernel,
        out_shape=jax.ShapeDtypeStruct((n_tiles * tm, D_MODEL), F32),
        grid=(n_tiles,),
        in_specs=[
            row_spec(D_MODEL), _mod_spec(n_lat_tiles), _const_spec((1, D_MODEL)),
            row_spec(F_W), row_spec(RET_VW), row_spec(RET_VW), row_spec(RET_VW), row_spec(DIFF_VW),
            _const_spec((D_MODEL, GATE_W)), _const_spec((F_W, D_MODEL)), _const_spec((RET_VW, D_MODEL)),
            _const_spec((DIFF_VW, D_MODEL)), _const_spec((D_MODEL, D_MODEL)), _const_spec((RET_VW, RET_VW)),
        ],
        out_specs=row_spec(D_MODEL),
        compiler_params=_params(("arbitrary",)),
        name="merge",
    )(x, mods, g.reshape(1, D_MODEL), f_out, o_f, o_b, rg, d_out, wgt, wbf, wbr, wbd, wo,
      jnp.asarray(bd, BF16))


def _final_kernel(x_ref, g_ref, o_ref):
    x = x_ref[...]
    ms = jnp.mean(x * x, axis=-1, keepdims=True)
    o_ref[...] = x * lax.rsqrt(ms + EPS) * g_ref[...]


def _final_call(x, g, n_tiles):
    tm = TOKEN_TILE
    return pl.pallas_call(
        _final_kernel,
        out_shape=jax.ShapeDtypeStruct((n_tiles * tm, D_MODEL), F32),
        grid=(n_tiles,),
        in_specs=[pl.BlockSpec((tm, D_MODEL), lambda i: (i, 0)), _const_spec((1, D_MODEL))],
        out_specs=pl.BlockSpec((tm, D_MODEL), lambda i: (i, 0)),
        compiler_params=_params(("arbitrary",)),
        name="final_norm",
    )(x, g.reshape(1, D_MODEL))


def _pick_tile(n, candidates):
    for c in candidates:
        if n % c == 0:
            return c
    raise ValueError(f"no tile for {n}")


def kernel(x, c, ctx, c_ctx, w_ada, b_ada, norm_g, ffn_w1, ffn_w3, ffn_w2, w_in, ret_decay_logit,
           diff_lambda, diff_subln, w_branch_f, w_branch_r, w_branch_d, w_out, final_g):
    batch, seq, d = x.shape
    ctx_len = ctx.shape[1]
    assert batch == 1 and d == D_MODEL
    assert seq % (DFT_N1 * 8) == 0 and seq % TOKEN_TILE == 0 and ctx_len % TOKEN_TILE == 0
    total = seq + ctx_len
    n_lat_tiles = seq // TOKEN_TILE
    n_tiles = total // TOKEN_TILE
    n_lat_chunks = seq // RET_CHUNK
    n_ctx_chunks = ctx_len // RET_CHUNK

    cc = jnp.zeros((8, D_MODEL), F32).at[0].set(c[0]).at[1].set(c_ctx)
    mods_all = _ada_call(cc, w_ada, b_ada)[:, :2].reshape(DEPTH, 2, N_MOD, D_MODEL)

    tables = _rope_tables(seq, ctx_len)
    twc, tws = _twiddles(seq)
    log_g2_all = jax.nn.log_sigmoid(ret_decay_logit.astype(F32))
    lv = diff_lambda.astype(F32)
    ones_cols = jnp.ones((DIFF_HEADS, total, LANES - DIFF_V), BF16)

    tq = _pick_tile(seq, (512, 256, 128))
    tk = _pick_tile(total, (1280, 640, 256, 128))

    xs = jnp.concatenate([x[0], ctx[0]], axis=0)
    for l in range(DEPTH):
        last = l == DEPTH - 1
        lam_init = 0.8 - 0.6 * math.exp(-0.3 * l)
        mods = mods_all[l]
        lam = (jnp.exp(jnp.sum(lv[l, 0] * lv[l, 1])) - jnp.exp(jnp.sum(lv[l, 2] * lv[l, 3]))
               + lam_init).reshape(1)
        w_aug, wgt = _build_proj_weight(w_in[l])

        xs = _ffn_call(xs, mods, norm_g[l, 0], ffn_w1[l, 0].astype(BF16), ffn_w3[l, 0].astype(BF16),
                       ffn_w2[l, 0].astype(BF16), 0, n_lat_tiles, n_tiles)

        uf, rq, rk, rv, rg, dq, dk, dv = _proj_call(xs, mods, norm_g[l, 1], w_aug, tables,
                                                     n_lat_tiles, n_tiles)

        f_lat = _fourier_latent(uf[:seq], twc, tws)
        f_ctx = jnp.zeros((ctx_len, F_W), F32) if last else _fourier_ctx(uf[seq:])
        f_out = jnp.concatenate([f_lat, f_ctx], axis=0)

        o_f, o_b = _retention_call(rq, rk.T, rv, log_g2_all[l], n_lat_chunks, n_ctx_chunks)

        kt = dk.T
        v_aug = jnp.concatenate(
            [dv.reshape(total, DIFF_HEADS, DIFF_V).transpose(1, 0, 2), ones_cols], axis=2)
        d_lat = _attn_call(dq, kt, v_aug, lam, diff_subln[l], lam_init, seq, tq, tk)
        if last:
            d_ctx = jnp.zeros((ctx_len, DIFF_VW), F32)
        else:
            d_ctx = _attn_call(dq[seq:], kt[:, seq:], v_aug[:, seq:], lam, diff_subln[l], lam_init,
                               ctx_len, ctx_len, ctx_len)
        d_out = jnp.concatenate([d_lat, d_ctx], axis=0)

        xs = _merge_call(xs, mods, norm_g[l, 1], f_out, o_f, o_b, rg, d_out, wgt,
                         w_branch_f[l].astype(BF16), w_branch_r[l].astype(BF16),
                         w_branch_d[l].astype(BF16), w_out[l].astype(BF16), n_lat_tiles, n_tiles)

        xs = _ffn_call(xs, mods, norm_g[l, 2], ffn_w1[l, 1].astype(BF16), ffn_w3[l, 1].astype(BF16),
                       ffn_w2[l, 1].astype(BF16), 6, n_lat_tiles, n_tiles)

    out = _final_call(xs, final_g, n_lat_tiles)
    return out.reshape(1, seq, D_MODEL)
```

```python
import functools
import math

import numpy as np
import jax
import jax.numpy as jnp
from jax import lax
from jax.experimental import pallas as pl
from jax.experimental.pallas import tpu as pltpu

D_MODEL = 1024
DEPTH = 4
GRID_W = 64
D_FF = 2816
N_MOD = 9
FOURIER_GROUPS = 4
FOURIER_CH = 64
RET_HEADS = 6
RET_QK = 32
RET_V = 64
RET_CHUNK = 128
DIFF_HEADS = 6
DIFF_QK = 32
DIFF_V = 64
ROPE_BASE = 10000.0
EPS = 1e-6
F_W = FOURIER_GROUPS * FOURIER_CH
RET_QW = RET_HEADS * RET_QK
RET_VW = RET_HEADS * RET_V
DIFF_QW = DIFF_HEADS * 2 * DIFF_QK
DIFF_VW = DIFF_HEADS * DIFF_V
GATE_W = 3 * D_MODEL

LANES = 128
VMEM_LIMIT_BYTES = 56 * 1024 * 1024

TOKEN_TILE = 256
DFT_N1 = 128

BF16 = jnp.bfloat16
F32 = jnp.float32
LOG2E = math.log2(math.e)


def _dot(a, b):
    return jnp.dot(a, b, preferred_element_type=F32)


def _split_bf16(x):
    hi = x.astype(BF16)
    lo = (x - hi.astype(F32)).astype(BF16)
    return hi, lo


def _dot3(a, b):
    ah, al = _split_bf16(a)
    bh, bl = _split_bf16(b)
    return _dot(ah, bh) + _dot(al, bh) + _dot(ah, bl)


def _norm_mod(x, g, shift, scale):
    ms = jnp.mean(x * x, axis=-1, keepdims=True)
    y = x * lax.rsqrt(ms + EPS) * g
    return y * (1.0 + scale) + shift


def _sigmoid(x):
    return 1.0 / (1.0 + jnp.exp(-x))


def _const_spec(shape):
    nd = len(shape)
    return pl.BlockSpec(shape, lambda *_: (0,) * nd, pipeline_mode=pl.Buffered(1))


def _params(sem):
    return pltpu.CompilerParams(dimension_semantics=sem, vmem_limit_bytes=VMEM_LIMIT_BYTES)


def _ada_kernel(cc_ref, w_ref, b_ref, o_ref):
    cc = cc_ref[...]
    s = cc * _sigmoid(cc)
    o_ref[0] = _dot3(s, w_ref[0]) + b_ref[0]


def _ada_call(cc, w_ada, b_ada):
    depth, d, n = w_ada.shape
    tn = 1152
    return pl.pallas_call(
        _ada_kernel,
        out_shape=jax.ShapeDtypeStruct((depth, 8, n), F32),
        grid=(depth, n // tn),
        in_specs=[
            pl.BlockSpec((8, d), lambda l, j: (0, 0)),
            pl.BlockSpec((1, d, tn), lambda l, j: (l, 0, j)),
            pl.BlockSpec((1, 1, tn), lambda l, j: (l, 0, j)),
        ],
        out_specs=pl.BlockSpec((1, 8, tn), lambda l, j: (l, 0, j)),
        compiler_params=_params(("arbitrary", "arbitrary")),
        name="adaln",
    )(cc, w_ada, b_ada.reshape(depth, 1, n))


def _mod_spec(n_lat_tiles):
    return pl.BlockSpec((1, N_MOD, D_MODEL), lambda i: (jnp.where(i >= n_lat_tiles, 1, 0), 0, 0))


def _ffn_kernel(x_ref, mod_ref, g_ref, w1_ref, w3_ref, w2_ref, o_ref, *, base):
    x = x_ref[...]
    shift = mod_ref[0, base:base + 1, :]
    scale = mod_ref[0, base + 1:base + 2, :]
    gate = mod_ref[0, base + 2:base + 3, :]
    hb = _norm_mod(x, g_ref[...], shift, scale).astype(BF16)
    a = _dot(hb, w1_ref[...])
    b = _dot(hb, w3_ref[...])
    u = (a * _sigmoid(a) * b).astype(BF16)
    o_ref[...] = x + (0.5 * gate) * _dot(u, w2_ref[...])


def _ffn_call(x, mods, g, w1, w3, w2, base, n_lat_tiles, n_tiles):
    tm = TOKEN_TILE
    return pl.pallas_call(
        functools.partial(_ffn_kernel, base=base),
        out_shape=jax.ShapeDtypeStruct((n_tiles * tm, D_MODEL), F32),
        grid=(n_tiles,),
        in_specs=[
            pl.BlockSpec((tm, D_MODEL), lambda i: (i, 0)),
            _mod_spec(n_lat_tiles),
            _const_spec((1, D_MODEL)),
            _const_spec((D_MODEL, D_FF)),
            _const_spec((D_MODEL, D_FF)),
            _const_spec((D_FF, D_MODEL)),
        ],
        out_specs=pl.BlockSpec((tm, D_MODEL), lambda i: (i, 0)),
        compiler_params=_params(("arbitrary",)),
        name="ffn",
    )(x, mods, g.reshape(1, D_MODEL), w1, w3, w2)


_C_F, _C_RQ, _C_RK, _C_RV, _C_RG = 0, 256, 512, 768, 1152
_C_DQ, _C_DK, _C_DV = 1536, 1920, 2304
_C_RQR, _C_RKR, _C_DQR, _C_DKR = 2688, 2944, 3200, 3584
PROJ_W = 3968


def _rot_perm(width, block):
    half = block // 2
    idx = np.arange(width)
    within = idx % block
    perm = np.where(within < half, idx + half, idx - half)
    sign = np.where(within < half, -1.0, 1.0).astype(np.float32)
    return perm, sign


def _build_proj_weight(w_in_l):
    cuts = np.cumsum([F_W, RET_QW, RET_QW, RET_VW, RET_VW, DIFF_QW, DIFF_QW, DIFF_VW])
    wf, wrq, wrk, wrv, wrg, wdq, wdk, wdv, wgt = jnp.split(w_in_l, cuts, axis=1)
    pr, sr = _rot_perm(RET_QW, RET_QK)
    pd, sd = _rot_perm(DIFF_QW, DIFF_QK // 2)
    z64 = jnp.zeros((D_MODEL, 64), w_in_l.dtype)
    parts = [wf, wrq, z64, wrk, z64, wrv, wrg, wdq, wdk, wdv,
             wrq[:, pr] * sr, z64, wrk[:, pr] * sr, z64, wdq[:, pd] * sd, wdk[:, pd] * sd]
    return jnp.concatenate(parts, axis=1).astype(BF16), wgt.astype(BF16)


def _rope_tables(seq, ctx_len):
    pos = jnp.arange(seq, dtype=F32)
    inv_r = ROPE_BASE ** (-jnp.arange(0, RET_QK, 2, dtype=F32) / RET_QK)
    ang_r = pos[:, None] * inv_r[None, :]
    cos_r = jnp.tile(jnp.cos(ang_r), (1, 2 * RET_HEADS))
    sin_r = jnp.tile(jnp.sin(ang_r), (1, 2 * RET_HEADS))
    rows = jnp.repeat(jnp.arange(seq // GRID_W, dtype=F32), GRID_W)
    cols = jnp.tile(jnp.arange(GRID_W, dtype=F32), seq // GRID_W)
    dim = DIFF_QK // 2
    inv_d = ROPE_BASE ** (-jnp.arange(0, dim, 2, dtype=F32) / dim)
    a_row = rows[:, None] * inv_d[None, :]
    a_col = cols[:, None] * inv_d[None, :]
    cos_hm = jnp.concatenate([jnp.cos(a_row)] * 2 + [jnp.cos(a_col)] * 2, axis=1)
    sin_hm = jnp.concatenate([jnp.sin(a_row)] * 2 + [jnp.sin(a_col)] * 2, axis=1)
    cos_d = jnp.tile(cos_hm, (1, 2 * DIFF_HEADS))
    sin_d = jnp.tile(sin_hm, (1, 2 * DIFF_HEADS))

    def with_ctx(t, fill):
        return jnp.concatenate([t, jnp.full((ctx_len, t.shape[1]), fill, F32)], axis=0)

    return with_ctx(cos_r, 1.0), with_ctx(sin_r, 0.0), with_ctx(cos_d, 1.0), with_ctx(sin_d, 0.0)


def _proj_kernel(x_ref, mod_ref, g_ref, w_ref, cr_ref, sr_ref, cd_ref, sd_ref,
                 uf_ref, rq_ref, rk_ref, rv_ref, rg_ref, dq_ref, dk_ref, dv_ref):
    x = x_ref[...]
    hb = _norm_mod(x, g_ref[...], mod_ref[0, 3:4, :], mod_ref[0, 4:5, :]).astype(BF16)
    p = _dot(hb, w_ref[...])
    cr, sr, cd, sd = cr_ref[...], sr_ref[...], cd_ref[...], sd_ref[...]
    uf_ref[...] = p[:, _C_F:_C_F + F_W]
    rq = p[:, _C_RQ:_C_RQ + RET_QW] * cr + p[:, _C_RQR:_C_RQR + RET_QW] * sr
    rk = p[:, _C_RK:_C_RK + RET_QW] * cr + p[:, _C_RKR:_C_RKR + RET_QW] * sr
    rq_ref[...] = rq.astype(BF16)
    rk_ref[...] = (rk * (RET_QK ** -0.5)).astype(BF16)
    rv_ref[...] = p[:, _C_RV:_C_RV + RET_VW].astype(BF16)
    rg_ref[...] = p[:, _C_RG:_C_RG + RET_VW]
    dq = p[:, _C_DQ:_C_DQ + DIFF_QW] * cd + p[:, _C_DQR:_C_DQR + DIFF_QW] * sd
    dk = p[:, _C_DK:_C_DK + DIFF_QW] * cd + p[:, _C_DKR:_C_DKR + DIFF_QW] * sd
    dq_ref[...] = (dq * ((DIFF_QK ** -0.5) * LOG2E)).astype(BF16)
    dk_ref[...] = dk.astype(BF16)
    dv_ref[...] = p[:, _C_DV:_C_DV + DIFF_VW].astype(BF16)


def _proj_call(x, mods, g, w_aug, tables, n_lat_tiles, n_tiles):
    tm = TOKEN_TILE
    t = n_tiles * tm

    def row_spec(w):
        return pl.BlockSpec((tm, w), lambda i: (i, 0))

    out_shape = (
        jax.ShapeDtypeStruct((t, F_W), F32),
        jax.ShapeDtypeStruct((t, RET_QW), BF16),
        jax.ShapeDtypeStruct((t, RET_QW), BF16),
        jax.ShapeDtypeStruct((t, RET_VW), BF16),
        jax.ShapeDtypeStruct((t, RET_VW), F32),
        jax.ShapeDtypeStruct((t, DIFF_QW), BF16),
        jax.ShapeDtypeStruct((t, DIFF_QW), BF16),
        jax.ShapeDtypeStruct((t, DIFF_VW), BF16),
    )
    return pl.pallas_call(
        _proj_kernel,
        out_shape=out_shape,
        grid=(n_tiles,),
        in_specs=[
            row_spec(D_MODEL), _mod_spec(n_lat_tiles), _const_spec((1, D_MODEL)),
            _const_spec((D_MODEL, PROJ_W)),
            row_spec(RET_QW), row_spec(RET_QW), row_spec(DIFF_QW), row_spec(DIFF_QW),
        ],
        out_specs=tuple(row_spec(s.shape[1]) for s in out_shape),
        compiler_params=_params(("arbitrary",)),
        name="mixer_proj",
    )(x, mods, g.reshape(1, D_MODEL), w_aug, *tables)


def _dft_mats(n):
    k = np.arange(n)
    ang = 2.0 * np.pi * ((k[:, None] * k[None, :]) % n) / n
    return np.cos(ang).astype(np.float32), np.sin(ang).astype(np.float32)


def _channel_dft_mats():
    c, s = _dft_mats(FOURIER_CH)
    eye = np.eye(FOURIER_GROUPS, dtype=np.float32)
    return np.kron(eye, c), np.kron(eye, s)


def _fourier_stage1_kernel(x_ref, cc_ref, sc_ref, c1_ref, s1_ref, twc_ref, tws_ref, tr_ref, ti_ref, *, nb):
    cc, sc = cc_ref[...], sc_ref[...]
    zr, zi = [], []
    for j in range(nb):
        u = x_ref[:, j * F_W:(j + 1) * F_W]
        zr.append(_dot3(u, cc))
        zi.append(-_dot3(u, sc))
    zr = jnp.concatenate(zr, axis=1) if nb > 1 else zr[0]
    zi = jnp.concatenate(zi, axis=1) if nb > 1 else zi[0]
    c1, s1 = c1_ref[...], s1_ref[...]
    tr = _dot3(c1, zr) + _dot3(s1, zi)
    ti = _dot3(c1, zi) - _dot3(s1, zr)
    twc, tws = twc_ref[...], tws_ref[...]
    tr_ref[...] = tr * twc + ti * tws
    ti_ref[...] = ti * twc - tr * tws


def _fourier_stage2_kernel(tr_ref, ti_ref, c2_ref, s2_ref, o_ref, *, kb):
    c2, s2 = c2_ref[...], s2_ref[...]
    for j in range(kb):
        o_ref[j] = _dot3(c2, tr_ref[j]) + _dot3(s2, ti_ref[j])


def _fourier_latent(u, twc, tws):
    seq = u.shape[0]
    n1, n2 = DFT_N1, seq // DFT_N1
    nb = min(8, n2)
    kb = 8
    cc, sc = _channel_dft_mats()
    c1, s1 = _dft_mats(n1)
    c2, s2 = _dft_mats(n2)
    x2 = u.reshape(n1, n2 * F_W)
    blk = pl.BlockSpec((n1, nb * F_W), lambda i: (0, i))
    tr, ti = pl.pallas_call(
        functools.partial(_fourier_stage1_kernel, nb=nb),
        out_shape=(jax.ShapeDtypeStruct((n1, n2 * F_W), F32),) * 2,
        grid=(n2 // nb,),
        in_specs=[blk, _const_spec((F_W, F_W)), _const_spec((F_W, F_W)),
                  _const_spec((n1, n1)), _const_spec((n1, n1)), blk, blk],
        out_specs=(blk, blk),
        compiler_params=_params(("arbitrary",)),
        name="fourier_stage1",
    )(x2, cc, sc, c1, s1, twc, tws)
    tr3 = tr.reshape(n1, n2, F_W)
    ti3 = ti.reshape(n1, n2, F_W)
    blk3 = pl.BlockSpec((kb, n2, F_W), lambda i: (i, 0, 0))
    o3 = pl.pallas_call(
        functools.partial(_fourier_stage2_kernel, kb=kb),
        out_shape=jax.ShapeDtypeStruct((n1, n2, F_W), F32),
        grid=(n1 // kb,),
        in_specs=[blk3, blk3, _const_spec((n2, n2)), _const_spec((n2, n2))],
        out_specs=blk3,
        compiler_params=_params(("arbitrary",)),
        name="fourier_stage2",
    )(tr3, ti3, c2, s2)
    return jnp.transpose(o3, (1, 0, 2)).reshape(seq, F_W)


def _twiddles(seq):
    n1, n2 = DFT_N1, seq // DFT_N1
    k1 = jnp.arange(n1, dtype=jnp.int32)[:, None]
    m2 = jnp.arange(n2, dtype=jnp.int32)[None, :]
    ang = (2.0 * math.pi / seq) * ((k1 * m2) % seq).astype(F32)
    scale = 1.0 / math.sqrt(seq * FOURIER_CH)
    twc = jnp.repeat(jnp.cos(ang) * scale, F_W, axis=1)
    tws = jnp.repeat(jnp.sin(ang) * scale, F_W, axis=1)
    return twc, tws


def _fourier_ctx_kernel(u_ref, cc_ref, sc_ref, cl_ref, sl_ref, o_ref, *, scale):
    u = u_ref[...]
    a = _dot3(u, cc_ref[...])
    b = _dot3(u, sc_ref[...])
    o_ref[...] = (_dot3(cl_ref[...], a) - _dot3(sl_ref[...], b)) * scale


def _fourier_ctx(u):
    n = u.shape[0]
    cc, sc = _channel_dft_mats()
    cl, sl = _dft_mats(n)
    return pl.pallas_call(
        functools.partial(_fourier_ctx_kernel, scale=1.0 / math.sqrt(n * FOURIER_CH)),
        out_shape=jax.ShapeDtypeStruct((n, F_W), F32),
        name="fourier_ctx",
    )(u, cc, sc, cl, sl)


def _ret_dir(q, kt, v, s_ref, d_ref, qdec, kdec, cd, bdmask, hmask_ref, vmask_ref):
    pieces = []
    vparts = []
    for h in range(RET_HEADS):
        qh = q * hmask_ref[h]
        pieces.append((_dot(qh, kt) * d_ref[h]).astype(BF16))
        vparts.append(v * vmask_ref[h])
    inner = jnp.concatenate(pieces, axis=1)
    vbd = jnp.concatenate(vparts, axis=0)
    s = s_ref[...]
    o = _dot(inner, vbd) + _dot(q, s.astype(BF16)) * qdec
    kd = (kt.astype(F32) * kdec).astype(BF16)
    s_ref[...] = s * cd + bdmask * _dot(kd, v)
    return o


def _ret_kernel(logg_ref, lgv_ref, lgk_ref, bdmask_ref, hmask_ref, vmask_ref,
                qf_ref, ktf_ref, vf_ref, qb_ref, ktb_ref, vb_ref,
                of_ref, ob_ref,
                sf_ref, sb_ref, df_ref, db_ref, qdf_ref, qdb_ref, kdf_ref, kdb_ref, cdf_ref, cdb_ref):
    c = RET_CHUNK

    @pl.when(pl.program_id(0) == 0)
    def _():
        sf_ref[...] = jnp.zeros_like(sf_ref)
        sb_ref[...] = jnp.zeros_like(sb_ref)
        ii = lax.broadcasted_iota(jnp.int32, (c, c), 0).astype(F32)
        jj = lax.broadcasted_iota(jnp.int32, (c, c), 1).astype(F32)
        for h in range(RET_HEADS):
            df_ref[h] = jnp.where(ii >= jj, jnp.exp(logg_ref[0, h] * jnp.maximum(ii - jj, 0.0)), 0.0)
            db_ref[h] = jnp.where(jj >= ii, jnp.exp(logg_ref[1, h] * jnp.maximum(jj - ii, 0.0)), 0.0)
        ri = lax.broadcasted_iota(jnp.int32, (c, RET_VW), 0).astype(F32)
        qdf_ref[...] = jnp.exp(lgv_ref[0] * (ri + 1.0))
        qdb_ref[...] = jnp.exp(lgv_ref[1] * (c - ri))
        cj = lax.broadcasted_iota(jnp.int32, (RET_QW, c), 1).astype(F32)
        kdf_ref[...] = jnp.exp(lgk_ref[0] * (c - 1.0 - cj))
        kdb_ref[...] = jnp.exp(lgk_ref[1] * cj)
        cdf_ref[...] = jnp.exp(lgv_ref[0] * float(c))
        cdb_ref[...] = jnp.exp(lgv_ref[1] * float(c))

    bdmask = bdmask_ref[...]
    of_ref[...] = _ret_dir(qf_ref[...], ktf_ref[...], vf_ref[...], sf_ref, df_ref,
                           qdf_ref[...], kdf_ref[...], cdf_ref[...], bdmask, hmask_ref, vmask_ref)
    ob_ref[...] = _ret_dir(qb_ref[...], ktb_ref[...], vb_ref[...], sb_ref, db_ref,
                           qdb_ref[...], kdb_ref[...], cdb_ref[...], bdmask, hmask_ref, vmask_ref)


def _ret_masks():
    hm = np.zeros((RET_HEADS, 1, RET_QW), np.float32)
    vm = np.zeros((RET_HEADS, 1, RET_VW), np.float32)
    bd = np.zeros((RET_QW, RET_VW), np.float32)
    for h in range(RET_HEADS):
        hm[h, 0, h * RET_QK:(h + 1) * RET_QK] = 1.0
        vm[h, 0, h * RET_V:(h + 1) * RET_V] = 1.0
        bd[h * RET_QK:(h + 1) * RET_QK, h * RET_V:(h + 1) * RET_V] = 1.0
    return jnp.asarray(bd), jnp.asarray(hm, BF16), jnp.asarray(vm, BF16)


def _retention_call(rq, rkt, rv, log_g2, n_lat, n_ctx):
    c = RET_CHUNK
    n = n_lat + n_ctx
    t = n * c

    def fwd(i):
        return jnp.where(i < n_ctx, n_lat + i, i - n_ctx)

    def bwd(i):
        return jnp.where(i < n_ctx, n - 1 - i, n_lat + n_ctx - 1 - i)

    lgv = jnp.repeat(log_g2, RET_V, axis=1).reshape(2, 1, RET_VW)
    lgk = jnp.broadcast_to(jnp.repeat(log_g2, RET_QK, axis=1)[:, :, None], (2, RET_QW, c))
    bd, hm, vm = _ret_masks()

    def specs(ix):
        return [pl.BlockSpec((c, RET_QW), lambda i: (ix(i), 0)),
                pl.BlockSpec((RET_QW, c), lambda i: (0, ix(i))),
                pl.BlockSpec((c, RET_VW), lambda i: (ix(i), 0))]

    vm_f32 = pltpu.VMEM
    return pl.pallas_call(
        _ret_kernel,
        out_shape=(jax.ShapeDtypeStruct((t, RET_VW), F32),) * 2,
        grid=(n,),
        in_specs=[pl.BlockSpec(memory_space=pltpu.SMEM),
                  _const_spec((2, 1, RET_VW)), _const_spec((2, RET_QW, c)),
                  _const_spec((RET_QW, RET_VW)), _const_spec((RET_HEADS, 1, RET_QW)),
                  _const_spec((RET_HEADS, 1, RET_VW))] + specs(fwd) + specs(bwd),
        out_specs=(pl.BlockSpec((c, RET_VW), lambda i: (fwd(i), 0)),
                   pl.BlockSpec((c, RET_VW), lambda i: (bwd(i), 0))),
        scratch_shapes=[vm_f32((RET_QW, RET_VW), F32), vm_f32((RET_QW, RET_VW), F32),
                        vm_f32((RET_HEADS, c, c), F32), vm_f32((RET_HEADS, c, c), F32),
                        vm_f32((c, RET_VW), F32), vm_f32((c, RET_VW), F32),
                        vm_f32((RET_QW, c), F32), vm_f32((RET_QW, c), F32),
                        vm_f32((1, RET_VW), F32), vm_f32((1, RET_VW), F32)],
        compiler_params=_params(("arbitrary",)),
        name="retention",
    )(log_g2, lgv, lgk, bd, hm, vm, rq, rkt, rv, rq, rkt, rv)


def _attn_kernel(lam_ref, q_ref, kt_ref, ktn_ref, v_ref, subln_ref, o_ref, qm_ref, m_ref, acc_ref, s0_ref,
                 *, out_scale):
    ki = pl.program_id(2)

    @pl.when(ki == 0)
    def _():
        q = q_ref[...]
        lane = lax.broadcasted_iota(jnp.int32, q.shape, 1)
        for j in range(4):
            lo = j * DIFF_QK
            qm_ref[j] = jnp.where((lane >= lo) & (lane < lo + DIFF_QK), q, jnp.zeros_like(q))
        m_ref[...] = jnp.full(m_ref.shape, -jnp.inf, F32)
        acc_ref[...] = jnp.zeros_like(acc_ref)
        s0_ref[...] = _dot(qm_ref[0], kt_ref[...])

    kt = kt_ref[...]
    s_next = s0_ref[...]
    for j in range(4):
        s = s_next
        s_next = _dot(qm_ref[j + 1], kt) if j + 1 < 4 else _dot(qm_ref[0], ktn_ref[...])
        m_prev = m_ref[j]
        m_new = jnp.maximum(m_prev, jnp.max(s, axis=1, keepdims=True))
        p = jnp.exp2(s - m_new)
        alpha = jnp.exp2(m_prev - m_new)
        acc_ref[j] = alpha * acc_ref[j] + _dot(p.astype(BF16), v_ref[j // 2])
        m_ref[j] = m_new
    s0_ref[...] = s_next

    @pl.when(ki == pl.num_programs(2) - 1)
    def _():
        lam = lam_ref[0]
        outs = []
        for hl in range(2):
            a1 = acc_ref[2 * hl]
            a2 = acc_ref[2 * hl + 1]
            o = (a1[:, :DIFF_V] / a1[:, DIFF_V:DIFF_V + 1]
                 - lam * (a2[:, :DIFF_V] / a2[:, DIFF_V:DIFF_V + 1]))
            ms = jnp.mean(o * o, axis=-1, keepdims=True)
            outs.append(o * lax.rsqrt(ms + EPS) * subln_ref[...] * out_scale)
        o_ref[...] = jnp.concatenate(outs, axis=1)


def _attn_call(dq, kt, v_aug, lam, subln, lam_init, n_q, tq, tk):
    t_k = kt.shape[1]
    nk = t_k // tk
    return pl.pallas_call(
        functools.partial(_attn_kernel, out_scale=1.0 - lam_init),
        out_shape=jax.ShapeDtypeStruct((n_q, DIFF_VW), F32),
        grid=(DIFF_HEADS // 2, n_q // tq, nk),
        in_specs=[
            pl.BlockSpec(memory_space=pltpu.SMEM),
            pl.BlockSpec((tq, LANES), lambda h, i, k: (i, h)),
            pl.BlockSpec((LANES, tk), lambda h, i, k: (h, k)),
            pl.BlockSpec((LANES, tk), lambda h, i, k: (h, jnp.minimum(k + 1, nk - 1))),
            pl.BlockSpec((2, tk, LANES), lambda h, i, k: (h, k, 0)),
            pl.BlockSpec((1, DIFF_V), lambda h, i, k: (0, 0)),
        ],
        out_specs=pl.BlockSpec((tq, LANES), lambda h, i, k: (i, h)),
        scratch_shapes=[pltpu.VMEM((4, tq, LANES), BF16),
                        pltpu.VMEM((4, tq, 1), F32),
                        pltpu.VMEM((4, tq, LANES), F32),
                        pltpu.VMEM((tq, tk), F32)],
        compiler_params=_params(("arbitrary", "arbitrary", "arbitrary")),
        name="diff_attn",
    )(lam, dq, kt, kt, v_aug, subln.reshape(1, DIFF_V))


def _merge_kernel(x_ref, mod_ref, g_ref, f_ref, of_ref, ob_ref, rg_ref, d_ref,
                  wgt_ref, wbf_ref, wbr_ref, wbd_ref, wo_ref, bd_ref, o_ref):
    x = x_ref[...]
    hb = _norm_mod(x, g_ref[...], mod_ref[0, 3:4, :], mod_ref[0, 4:5, :]).astype(BF16)
    gates = _sigmoid(_dot(hb, wgt_ref[...]))
    r = of_ref[...] + ob_ref[...]
    rr_hi, rr_lo = _split_bf16(r * r)
    ms = _dot(rr_hi, bd_ref[...]) + _dot(rr_lo, bd_ref[...])
    rg = rg_ref[...]
    yr = r * lax.rsqrt(ms + EPS) * (rg * _sigmoid(rg))
    mixed = (gates[:, :D_MODEL] * _dot(f_ref[...].astype(BF16), wbf_ref[...])
             + gates[:, D_MODEL:2 * D_MODEL] * _dot(yr.astype(BF16), wbr_ref[...])
             + gates[:, 2 * D_MODEL:] * _dot(d_ref[...].astype(BF16), wbd_ref[...]))
    y = _dot(mixed.astype(BF16), wo_ref[...])
    o_ref[...] = x + mod_ref[0, 5:6, :] * y


def _merge_call(x, mods, g, f_out, o_f, o_b, rg, d_out, wgt, wbf, wbr, wbd, wo, n_lat_tiles, n_tiles):
    tm = TOKEN_TILE
    bd = np.kron(np.eye(RET_HEADS, dtype=np.float32), np.full((RET_V, RET_V), 1.0 / RET_V, np.float32))

    def row_spec(w):
        return pl.BlockSpec((tm, w), lambda i: (i, 0))

    return pl.pallas_call(
        _merge_kernel,
        out_shape=jax.ShapeDtypeStruct((n_tiles * tm, D_MODEL), F32),
        grid=(n_tiles,),
        in_specs=[
            row_spec(D_MODEL), _mod_spec(n_lat_tiles), _const_spec((1, D_MODEL)),
            row_spec(F_W), row_spec(RET_VW), row_spec(RET_VW), row_spec(RET_VW), row_spec(DIFF_VW),
            _const_spec((D_MODEL, GATE_W)), _const_spec((F_W, D_MODEL)), _const_spec((RET_VW, D_MODEL)),
            _const_spec((DIFF_VW, D_MODEL)), _const_spec((D_MODEL, D_MODEL)), _const_spec((RET_VW, RET_VW)),
        ],
        out_specs=row_spec(D_MODEL),
        compiler_params=_params(("arbitrary",)),
        name="merge",
    )(x, mods, g.reshape(1, D_MODEL), f_out, o_f, o_b, rg, d_out, wgt, wbf, wbr, wbd, wo,
      jnp.asarray(bd, BF16))


def _final_kernel(x_ref, g_ref, o_ref):
    x = x_ref[...]
    ms = jnp.mean(x * x, axis=-1, keepdims=True)
    o_ref[...] = x * lax.rsqrt(ms + EPS) * g_ref[...]


def _final_call(x, g, n_tiles):
    tm = TOKEN_TILE
    return pl.pallas_call(
        _final_kernel,
        out_shape=jax.ShapeDtypeStruct((n_tiles * tm, D_MODEL), F32),
        grid=(n_tiles,),
        in_specs=[pl.BlockSpec((tm, D_MODEL), lambda i: (i, 0)), _const_spec((1, D_MODEL))],
        out_specs=pl.BlockSpec((tm, D_MODEL), lambda i: (i, 0)),
        compiler_params=_params(("arbitrary",)),
        name="final_norm",
    )(x, g.reshape(1, D_MODEL))


def _pick_tile(n, candidates):
    for c in candidates:
        if n % c == 0:
            return c
    raise ValueError(f"no tile for {n}")


def kernel(x, c, ctx, c_ctx, w_ada, b_ada, norm_g, ffn_w1, ffn_w3, ffn_w2, w_in, ret_decay_logit,
           diff_lambda, diff_subln, w_branch_f, w_branch_r, w_branch_d, w_out, final_g):
    batch, seq, d = x.shape
    ctx_len = ctx.shape[1]
    assert batch == 1 and d == D_MODEL
    assert seq % (DFT_N1 * 8) == 0 and seq % TOKEN_TILE == 0 and ctx_len % TOKEN_TILE == 0
    total = seq + ctx_len
    n_lat_tiles = seq // TOKEN_TILE
    n_tiles = total // TOKEN_TILE
    n_lat_chunks = seq // RET_CHUNK
    n_ctx_chunks = ctx_len // RET_CHUNK

    cc = jnp.zeros((8, D_MODEL), F32).at[0].set(c[0]).at[1].set(c_ctx)
    mods_all = _ada_call(cc, w_ada, b_ada)[:, :2].reshape(DEPTH, 2, N_MOD, D_MODEL)

    tables = _rope_tables(seq, ctx_len)
    twc, tws = _twiddles(seq)
    log_g2_all = jax.nn.log_sigmoid(ret_decay_logit.astype(F32))
    lv = diff_lambda.astype(F32)
    ones_cols = jnp.ones((DIFF_HEADS, total, LANES - DIFF_V), BF16)

    tq = _pick_tile(seq, (1024, 512, 256, 128))
    tk = _pick_tile(total, (1280, 640, 256, 128))

    xs = jnp.concatenate([x[0], ctx[0]], axis=0)
    for l in range(DEPTH):
        last = l == DEPTH - 1
        lam_init = 0.8 - 0.6 * math.exp(-0.3 * l)
        mods = mods_all[l]
        lam = (jnp.exp(jnp.sum(lv[l, 0] * lv[l, 1])) - jnp.exp(jnp.sum(lv[l, 2] * lv[l, 3]))
               + lam_init).reshape(1)
        w_aug, wgt = _build_proj_weight(w_in[l])

        xs = _ffn_call(xs, mods, norm_g[l, 0], ffn_w1[l, 0].astype(BF16), ffn_w3[l, 0].astype(BF16),
                       ffn_w2[l, 0].astype(BF16), 0, n_lat_tiles, n_tiles)

        uf, rq, rk, rv, rg, dq, dk, dv = _proj_call(xs, mods, norm_g[l, 1], w_aug, tables,
                                                     n_lat_tiles, n_tiles)

        f_lat = _fourier_latent(uf[:seq], twc, tws)
        f_ctx = jnp.zeros((ctx_len, F_W), F32) if last else _fourier_ctx(uf[seq:])
        f_out = jnp.concatenate([f_lat, f_ctx], axis=0)

        o_f, o_b = _retention_call(rq, rk.T, rv, log_g2_all[l], n_lat_chunks, n_ctx_chunks)

        kt = dk.T
        v_aug = jnp.concatenate(
            [dv.reshape(total, DIFF_HEADS, DIFF_V).transpose(1, 0, 2), ones_cols], axis=2)
        d_lat = _attn_call(dq, kt, v_aug, lam, diff_subln[l], lam_init, seq, tq, tk)
        if last:
            d_ctx = jnp.zeros((ctx_len, DIFF_VW), F32)
        else:
            d_ctx = _attn_call(dq[seq:], kt[:, seq:], v_aug[:, seq:], lam, diff_subln[l], lam_init,
                               ctx_len, ctx_len, ctx_len)
        d_out = jnp.concatenate([d_lat, d_ctx], axis=0)

        xs = _merge_call(xs, mods, norm_g[l, 1], f_out, o_f, o_b, rg, d_out, wgt,
                         w_branch_f[l].astype(BF16), w_branch_r[l].astype(BF16),
                         w_branch_d[l].astype(BF16), w_out[l].astype(BF16), n_lat_tiles, n_tiles)

        xs = _ffn_call(xs, mods, norm_g[l, 2], ffn_w1[l, 1].astype(BF16), ffn_w3[l, 1].astype(BF16),
                       ffn_w2[l, 1].astype(BF16), 6, n_lat_tiles, n_tiles)

    out = _final_call(xs, final_g, n_lat_tiles)
    return out.reshape(1, seq, D_MODEL)
```

```python
import functools
import math

import numpy as np
import jax
import jax.numpy as jnp
from jax import lax
from jax.experimental import pallas as pl
from jax.experimental.pallas import tpu as pltpu

D_MODEL = 1024
DEPTH = 4
GRID_W = 64
D_FF = 2816
N_MOD = 9
FOURIER_GROUPS = 4
FOURIER_CH = 64
RET_HEADS = 6
RET_QK = 32
RET_V = 64
RET_CHUNK = 128
DIFF_HEADS = 6
DIFF_QK = 32
DIFF_V = 64
ROPE_BASE = 10000.0
EPS = 1e-6
F_W = FOURIER_GROUPS * FOURIER_CH
RET_QW = RET_HEADS * RET_QK
RET_VW = RET_HEADS * RET_V
DIFF_QW = DIFF_HEADS * 2 * DIFF_QK
DIFF_VW = DIFF_HEADS * DIFF_V
GATE_W = 3 * D_MODEL

LANES = 128
VMEM_LIMIT_BYTES = 56 * 1024 * 1024

RET_QP = 2 * LANES
TOKEN_TILE = 512
DFT_N1 = 128
ATTN_UNIT_ROWS = 1024

BF16 = jnp.bfloat16
F32 = jnp.float32
LOG2E = math.log2(math.e)


def _dot(a, b):
    return jnp.dot(a, b, preferred_element_type=F32)


def _split_bf16(x):
    hi = x.astype(BF16)
    lo = (x - hi.astype(F32)).astype(BF16)
    return hi, lo


def _dot3(a, b):
    ah, al = _split_bf16(a)
    bh, bl = _split_bf16(b)
    return _dot(ah, bh) + _dot(al, bh) + _dot(ah, bl)


def _norm_mod(x, g, shift, scale):
    ms = jnp.mean(x * x, axis=-1, keepdims=True)
    y = x * lax.rsqrt(ms + EPS) * g
    return y * (1.0 + scale) + shift


def _sigmoid(x):
    return 1.0 / (1.0 + jnp.exp(-x))


def _const_spec(shape):
    nd = len(shape)
    return pl.BlockSpec(shape, lambda *_: (0,) * nd, pipeline_mode=pl.Buffered(1))


def _params(sem):
    return pltpu.CompilerParams(dimension_semantics=sem, vmem_limit_bytes=VMEM_LIMIT_BYTES)


def _row_spec(width):
    return pl.BlockSpec((TOKEN_TILE, width), lambda i: (i, 0))


def _ada_kernel(cc_ref, w_ref, b_ref, o_ref):
    cc = cc_ref[...]
    s = cc * _sigmoid(cc)
    o_ref[0] = _dot3(s, w_ref[0]) + b_ref[0]


def _ada_call(cc, w_ada, b_ada):
    depth, d, n = w_ada.shape
    tn = 1152
    return pl.pallas_call(
        _ada_kernel,
        out_shape=jax.ShapeDtypeStruct((depth, 8, n), F32),
        grid=(depth, n // tn),
        in_specs=[
            pl.BlockSpec((8, d), lambda l, j: (0, 0)),
            pl.BlockSpec((1, d, tn), lambda l, j: (l, 0, j)),
            pl.BlockSpec((1, 1, tn), lambda l, j: (l, 0, j)),
        ],
        out_specs=pl.BlockSpec((1, 8, tn), lambda l, j: (l, 0, j)),
        compiler_params=_params(("arbitrary", "arbitrary")),
        name="adaln",
    )(cc, w_ada, b_ada.reshape(depth, 1, n))


def _mod_spec(n_lat_tiles):
    return pl.BlockSpec((1, N_MOD, D_MODEL), lambda i: (jnp.where(i >= n_lat_tiles, 1, 0), 0, 0))


def _ffn_tile(x, mod_ref, g_ref, w1_ref, w3_ref, w2_ref, base):
    shift = mod_ref[0, base:base + 1, :]
    scale = mod_ref[0, base + 1:base + 2, :]
    gate = mod_ref[0, base + 2:base + 3, :]
    hb = _norm_mod(x, g_ref[...], shift, scale).astype(BF16)
    a = _dot(hb, w1_ref[...])
    b = _dot(hb, w3_ref[...])
    u = (a * _sigmoid(a) * b).astype(BF16)
    return x + (0.5 * gate) * _dot(u, w2_ref[...])


def _ffn_kernel(x_ref, mod_ref, g_ref, w1_ref, w3_ref, w2_ref, o_ref, *, base):
    o_ref[...] = _ffn_tile(x_ref[...], mod_ref, g_ref, w1_ref, w3_ref, w2_ref, base)


def _ffn_final_kernel(x_ref, mod_ref, g_ref, w1_ref, w3_ref, w2_ref, fg_ref, o_ref, *, base):
    y = _ffn_tile(x_ref[...], mod_ref, g_ref, w1_ref, w3_ref, w2_ref, base)
    ms = jnp.mean(y * y, axis=-1, keepdims=True)
    o_ref[...] = y * lax.rsqrt(ms + EPS) * fg_ref[...]


def _ffn_call(x, mods, g, w1, w3, w2, base, n_lat_tiles, n_tiles, final_g=None):
    in_specs = [
        _row_spec(D_MODEL), _mod_spec(n_lat_tiles), _const_spec((1, D_MODEL)),
        _const_spec((D_MODEL, D_FF)), _const_spec((D_MODEL, D_FF)), _const_spec((D_FF, D_MODEL)),
    ]
    args = [x, mods, g.reshape(1, D_MODEL), w1, w3, w2]
    body = _ffn_kernel
    if final_g is not None:
        in_specs.append(_const_spec((1, D_MODEL)))
        args.append(final_g.reshape(1, D_MODEL))
        body = _ffn_final_kernel
    return pl.pallas_call(
        functools.partial(body, base=base),
        out_shape=jax.ShapeDtypeStruct((n_tiles * TOKEN_TILE, D_MODEL), F32),
        grid=(n_tiles,),
        in_specs=in_specs,
        out_specs=_row_spec(D_MODEL),
        compiler_params=_params(("arbitrary",)),
        name="ffn",
    )(*args)


_C_F, _C_RQ, _C_RK, _C_RV, _C_RG = 0, 256, 512, 768, 1152
_C_DQ, _C_DK, _C_DV = 1536, 1920, 2304
_C_RQR, _C_RKR, _C_DQR, _C_DKR = 2688, 2944, 3200, 3584
PROJ_W = 3968


def _rotate_half_cols(w, block):
    depth, d, width = w.shape
    w5 = w.reshape(depth, d, width // block, 2, block // 2)
    return jnp.concatenate([-w5[:, :, :, 1:2], w5[:, :, :, 0:1]], axis=3).reshape(depth, d, width)


def _prep_proj_weights(w_in):
    cuts = np.cumsum([F_W, RET_QW, RET_QW, RET_VW, RET_VW, DIFF_QW, DIFF_QW, DIFF_VW])
    wf, wrq, wrk, wrv, wrg, wdq, wdk, wdv, wgt = jnp.split(w_in, cuts, axis=2)
    z = jnp.zeros(w_in.shape[:2] + (RET_QP - RET_QW,), w_in.dtype)
    parts = [wf, wrq, z, wrk, z, wrv, wrg, wdq, wdk, wdv,
             _rotate_half_cols(wrq, RET_QK), z, _rotate_half_cols(wrk, RET_QK), z,
             _rotate_half_cols(wdq, DIFF_QK // 2), _rotate_half_cols(wdk, DIFF_QK // 2)]
    return jnp.concatenate(parts, axis=2).astype(BF16), wgt.astype(BF16)


def _rope_tables(seq, n_rows):
    pos = jnp.arange(seq, dtype=F32)
    inv_r = ROPE_BASE ** (-jnp.arange(0, RET_QK, 2, dtype=F32) / RET_QK)
    ang_r = pos[:, None] * inv_r[None, :]
    cos_r = jnp.tile(jnp.cos(ang_r), (1, 2 * RET_HEADS))
    sin_r = jnp.tile(jnp.sin(ang_r), (1, 2 * RET_HEADS))
    rows = jnp.repeat(jnp.arange(seq // GRID_W, dtype=F32), GRID_W)
    cols = jnp.tile(jnp.arange(GRID_W, dtype=F32), seq // GRID_W)
    dim = DIFF_QK // 2
    inv_d = ROPE_BASE ** (-jnp.arange(0, dim, 2, dtype=F32) / dim)
    a_row = rows[:, None] * inv_d[None, :]
    a_col = cols[:, None] * inv_d[None, :]
    cos_hm = jnp.concatenate([jnp.cos(a_row)] * 2 + [jnp.cos(a_col)] * 2, axis=1)
    sin_hm = jnp.concatenate([jnp.sin(a_row)] * 2 + [jnp.sin(a_col)] * 2, axis=1)
    cos_d = jnp.tile(cos_hm, (1, 2 * DIFF_HEADS))
    sin_d = jnp.tile(sin_hm, (1, 2 * DIFF_HEADS))

    def finish(t, width, fill):
        t = jnp.pad(t, ((0, 0), (0, width - t.shape[1])), constant_values=fill)
        return jnp.pad(t, ((0, n_rows - seq), (0, 0)), constant_values=fill)

    return (finish(cos_r, RET_QP, 1.0), finish(sin_r, RET_QP, 0.0),
            finish(cos_d, DIFF_QW, 1.0), finish(sin_d, DIFF_QW, 0.0))


def _proj_kernel(x_ref, mod_ref, g_ref, w_ref, cr_ref, sr_ref, cd_ref, sd_ref,
                 uf_ref, rq_ref, rkt_ref, rv_ref, rg_ref, dq_ref, dkt_ref, va_ref):
    x = x_ref[...]
    hb = _norm_mod(x, g_ref[...], mod_ref[0, 3:4, :], mod_ref[0, 4:5, :]).astype(BF16)
    p = _dot(hb, w_ref[...])
    cr, sr, cd, sd = cr_ref[...], sr_ref[...], cd_ref[...], sd_ref[...]
    uf_ref[...] = p[:, _C_F:_C_F + F_W]
    rq = p[:, _C_RQ:_C_RQ + RET_QP] * cr + p[:, _C_RQR:_C_RQR + RET_QP] * sr
    rk = p[:, _C_RK:_C_RK + RET_QP] * cr + p[:, _C_RKR:_C_RKR + RET_QP] * sr
    rq_ref[...] = rq.astype(BF16)
    rkt_ref[...] = (rk * (RET_QK ** -0.5)).T.astype(BF16)
    rv_ref[...] = p[:, _C_RV:_C_RV + RET_VW].astype(BF16)
    rg_ref[...] = p[:, _C_RG:_C_RG + RET_VW]
    dq = p[:, _C_DQ:_C_DQ + DIFF_QW] * cd + p[:, _C_DQR:_C_DQR + DIFF_QW] * sd
    dk = p[:, _C_DK:_C_DK + DIFF_QW] * cd + p[:, _C_DKR:_C_DKR + DIFF_QW] * sd
    dq_ref[...] = (dq * ((DIFF_QK ** -0.5) * LOG2E)).astype(BF16)
    dkt_ref[...] = dk.T.astype(BF16)
    ones = jnp.ones((x.shape[0], LANES - DIFF_V), F32)
    for h in range(DIFF_HEADS):
        c0 = _C_DV + h * DIFF_V
        va_ref[h] = jnp.concatenate([p[:, c0:c0 + DIFF_V], ones], axis=1).astype(BF16)


def _proj_call(x, mods, g, w_aug, tables, n_lat_tiles, n_tiles):
    tm = TOKEN_TILE
    t = n_tiles * tm
    out_shape = (
        jax.ShapeDtypeStruct((t, F_W), F32),
        jax.ShapeDtypeStruct((t, RET_QP), BF16),
        jax.ShapeDtypeStruct((RET_QP, t), BF16),
        jax.ShapeDtypeStruct((t, RET_VW), BF16),
        jax.ShapeDtypeStruct((t, RET_VW), F32),
        jax.ShapeDtypeStruct((t, DIFF_QW), BF16),
        jax.ShapeDtypeStruct((DIFF_QW, t), BF16),
        jax.ShapeDtypeStruct((DIFF_HEADS, t, LANES), BF16),
    )
    out_specs = (
        _row_spec(F_W), _row_spec(RET_QP),
        pl.BlockSpec((RET_QP, tm), lambda i: (0, i)),
        _row_spec(RET_VW), _row_spec(RET_VW), _row_spec(DIFF_QW),
        pl.BlockSpec((DIFF_QW, tm), lambda i: (0, i)),
        pl.BlockSpec((DIFF_HEADS, tm, LANES), lambda i: (0, i, 0)),
    )
    return pl.pallas_call(
        _proj_kernel,
        out_shape=out_shape,
        grid=(n_tiles,),
        in_specs=[
            _row_spec(D_MODEL), _mod_spec(n_lat_tiles), _const_spec((1, D_MODEL)),
            _const_spec((D_MODEL, PROJ_W)),
            _row_spec(RET_QP), _row_spec(RET_QP), _row_spec(DIFF_QW), _row_spec(DIFF_QW),
        ],
        out_specs=out_specs,
        compiler_params=_params(("arbitrary",)),
        name="mixer_proj",
    )(x, mods, g.reshape(1, D_MODEL), w_aug, *tables)


def _dft_mats(n):
    k = np.arange(n)
    ang = 2.0 * np.pi * ((k[:, None] * k[None, :]) % n) / n
    return np.cos(ang).astype(np.float32), np.sin(ang).astype(np.float32)


def _channel_dft_mats():
    c, s = _dft_mats(FOURIER_CH)
    eye = np.eye(FOURIER_GROUPS, dtype=np.float32)
    return np.kron(eye, c), np.kron(eye, s)


def _fourier_stage1_kernel(x_ref, cc_ref, sc_ref, c1_ref, s1_ref, twc_ref, tws_ref, tr_ref, ti_ref, *, nb):
    cc, sc = cc_ref[...], sc_ref[...]
    zr, zi = [], []
    for j in range(nb):
        u = x_ref[:, j * F_W:(j + 1) * F_W]
        zr.append(_dot3(u, cc))
        zi.append(-_dot3(u, sc))
    zr = jnp.concatenate(zr, axis=1) if nb > 1 else zr[0]
    zi = jnp.concatenate(zi, axis=1) if nb > 1 else zi[0]
    c1, s1 = c1_ref[...], s1_ref[...]
    tr = _dot3(c1, zr) + _dot3(s1, zi)
    ti = _dot3(c1, zi) - _dot3(s1, zr)
    twc, tws = twc_ref[...], tws_ref[...]
    tr_ref[...] = tr * twc + ti * tws
    ti_ref[...] = ti * twc - tr * tws


def _fourier_stage2_kernel(tr_ref, ti_ref, c2_ref, s2_ref, o_ref, *, kb):
    c2, s2 = c2_ref[...], s2_ref[...]
    for j in range(kb):
        o_ref[j] = _dot3(c2, tr_ref[j]) + _dot3(s2, ti_ref[j])


def _fourier_latent(u, twc, tws):
    seq = u.shape[0]
    n1, n2 = DFT_N1, seq // DFT_N1
    nb = min(8, n2)
    kb = 8
    cc, sc = _channel_dft_mats()
    c1, s1 = _dft_mats(n1)
    c2, s2 = _dft_mats(n2)
    x2 = u.reshape(n1, n2 * F_W)
    blk = pl.BlockSpec((n1, nb * F_W), lambda i: (0, i))
    tr, ti = pl.pallas_call(
        functools.partial(_fourier_stage1_kernel, nb=nb),
        out_shape=(jax.ShapeDtypeStruct((n1, n2 * F_W), F32),) * 2,
        grid=(n2 // nb,),
        in_specs=[blk, _const_spec((F_W, F_W)), _const_spec((F_W, F_W)),
                  _const_spec((n1, n1)), _const_spec((n1, n1)), blk, blk],
        out_specs=(blk, blk),
        compiler_params=_params(("arbitrary",)),
        name="fourier_stage1",
    )(x2, cc, sc, c1, s1, twc, tws)
    tr3 = tr.reshape(n1, n2, F_W)
    ti3 = ti.reshape(n1, n2, F_W)
    blk3 = pl.BlockSpec((kb, n2, F_W), lambda i: (i, 0, 0))
    o3 = pl.pallas_call(
        functools.partial(_fourier_stage2_kernel, kb=kb),
        out_shape=jax.ShapeDtypeStruct((n1, n2, F_W), F32),
        grid=(n1 // kb,),
        in_specs=[blk3, blk3, _const_spec((n2, n2)), _const_spec((n2, n2))],
        out_specs=blk3,
        compiler_params=_params(("arbitrary",)),
        name="fourier_stage2",
    )(tr3, ti3, c2, s2)
    return jnp.transpose(o3, (1, 0, 2)).reshape(seq, F_W)


def _twiddles(seq):
    n1, n2 = DFT_N1, seq // DFT_N1
    k1 = jnp.arange(n1, dtype=jnp.int32)[:, None]
    m2 = jnp.arange(n2, dtype=jnp.int32)[None, :]
    ang = (2.0 * math.pi / seq) * ((k1 * m2) % seq).astype(F32)
    scale = 1.0 / math.sqrt(seq * FOURIER_CH)
    twc = jnp.repeat(jnp.cos(ang) * scale, F_W, axis=1)
    tws = jnp.repeat(jnp.sin(ang) * scale, F_W, axis=1)
    return twc, tws


def _fourier_ctx_kernel(u_ref, cc_ref, sc_ref, cl_ref, sl_ref, o_ref, *, scale):
    u = u_ref[...]
    a = _dot3(u, cc_ref[...])
    b = _dot3(u, sc_ref[...])
    o_ref[...] = (_dot3(cl_ref[...], a) - _dot3(sl_ref[...], b)) * scale


def _fourier_ctx(u):
    n = u.shape[0]
    cc, sc = _channel_dft_mats()
    cl, sl = _dft_mats(n)
    return pl.pallas_call(
        functools.partial(_fourier_ctx_kernel, scale=1.0 / math.sqrt(n * FOURIER_CH)),
        out_shape=jax.ShapeDtypeStruct((n, F_W), F32),
        name="fourier_ctx",
    )(u, cc, sc, cl, sl)


def _ret_dir(q, kt, v, s_ref, d_ref, qdec, kdec, cd, bdmask, hmask_ref, vmask_ref):
    pieces = []
    vparts = []
    for h in range(RET_HEADS):
        qh = q * hmask_ref[h]
        pieces.append((_dot(qh, kt) * d_ref[h]).astype(BF16))
        vparts.append(v * vmask_ref[h])
    inner = jnp.concatenate(pieces, axis=1)
    vbd = jnp.concatenate(vparts, axis=0)
    s = s_ref[...]
    o = _dot(inner, vbd) + _dot(q, s.astype(BF16)) * qdec
    kd = (kt.astype(F32) * kdec).astype(BF16)
    s_ref[...] = s * cd + bdmask * _dot(kd, v)
    return o


def _ret_kernel(logg_ref, lgv_ref, lgk_ref, bdmask_ref, hmask_ref, vmask_ref,
                qf_ref, ktf_ref, vf_ref, qb_ref, ktb_ref, vb_ref,
                of_ref, ob_ref,
                sf_ref, sb_ref, df_ref, db_ref, qdf_ref, qdb_ref, kdf_ref, kdb_ref, cdf_ref, cdb_ref):
    c = RET_CHUNK

    @pl.when(pl.program_id(0) == 0)
    def _():
        sf_ref[...] = jnp.zeros_like(sf_ref)
        sb_ref[...] = jnp.zeros_like(sb_ref)
        ii = lax.broadcasted_iota(jnp.int32, (c, c), 0).astype(F32)
        jj = lax.broadcasted_iota(jnp.int32, (c, c), 1).astype(F32)
        for h in range(RET_HEADS):
            df_ref[h] = jnp.where(ii >= jj, jnp.exp(logg_ref[0, h] * jnp.maximum(ii - jj, 0.0)), 0.0)
            db_ref[h] = jnp.where(jj >= ii, jnp.exp(logg_ref[1, h] * jnp.maximum(jj - ii, 0.0)), 0.0)
        ri = lax.broadcasted_iota(jnp.int32, (c, RET_VW), 0).astype(F32)
        qdf_ref[...] = jnp.exp(lgv_ref[0] * (ri + 1.0))
        qdb_ref[...] = jnp.exp(lgv_ref[1] * (c - ri))
        cj = lax.broadcasted_iota(jnp.int32, (RET_QP, c), 1).astype(F32)
        kdf_ref[...] = jnp.exp(lgk_ref[0] * (c - 1.0 - cj))
        kdb_ref[...] = jnp.exp(lgk_ref[1] * cj)
        cdf_ref[...] = jnp.exp(lgv_ref[0] * float(c))
        cdb_ref[...] = jnp.exp(lgv_ref[1] * float(c))

    bdmask = bdmask_ref[...]
    of_ref[...] = _ret_dir(qf_ref[...], ktf_ref[...], vf_ref[...], sf_ref, df_ref,
                           qdf_ref[...], kdf_ref[...], cdf_ref[...], bdmask, hmask_ref, vmask_ref)
    ob_ref[...] = _ret_dir(qb_ref[...], ktb_ref[...], vb_ref[...], sb_ref, db_ref,
                           qdb_ref[...], kdb_ref[...], cdb_ref[...], bdmask, hmask_ref, vmask_ref)


def _ret_masks():
    hm = np.zeros((RET_HEADS, 1, RET_QP), np.float32)
    vm = np.zeros((RET_HEADS, 1, RET_VW), np.float32)
    bd = np.zeros((RET_QP, RET_VW), np.float32)
    for h in range(RET_HEADS):
        hm[h, 0, h * RET_QK:(h + 1) * RET_QK] = 1.0
        vm[h, 0, h * RET_V:(h + 1) * RET_V] = 1.0
        bd[h * RET_QK:(h + 1) * RET_QK, h * RET_V:(h + 1) * RET_V] = 1.0
    return jnp.asarray(bd), jnp.asarray(hm, BF16), jnp.asarray(vm, BF16)


def _retention_call(rq, rkt, rv, log_g2, n_lat, n_ctx, n_pad):
    c = RET_CHUNK
    n_real = n_lat + n_ctx
    n = n_real + n_pad

    def fwd(i):
        return jnp.where(i < n_ctx, n_lat + i, jnp.where(i < n_real, i - n_ctx, i))

    def bwd(i):
        return jnp.where(i < n_real, n_real - 1 - i, i)

    lgv = jnp.repeat(log_g2, RET_V, axis=1).reshape(2, 1, RET_VW)
    lgk = jnp.pad(jnp.repeat(log_g2, RET_QK, axis=1), ((0, 0), (0, RET_QP - RET_QW)))
    lgk = jnp.broadcast_to(lgk[:, :, None], (2, RET_QP, c))
    bd, hm, vm = _ret_masks()

    def specs(ix):
        return [pl.BlockSpec((c, RET_QP), lambda i: (ix(i), 0)),
                pl.BlockSpec((RET_QP, c), lambda i: (0, ix(i))),
                pl.BlockSpec((c, RET_VW), lambda i: (ix(i), 0))]

    vmem = pltpu.VMEM
    return pl.pallas_call(
        _ret_kernel,
        out_shape=(jax.ShapeDtypeStruct((n * c, RET_VW), F32),) * 2,
        grid=(n,),
        in_specs=[pl.BlockSpec(memory_space=pltpu.SMEM),
                  _const_spec((2, 1, RET_VW)), _const_spec((2, RET_QP, c)),
                  _const_spec((RET_QP, RET_VW)), _const_spec((RET_HEADS, 1, RET_QP)),
                  _const_spec((RET_HEADS, 1, RET_VW))] + specs(fwd) + specs(bwd),
        out_specs=(pl.BlockSpec((c, RET_VW), lambda i: (fwd(i), 0)),
                   pl.BlockSpec((c, RET_VW), lambda i: (bwd(i), 0))),
        scratch_shapes=[vmem((RET_QP, RET_VW), F32), vmem((RET_QP, RET_VW), F32),
                        vmem((RET_HEADS, c, c), F32), vmem((RET_HEADS, c, c), F32),
                        vmem((c, RET_VW), F32), vmem((c, RET_VW), F32),
                        vmem((RET_QP, c), F32), vmem((RET_QP, c), F32),
                        vmem((1, RET_VW), F32), vmem((1, RET_VW), F32)],
        compiler_params=_params(("arbitrary",)),
        name="retention",
    )(log_g2, lgv, lgk, bd, hm, vm, rq, rkt, rv, rq, rkt, rv)


def _attn_kernel(lam_ref, q_ref, kt_ref, ktn_ref, v_ref, subln_ref, o_ref, qm_ref, m_ref, acc_ref, s0_ref,
                 *, out_scale, unit_rows):
    ki = pl.program_id(2)

    @pl.when(ki == 0)
    def _():
        q = q_ref[...]
        lane = lax.broadcasted_iota(jnp.int32, q.shape, 1)
        for j in range(4):
            lo = j * DIFF_QK
            qm_ref[j] = jnp.where((lane >= lo) & (lane < lo + DIFF_QK), q, jnp.zeros_like(q))
        m_ref[...] = jnp.full(m_ref.shape, -jnp.inf, F32)
        acc_ref[...] = jnp.zeros_like(acc_ref)
        s0_ref[...] = _dot(qm_ref[0, :unit_rows], kt_ref[...])

    units = [(j, r * unit_rows) for j in range(4) for r in range(q_ref.shape[0] // unit_rows)]
    kt = kt_ref[...]
    s_next = s0_ref[...]
    for u, (j, r0) in enumerate(units):
        rows = slice(r0, r0 + unit_rows)
        s = s_next
        if u + 1 < len(units):
            jn, rn = units[u + 1]
            s_next = _dot(qm_ref[jn, rn:rn + unit_rows], kt)
        else:
            s_next = _dot(qm_ref[0, :unit_rows], ktn_ref[...])
        m_prev = m_ref[j, rows]
        m_new = jnp.maximum(m_prev, jnp.max(s, axis=1, keepdims=True))
        p = jnp.exp2(s - m_new)
        alpha = jnp.exp2(m_prev - m_new)
        acc_ref[j, rows] = alpha * acc_ref[j, rows] + _dot(p.astype(BF16), v_ref[j // 2])
        m_ref[j, rows] = m_new
    s0_ref[...] = s_next

    @pl.when(ki == pl.num_programs(2) - 1)
    def _():
        lam = lam_ref[0]
        outs = []
        for hl in range(2):
            a1 = acc_ref[2 * hl]
            a2 = acc_ref[2 * hl + 1]
            o = (a1[:, :DIFF_V] / a1[:, DIFF_V:DIFF_V + 1]
                 - lam * (a2[:, :DIFF_V] / a2[:, DIFF_V:DIFF_V + 1]))
            ms = jnp.mean(o * o, axis=-1, keepdims=True)
            outs.append(o * lax.rsqrt(ms + EPS) * subln_ref[...] * out_scale)
        o_ref[...] = jnp.concatenate(outs, axis=1)


def _attn_call(dq, kt, v_aug, lam, subln, lam_init, n_q, tq, tk, nk, q_off=0, k_off=0):
    unit_rows = min(tq, ATTN_UNIT_ROWS)
    return pl.pallas_call(
        functools.partial(_attn_kernel, out_scale=1.0 - lam_init, unit_rows=unit_rows),
        out_shape=jax.ShapeDtypeStruct((n_q, DIFF_VW), F32),
        grid=(DIFF_HEADS // 2, n_q // tq, nk),
        in_specs=[
            pl.BlockSpec(memory_space=pltpu.SMEM),
            pl.BlockSpec((tq, LANES), lambda h, i, k: (q_off + i, h)),
            pl.BlockSpec((LANES, tk), lambda h, i, k: (h, k_off + k)),
            pl.BlockSpec((LANES, tk), lambda h, i, k: (h, k_off + jnp.minimum(k + 1, nk - 1))),
            pl.BlockSpec((2, tk, LANES), lambda h, i, k: (h, k_off + k, 0)),
            pl.BlockSpec((1, DIFF_V), lambda h, i, k: (0, 0)),
        ],
        out_specs=pl.BlockSpec((tq, LANES), lambda h, i, k: (i, h)),
        scratch_shapes=[pltpu.VMEM((4, tq, LANES), BF16),
                        pltpu.VMEM((4, tq, 1), F32),
                        pltpu.VMEM((4, tq, LANES), F32),
                        pltpu.VMEM((unit_rows, tk), F32)],
        compiler_params=_params(("arbitrary", "arbitrary", "arbitrary")),
        name="diff_attn",
    )(lam, dq, kt, kt, v_aug, subln.reshape(1, DIFF_V))


def _merge_kernel(x_ref, mod_ref, g_ref, flat_ref, ftail_ref, of_ref, ob_ref, rg_ref, dlat_ref, dtail_ref,
                  wgt_ref, wbf_ref, wbr_ref, wbd_ref, wo_ref, bd_ref, o_ref, *, n_lat_tiles):
    x = x_ref[...]
    hb = _norm_mod(x, g_ref[...], mod_ref[0, 3:4, :], mod_ref[0, 4:5, :]).astype(BF16)
    gates = _sigmoid(_dot(hb, wgt_ref[...]))
    is_tail = pl.program_id(0) >= n_lat_tiles
    f = jnp.where(is_tail, ftail_ref[...], flat_ref[...])
    d = jnp.where(is_tail, dtail_ref[...], dlat_ref[...])
    r = of_ref[...] + ob_ref[...]
    rr_hi, rr_lo = _split_bf16(r * r)
    ms = _dot(rr_hi, bd_ref[...]) + _dot(rr_lo, bd_ref[...])
    rg = rg_ref[...]
    yr = r * lax.rsqrt(ms + EPS) * (rg * _sigmoid(rg))
    mixed = (gates[:, :D_MODEL] * _dot(f.astype(BF16), wbf_ref[...])
             + gates[:, D_MODEL:2 * D_MODEL] * _dot(yr.astype(BF16), wbr_ref[...])
             + gates[:, 2 * D_MODEL:] * _dot(d.astype(BF16), wbd_ref[...]))
    y = _dot(mixed.astype(BF16), wo_ref[...])
    o_ref[...] = x + mod_ref[0, 5:6, :] * y


def _merge_call(x, mods, g, f_lat, f_tail, o_f, o_b, rg, d_lat, d_tail, wgt, wbf, wbr, wbd, wo,
                n_lat_tiles, n_tiles):
    tm = TOKEN_TILE
    bd = np.kron(np.eye(RET_HEADS, dtype=np.float32), np.full((RET_V, RET_V), 1.0 / RET_V, np.float32))

    def lat_spec(w):
        return pl.BlockSpec((tm, w), lambda i: (jnp.minimum(i, n_lat_tiles - 1), 0))

    return pl.pallas_call(
        functools.partial(_merge_kernel, n_lat_tiles=n_lat_tiles),
        out_shape=jax.ShapeDtypeStruct((n_tiles * tm, D_MODEL), F32),
        grid=(n_tiles,),
        in_specs=[
            _row_spec(D_MODEL), _mod_spec(n_lat_tiles), _const_spec((1, D_MODEL)),
            lat_spec(F_W), _const_spec((tm, F_W)),
            _row_spec(RET_VW), _row_spec(RET_VW), _row_spec(RET_VW),
            lat_spec(DIFF_VW), _const_spec((tm, DIFF_VW)),
            _const_spec((D_MODEL, GATE_W)), _const_spec((F_W, D_MODEL)), _const_spec((RET_VW, D_MODEL)),
            _const_spec((DIFF_VW, D_MODEL)), _const_spec((D_MODEL, D_MODEL)), _const_spec((RET_VW, RET_VW)),
        ],
        out_specs=_row_spec(D_MODEL),
        compiler_params=_params(("arbitrary",)),
        name="merge",
    )(x, mods, g.reshape(1, D_MODEL), f_lat, f_tail, o_f, o_b, rg, d_lat, d_tail, wgt, wbf, wbr, wbd, wo,
      jnp.asarray(bd, BF16))


def _pick_tile(n, candidates):
    for c in candidates:
        if n % c == 0:
            return c
    raise ValueError(f"no tile for {n}")


def kernel(x, c, ctx, c_ctx, w_ada, b_ada, norm_g, ffn_w1, ffn_w3, ffn_w2, w_in, ret_decay_logit,
           diff_lambda, diff_subln, w_branch_f, w_branch_r, w_branch_d, w_out, final_g):
    batch, seq, d = x.shape
    ctx_len = ctx.shape[1]
    tm = TOKEN_TILE
    assert batch == 1 and d == D_MODEL
    assert seq % max(DFT_N1 * 8, tm) == 0 and ctx_len % 256 == 0 and ctx_len <= tm
    total = seq + ctx_len
    n_lat_tiles = seq // tm
    n_tiles = n_lat_tiles + 1
    n_rows = n_tiles * tm
    n_lat_chunks = seq // RET_CHUNK
    n_ctx_chunks = ctx_len // RET_CHUNK
    n_pad_chunks = (n_rows - total) // RET_CHUNK

    cc = jnp.zeros((8, D_MODEL), F32).at[0].set(c[0]).at[1].set(c_ctx)
    mods_all = _ada_call(cc, w_ada, b_ada)[:, :2].reshape(DEPTH, 2, N_MOD, D_MODEL)

    tables = _rope_tables(seq, n_rows)
    twc, tws = _twiddles(seq)
    log_g2_all = jax.nn.log_sigmoid(ret_decay_logit.astype(F32))
    lv = diff_lambda.astype(F32)
    w_aug_all, wgt_all = _prep_proj_weights(w_in)
    w1_all, w3_all, w2_all = ffn_w1.astype(BF16), ffn_w3.astype(BF16), ffn_w2.astype(BF16)
    wbf_all, wbr_all = w_branch_f.astype(BF16), w_branch_r.astype(BF16)
    wbd_all, wo_all = w_branch_d.astype(BF16), w_out.astype(BF16)

    tq = _pick_tile(seq, (1024, 512, 256, 128))
    tk = _pick_tile(total, (1280, 640, 256, 128))
    tail_pad = ((0, tm - ctx_len), (0, 0))

    xs = jnp.concatenate([x[0], ctx[0], jnp.zeros((n_rows - total, D_MODEL), F32)], axis=0)
    for l in range(DEPTH):
        last = l == DEPTH - 1
        lam_init = 0.8 - 0.6 * math.exp(-0.3 * l)
        mods = mods_all[l]
        lam = (jnp.exp(jnp.sum(lv[l, 0] * lv[l, 1])) - jnp.exp(jnp.sum(lv[l, 2] * lv[l, 3]))
               + lam_init).reshape(1)

        xs = _ffn_call(xs, mods, norm_g[l, 0], w1_all[l, 0], w3_all[l, 0], w2_all[l, 0], 0,
                       n_lat_tiles, n_tiles)

        uf, rq, rkt, rv, rg, dq, dkt, v_aug = _proj_call(xs, mods, norm_g[l, 1], w_aug_all[l], tables,
                                                          n_lat_tiles, n_tiles)

        f_lat = _fourier_latent(uf[:seq], twc, tws)
        o_f, o_b = _retention_call(rq, rkt, rv, log_g2_all[l], n_lat_chunks, n_ctx_chunks, n_pad_chunks)
        d_lat = _attn_call(dq, dkt, v_aug, lam, diff_subln[l], lam_init, seq, tq, tk, total // tk)

        if last:
            f_tail = jnp.zeros((tm, F_W), F32)
            d_tail = jnp.zeros((tm, DIFF_VW), F32)
            n_out = n_lat_tiles
        else:
            f_tail = jnp.pad(_fourier_ctx(uf[seq:total]), tail_pad)
            d_ctx = _attn_call(dq, dkt, v_aug, lam, diff_subln[l], lam_init, ctx_len, ctx_len, ctx_len, 1,
                               q_off=seq // ctx_len, k_off=seq // ctx_len)
            d_tail = jnp.pad(d_ctx, tail_pad)
            n_out = n_tiles

        xs = _merge_call(xs, mods, norm_g[l, 1], f_lat, f_tail, o_f, o_b, rg, d_lat, d_tail, wgt_all[l],
                         wbf_all[l], wbr_all[l], wbd_all[l], wo_all[l], n_lat_tiles, n_out)

        xs = _ffn_call(xs, mods, norm_g[l, 2], w1_all[l, 1], w3_all[l, 1], w2_all[l, 1], 6,
                       n_lat_tiles, n_out, final_g=final_g if last else None)

    return xs.reshape(1, seq, D_MODEL)
```

```python
import functools
import math

import numpy as np
import jax
import jax.numpy as jnp
from jax import lax
from jax.experimental import pallas as pl
from jax.experimental.pallas import tpu as pltpu

D_MODEL = 1024
DEPTH = 4
GRID_W = 64
D_FF = 2816
N_MOD = 9
FOURIER_GROUPS = 4
FOURIER_CH = 64
RET_HEADS = 6
RET_QK = 32
RET_V = 64
RET_CHUNK = 128
DIFF_HEADS = 6
DIFF_QK = 32
DIFF_V = 64
ROPE_BASE = 10000.0
EPS = 1e-6
F_W = FOURIER_GROUPS * FOURIER_CH
RET_QW = RET_HEADS * RET_QK
RET_VW = RET_HEADS * RET_V
DIFF_QW = DIFF_HEADS * 2 * DIFF_QK
DIFF_VW = DIFF_HEADS * DIFF_V
GATE_W = 3 * D_MODEL

LANES = 128
VMEM_LIMIT_BYTES = 56 * 1024 * 1024

RET_QP = 2 * LANES
TOKEN_TILE = 512
DFT_N1 = 128
ATTN_KEY_CHUNK = 256
ATTN_VT_ROWS = DIFF_V + 16

BF16 = jnp.bfloat16
F32 = jnp.float32
LOG2E = math.log2(math.e)


def _dot(a, b):
    return jnp.dot(a, b, preferred_element_type=F32)


def _split_bf16(x):
    hi = x.astype(BF16)
    lo = (x - hi.astype(F32)).astype(BF16)
    return hi, lo


def _dot3(a, b):
    ah, al = _split_bf16(a)
    bh, bl = _split_bf16(b)
    return _dot(ah, bh) + _dot(al, bh) + _dot(ah, bl)


def _norm_mod(x, g, shift, scale):
    ms = jnp.mean(x * x, axis=-1, keepdims=True)
    y = x * lax.rsqrt(ms + EPS) * g
    return y * (1.0 + scale) + shift


def _sigmoid(x):
    return 1.0 / (1.0 + jnp.exp(-x))


def _const_spec(shape):
    nd = len(shape)
    return pl.BlockSpec(shape, lambda *_: (0,) * nd, pipeline_mode=pl.Buffered(1))


def _params(sem):
    return pltpu.CompilerParams(dimension_semantics=sem, vmem_limit_bytes=VMEM_LIMIT_BYTES)


def _row_spec(width):
    return pl.BlockSpec((TOKEN_TILE, width), lambda i: (i, 0))


def _ada_kernel(cc_ref, w_ref, b_ref, o_ref):
    cc = cc_ref[...]
    s = cc * _sigmoid(cc)
    o_ref[0] = _dot3(s, w_ref[0]) + b_ref[0]


def _ada_call(cc, w_ada, b_ada):
    depth, d, n = w_ada.shape
    tn = 1152
    return pl.pallas_call(
        _ada_kernel,
        out_shape=jax.ShapeDtypeStruct((depth, 8, n), F32),
        grid=(depth, n // tn),
        in_specs=[
            pl.BlockSpec((8, d), lambda l, j: (0, 0)),
            pl.BlockSpec((1, d, tn), lambda l, j: (l, 0, j)),
            pl.BlockSpec((1, 1, tn), lambda l, j: (l, 0, j)),
        ],
        out_specs=pl.BlockSpec((1, 8, tn), lambda l, j: (l, 0, j)),
        compiler_params=_params(("arbitrary", "arbitrary")),
        name="adaln",
    )(cc, w_ada, b_ada.reshape(depth, 1, n))


def _mod_spec(n_lat_tiles):
    return pl.BlockSpec((1, N_MOD, D_MODEL), lambda i: (jnp.where(i >= n_lat_tiles, 1, 0), 0, 0))


def _ffn_tile(x, mod_ref, g_ref, w1_ref, w3_ref, w2_ref, base):
    shift = mod_ref[0, base:base + 1, :]
    scale = mod_ref[0, base + 1:base + 2, :]
    gate = mod_ref[0, base + 2:base + 3, :]
    hb = _norm_mod(x, g_ref[...], shift, scale).astype(BF16)
    a = _dot(hb, w1_ref[...])
    b = _dot(hb, w3_ref[...])
    u = (a * _sigmoid(a) * b).astype(BF16)
    return x + (0.5 * gate) * _dot(u, w2_ref[...])


def _ffn_kernel(x_ref, mod_ref, g_ref, w1_ref, w3_ref, w2_ref, o_ref, *, base):
    o_ref[...] = _ffn_tile(x_ref[...], mod_ref, g_ref, w1_ref, w3_ref, w2_ref, base)


def _ffn_final_kernel(x_ref, mod_ref, g_ref, w1_ref, w3_ref, w2_ref, fg_ref, o_ref, *, base):
    y = _ffn_tile(x_ref[...], mod_ref, g_ref, w1_ref, w3_ref, w2_ref, base)
    ms = jnp.mean(y * y, axis=-1, keepdims=True)
    o_ref[...] = y * lax.rsqrt(ms + EPS) * fg_ref[...]


def _ffn_call(x, mods, g, w1, w3, w2, base, n_lat_tiles, n_tiles, final_g=None):
    in_specs = [
        _row_spec(D_MODEL), _mod_spec(n_lat_tiles), _const_spec((1, D_MODEL)),
        _const_spec((D_MODEL, D_FF)), _const_spec((D_MODEL, D_FF)), _const_spec((D_FF, D_MODEL)),
    ]
    args = [x, mods, g.reshape(1, D_MODEL), w1, w3, w2]
    body = _ffn_kernel
    if final_g is not None:
        in_specs.append(_const_spec((1, D_MODEL)))
        args.append(final_g.reshape(1, D_MODEL))
        body = _ffn_final_kernel
    return pl.pallas_call(
        functools.partial(body, base=base),
        out_shape=jax.ShapeDtypeStruct((n_tiles * TOKEN_TILE, D_MODEL), F32),
        grid=(n_tiles,),
        in_specs=in_specs,
        out_specs=_row_spec(D_MODEL),
        compiler_params=_params(("arbitrary",)),
        name="ffn",
    )(*args)


_C_F, _C_RQ, _C_RK, _C_RV, _C_RG = 0, 256, 512, 768, 1152
_C_DQ, _C_DK, _C_DV = 1536, 1920, 2304
_C_RQR, _C_RKR, _C_DQR, _C_DKR = 2688, 2944, 3200, 3584
PROJ_W = 3968


def _rotate_half_cols(w, block):
    depth, d, width = w.shape
    w5 = w.reshape(depth, d, width // block, 2, block // 2)
    return jnp.concatenate([-w5[:, :, :, 1:2], w5[:, :, :, 0:1]], axis=3).reshape(depth, d, width)


def _prep_proj_weights(w_in):
    cuts = np.cumsum([F_W, RET_QW, RET_QW, RET_VW, RET_VW, DIFF_QW, DIFF_QW, DIFF_VW])
    wf, wrq, wrk, wrv, wrg, wdq, wdk, wdv, wgt = jnp.split(w_in, cuts, axis=2)
    z = jnp.zeros(w_in.shape[:2] + (RET_QP - RET_QW,), w_in.dtype)
    parts = [wf, wrq, z, wrk, z, wrv, wrg, wdq, wdk, wdv,
             _rotate_half_cols(wrq, RET_QK), z, _rotate_half_cols(wrk, RET_QK), z,
             _rotate_half_cols(wdq, DIFF_QK // 2), _rotate_half_cols(wdk, DIFF_QK // 2)]
    return jnp.concatenate(parts, axis=2).astype(BF16), wgt.astype(BF16)


def _rope_tables(seq, n_rows):
    pos = jnp.arange(seq, dtype=F32)
    inv_r = ROPE_BASE ** (-jnp.arange(0, RET_QK, 2, dtype=F32) / RET_QK)
    ang_r = pos[:, None] * inv_r[None, :]
    cos_r = jnp.tile(jnp.cos(ang_r), (1, 2 * RET_HEADS))
    sin_r = jnp.tile(jnp.sin(ang_r), (1, 2 * RET_HEADS))
    rows = jnp.repeat(jnp.arange(seq // GRID_W, dtype=F32), GRID_W)
    cols = jnp.tile(jnp.arange(GRID_W, dtype=F32), seq // GRID_W)
    dim = DIFF_QK // 2
    inv_d = ROPE_BASE ** (-jnp.arange(0, dim, 2, dtype=F32) / dim)
    a_row = rows[:, None] * inv_d[None, :]
    a_col = cols[:, None] * inv_d[None, :]
    cos_hm = jnp.concatenate([jnp.cos(a_row)] * 2 + [jnp.cos(a_col)] * 2, axis=1)
    sin_hm = jnp.concatenate([jnp.sin(a_row)] * 2 + [jnp.sin(a_col)] * 2, axis=1)
    cos_d = jnp.tile(cos_hm, (1, 2 * DIFF_HEADS))
    sin_d = jnp.tile(sin_hm, (1, 2 * DIFF_HEADS))

    def finish(t, width, fill):
        t = jnp.pad(t, ((0, 0), (0, width - t.shape[1])), constant_values=fill)
        return jnp.pad(t, ((0, n_rows - seq), (0, 0)), constant_values=fill)

    return (finish(cos_r, RET_QP, 1.0), finish(sin_r, RET_QP, 0.0),
            finish(cos_d, DIFF_QW, 1.0), finish(sin_d, DIFF_QW, 0.0))


def _proj_kernel(x_ref, mod_ref, g_ref, w_ref, cr_ref, sr_ref, cd_ref, sd_ref,
                 uf_ref, rq_ref, rkt_ref, rv_ref, rg_ref, dqt_ref, dk_ref, vta_ref):
    x = x_ref[...]
    hb = _norm_mod(x, g_ref[...], mod_ref[0, 3:4, :], mod_ref[0, 4:5, :]).astype(BF16)
    p = _dot(hb, w_ref[...])
    cr, sr, cd, sd = cr_ref[...], sr_ref[...], cd_ref[...], sd_ref[...]
    uf_ref[...] = p[:, _C_F:_C_F + F_W]
    rq = p[:, _C_RQ:_C_RQ + RET_QP] * cr + p[:, _C_RQR:_C_RQR + RET_QP] * sr
    rk = p[:, _C_RK:_C_RK + RET_QP] * cr + p[:, _C_RKR:_C_RKR + RET_QP] * sr
    rq_ref[...] = rq.astype(BF16)
    rkt_ref[...] = (rk * (RET_QK ** -0.5)).T.astype(BF16)
    rv_ref[...] = p[:, _C_RV:_C_RV + RET_VW].astype(BF16)
    rg_ref[...] = p[:, _C_RG:_C_RG + RET_VW]
    dq = p[:, _C_DQ:_C_DQ + DIFF_QW] * cd + p[:, _C_DQR:_C_DQR + DIFF_QW] * sd
    dk = p[:, _C_DK:_C_DK + DIFF_QW] * cd + p[:, _C_DKR:_C_DKR + DIFF_QW] * sd
    dqt_ref[...] = (dq * ((DIFF_QK ** -0.5) * LOG2E)).T.astype(BF16)
    dk_ref[...] = dk.astype(BF16)
    vt = p[:, _C_DV:_C_DV + DIFF_VW].T
    ones = jnp.ones((ATTN_VT_ROWS - DIFF_V, x.shape[0]), F32)
    for h in range(DIFF_HEADS):
        vta_ref[h] = jnp.concatenate([vt[h * DIFF_V:(h + 1) * DIFF_V], ones], axis=0).astype(BF16)


def _proj_call(x, mods, g, w_aug, tables, n_lat_tiles, n_tiles):
    tm = TOKEN_TILE
    t = n_tiles * tm
    out_shape = (
        jax.ShapeDtypeStruct((t, F_W), F32),
        jax.ShapeDtypeStruct((t, RET_QP), BF16),
        jax.ShapeDtypeStruct((RET_QP, t), BF16),
        jax.ShapeDtypeStruct((t, RET_VW), BF16),
        jax.ShapeDtypeStruct((t, RET_VW), F32),
        jax.ShapeDtypeStruct((DIFF_QW, t), BF16),
        jax.ShapeDtypeStruct((t, DIFF_QW), BF16),
        jax.ShapeDtypeStruct((DIFF_HEADS, ATTN_VT_ROWS, t), BF16),
    )
    out_specs = (
        _row_spec(F_W), _row_spec(RET_QP),
        pl.BlockSpec((RET_QP, tm), lambda i: (0, i)),
        _row_spec(RET_VW), _row_spec(RET_VW),
        pl.BlockSpec((DIFF_QW, tm), lambda i: (0, i)),
        _row_spec(DIFF_QW),
        pl.BlockSpec((DIFF_HEADS, ATTN_VT_ROWS, tm), lambda i: (0, 0, i)),
    )
    return pl.pallas_call(
        _proj_kernel,
        out_shape=out_shape,
        grid=(n_tiles,),
        in_specs=[
            _row_spec(D_MODEL), _mod_spec(n_lat_tiles), _const_spec((1, D_MODEL)),
            _const_spec((D_MODEL, PROJ_W)),
            _row_spec(RET_QP), _row_spec(RET_QP), _row_spec(DIFF_QW), _row_spec(DIFF_QW),
        ],
        out_specs=out_specs,
        compiler_params=_params(("arbitrary",)),
        name="mixer_proj",
    )(x, mods, g.reshape(1, D_MODEL), w_aug, *tables)


def _dft_mats(n):
    k = np.arange(n)
    ang = 2.0 * np.pi * ((k[:, None] * k[None, :]) % n) / n
    return np.cos(ang).astype(np.float32), np.sin(ang).astype(np.float32)


def _channel_dft_mats():
    c, s = _dft_mats(FOURIER_CH)
    eye = np.eye(FOURIER_GROUPS, dtype=np.float32)
    return np.kron(eye, c), np.kron(eye, s)


def _fourier_stage1_kernel(x_ref, cc_ref, sc_ref, c1_ref, s1_ref, twc_ref, tws_ref, tr_ref, ti_ref, *, nb):
    cc, sc = cc_ref[...], sc_ref[...]
    zr, zi = [], []
    for j in range(nb):
        u = x_ref[:, j * F_W:(j + 1) * F_W]
        zr.append(_dot3(u, cc))
        zi.append(-_dot3(u, sc))
    zr = jnp.concatenate(zr, axis=1) if nb > 1 else zr[0]
    zi = jnp.concatenate(zi, axis=1) if nb > 1 else zi[0]
    c1, s1 = c1_ref[...], s1_ref[...]
    tr = _dot3(c1, zr) + _dot3(s1, zi)
    ti = _dot3(c1, zi) - _dot3(s1, zr)
    twc, tws = twc_ref[...], tws_ref[...]
    tr_ref[...] = tr * twc + ti * tws
    ti_ref[...] = ti * twc - tr * tws


def _fourier_stage2_kernel(tr_ref, ti_ref, c2_ref, s2_ref, o_ref, *, kb):
    c2, s2 = c2_ref[...], s2_ref[...]
    for j in range(kb):
        o_ref[j] = _dot3(c2, tr_ref[j]) + _dot3(s2, ti_ref[j])


def _fourier_latent(u, twc, tws):
    seq = u.shape[0]
    n1, n2 = DFT_N1, seq // DFT_N1
    nb = min(8, n2)
    kb = 8
    cc, sc = _channel_dft_mats()
    c1, s1 = _dft_mats(n1)
    c2, s2 = _dft_mats(n2)
    x2 = u.reshape(n1, n2 * F_W)
    blk = pl.BlockSpec((n1, nb * F_W), lambda i: (0, i))
    tr, ti = pl.pallas_call(
        functools.partial(_fourier_stage1_kernel, nb=nb),
        out_shape=(jax.ShapeDtypeStruct((n1, n2 * F_W), F32),) * 2,
        grid=(n2 // nb,),
        in_specs=[blk, _const_spec((F_W, F_W)), _const_spec((F_W, F_W)),
                  _const_spec((n1, n1)), _const_spec((n1, n1)), blk, blk],
        out_specs=(blk, blk),
        compiler_params=_params(("arbitrary",)),
        name="fourier_stage1",
    )(x2, cc, sc, c1, s1, twc, tws)
    tr3 = tr.reshape(n1, n2, F_W)
    ti3 = ti.reshape(n1, n2, F_W)
    blk3 = pl.BlockSpec((kb, n2, F_W), lambda i: (i, 0, 0))
    o3 = pl.pallas_call(
        functools.partial(_fourier_stage2_kernel, kb=kb),
        out_shape=jax.ShapeDtypeStruct((n1, n2, F_W), F32),
        grid=(n1 // kb,),
        in_specs=[blk3, blk3, _const_spec((n2, n2)), _const_spec((n2, n2))],
        out_specs=blk3,
        compiler_params=_params(("arbitrary",)),
        name="fourier_stage2",
    )(tr3, ti3, c2, s2)
    return jnp.transpose(o3, (1, 0, 2)).reshape(seq, F_W)


def _twiddles(seq):
    n1, n2 = DFT_N1, seq // DFT_N1
    k1 = jnp.arange(n1, dtype=jnp.int32)[:, None]
    m2 = jnp.arange(n2, dtype=jnp.int32)[None, :]
    ang = (2.0 * math.pi / seq) * ((k1 * m2) % seq).astype(F32)
    scale = 1.0 / math.sqrt(seq * FOURIER_CH)
    twc = jnp.repeat(jnp.cos(ang) * scale, F_W, axis=1)
    tws = jnp.repeat(jnp.sin(ang) * scale, F_W, axis=1)
    return twc, tws


def _fourier_ctx_kernel(u_ref, cc_ref, sc_ref, cl_ref, sl_ref, o_ref, *, scale):
    u = u_ref[...]
    a = _dot3(u, cc_ref[...])
    b = _dot3(u, sc_ref[...])
    o_ref[...] = (_dot3(cl_ref[...], a) - _dot3(sl_ref[...], b)) * scale


def _fourier_ctx(u):
    n = u.shape[0]
    cc, sc = _channel_dft_mats()
    cl, sl = _dft_mats(n)
    return pl.pallas_call(
        functools.partial(_fourier_ctx_kernel, scale=1.0 / math.sqrt(n * FOURIER_CH)),
        out_shape=jax.ShapeDtypeStruct((n, F_W), F32),
        name="fourier_ctx",
    )(u, cc, sc, cl, sl)


def _ret_dir(q, kt, v, s_ref, d_ref, qdec, kdec, cd, bdmask, hmask_ref, vmask_ref):
    pieces = []
    vparts = []
    for h in range(RET_HEADS):
        qh = q * hmask_ref[h]
        pieces.append((_dot(qh, kt) * d_ref[h]).astype(BF16))
        vparts.append(v * vmask_ref[h])
    inner = jnp.concatenate(pieces, axis=1)
    vbd = jnp.concatenate(vparts, axis=0)
    s = s_ref[...]
    o = _dot(inner, vbd) + _dot(q, s.astype(BF16)) * qdec
    kd = (kt.astype(F32) * kdec).astype(BF16)
    s_ref[...] = s * cd + bdmask * _dot(kd, v)
    return o


def _ret_kernel(logg_ref, lgv_ref, lgk_ref, bdmask_ref, hmask_ref, vmask_ref,
                qf_ref, ktf_ref, vf_ref, qb_ref, ktb_ref, vb_ref,
                of_ref, ob_ref,
                sf_ref, sb_ref, df_ref, db_ref, qdf_ref, qdb_ref, kdf_ref, kdb_ref, cdf_ref, cdb_ref):
    c = RET_CHUNK

    @pl.when(pl.program_id(0) == 0)
    def _():
        sf_ref[...] = jnp.zeros_like(sf_ref)
        sb_ref[...] = jnp.zeros_like(sb_ref)
        ii = lax.broadcasted_iota(jnp.int32, (c, c), 0).astype(F32)
        jj = lax.broadcasted_iota(jnp.int32, (c, c), 1).astype(F32)
        for h in range(RET_HEADS):
            df_ref[h] = jnp.where(ii >= jj, jnp.exp(logg_ref[0, h] * jnp.maximum(ii - jj, 0.0)), 0.0)
            db_ref[h] = jnp.where(jj >= ii, jnp.exp(logg_ref[1, h] * jnp.maximum(jj - ii, 0.0)), 0.0)
        ri = lax.broadcasted_iota(jnp.int32, (c, RET_VW), 0).astype(F32)
        qdf_ref[...] = jnp.exp(lgv_ref[0] * (ri + 1.0))
        qdb_ref[...] = jnp.exp(lgv_ref[1] * (c - ri))
        cj = lax.broadcasted_iota(jnp.int32, (RET_QP, c), 1).astype(F32)
        kdf_ref[...] = jnp.exp(lgk_ref[0] * (c - 1.0 - cj))
        kdb_ref[...] = jnp.exp(lgk_ref[1] * cj)
        cdf_ref[...] = jnp.exp(lgv_ref[0] * float(c))
        cdb_ref[...] = jnp.exp(lgv_ref[1] * float(c))

    bdmask = bdmask_ref[...]
    of_ref[...] = _ret_dir(qf_ref[...], ktf_ref[...], vf_ref[...], sf_ref, df_ref,
                           qdf_ref[...], kdf_ref[...], cdf_ref[...], bdmask, hmask_ref, vmask_ref)
    ob_ref[...] = _ret_dir(qb_ref[...], ktb_ref[...], vb_ref[...], sb_ref, db_ref,
                           qdb_ref[...], kdb_ref[...], cdb_ref[...], bdmask, hmask_ref, vmask_ref)


def _ret_masks():
    hm = np.zeros((RET_HEADS, 1, RET_QP), np.float32)
    vm = np.zeros((RET_HEADS, 1, RET_VW), np.float32)
    bd = np.zeros((RET_QP, RET_VW), np.float32)
    for h in range(RET_HEADS):
        hm[h, 0, h * RET_QK:(h + 1) * RET_QK] = 1.0
        vm[h, 0, h * RET_V:(h + 1) * RET_V] = 1.0
        bd[h * RET_QK:(h + 1) * RET_QK, h * RET_V:(h + 1) * RET_V] = 1.0
    return jnp.asarray(bd), jnp.asarray(hm, BF16), jnp.asarray(vm, BF16)


def _retention_call(rq, rkt, rv, log_g2, n_lat, n_ctx, n_pad):
    c = RET_CHUNK
    n_real = n_lat + n_ctx
    n = n_real + n_pad

    def fwd(i):
        return jnp.where(i < n_ctx, n_lat + i, jnp.where(i < n_real, i - n_ctx, i))

    def bwd(i):
        return jnp.where(i < n_real, n_real - 1 - i, i)

    lgv = jnp.repeat(log_g2, RET_V, axis=1).reshape(2, 1, RET_VW)
    lgk = jnp.pad(jnp.repeat(log_g2, RET_QK, axis=1), ((0, 0), (0, RET_QP - RET_QW)))
    lgk = jnp.broadcast_to(lgk[:, :, None], (2, RET_QP, c))
    bd, hm, vm = _ret_masks()

    def specs(ix):
        return [pl.BlockSpec((c, RET_QP), lambda i: (ix(i), 0)),
                pl.BlockSpec((RET_QP, c), lambda i: (0, ix(i))),
                pl.BlockSpec((c, RET_VW), lambda i: (ix(i), 0))]

    vmem = pltpu.VMEM
    return pl.pallas_call(
        _ret_kernel,
        out_shape=(jax.ShapeDtypeStruct((n * c, RET_VW), F32),) * 2,
        grid=(n,),
        in_specs=[pl.BlockSpec(memory_space=pltpu.SMEM),
                  _const_spec((2, 1, RET_VW)), _const_spec((2, RET_QP, c)),
                  _const_spec((RET_QP, RET_VW)), _const_spec((RET_HEADS, 1, RET_QP)),
                  _const_spec((RET_HEADS, 1, RET_VW))] + specs(fwd) + specs(bwd),
        out_specs=(pl.BlockSpec((c, RET_VW), lambda i: (fwd(i), 0)),
                   pl.BlockSpec((c, RET_VW), lambda i: (bwd(i), 0))),
        scratch_shapes=[vmem((RET_QP, RET_VW), F32), vmem((RET_QP, RET_VW), F32),
                        vmem((RET_HEADS, c, c), F32), vmem((RET_HEADS, c, c), F32),
                        vmem((c, RET_VW), F32), vmem((c, RET_VW), F32),
                        vmem((RET_QP, c), F32), vmem((RET_QP, c), F32),
                        vmem((1, RET_VW), F32), vmem((1, RET_VW), F32)],
        compiler_params=_params(("arbitrary",)),
        name="retention",
    )(log_g2, lgv, lgk, bd, hm, vm, rq, rkt, rv, rq, rkt, rv)


def _chunk_scores(k_chunk, qm, s_out_ref, rows, run_max):
    s = _dot(k_chunk, qm)
    s_out_ref[rows, :] = s
    cm = jnp.max(s.reshape(s.shape[0] // 8, 8, s.shape[1]), axis=0)
    return cm if run_max is None else jnp.maximum(run_max, cm)


def _attn_kernel(lam_ref, qt_ref, k_ref, kn_ref, vt_ref, subln_ref, o_ref,
                 qm_ref, m_ref, acc_ref, s_ref, mx_ref, *, out_scale, key_chunk):
    ki = pl.program_id(2)
    tk = k_ref.shape[0]
    n_chunks = tk // key_chunk

    def chunk(c):
        return slice(c * key_chunk, (c + 1) * key_chunk)

    @pl.when(ki == 0)
    def _():
        qt = qt_ref[...]
        row = lax.broadcasted_iota(jnp.int32, qt.shape, 0)
        for j in range(4):
            lo = j * DIFF_QK
            qm_ref[j] = jnp.where((row >= lo) & (row < lo + DIFF_QK), qt, jnp.zeros_like(qt))
        m_ref[...] = jnp.full(m_ref.shape, -jnp.inf, F32)
        acc_ref[...] = jnp.zeros_like(acc_ref)
        run = None
        for c in range(n_chunks):
            run = _chunk_scores(k_ref[chunk(c), :], qm_ref[0], s_ref.at[0], chunk(c), run)
        mx_ref[...] = run

    run = mx_ref[...]
    for u in range(4):
        cur, nxt = u % 2, (u + 1) % 2
        m_prev = m_ref[u]
        m_new = jnp.maximum(m_prev, jnp.max(run, axis=0, keepdims=True))
        alpha = jnp.exp2(m_prev - m_new)
        vt = vt_ref[u // 2]
        run = None
        acc = None
        p_prev = None
        for c in range(n_chunks):
            k_chunk = k_ref[chunk(c), :] if u < 3 else kn_ref[chunk(c), :]
            run = _chunk_scores(k_chunk, qm_ref[(u + 1) % 4], s_ref.at[nxt], chunk(c), run)
            p = jnp.exp2(s_ref[cur, chunk(c), :] - m_new).astype(BF16)
            if p_prev is not None:
                pv = _dot(vt[:, chunk(c - 1)], p_prev)
                acc = pv if acc is None else acc + pv
            p_prev = p
        pv = _dot(vt[:, chunk(n_chunks - 1)], p_prev)
        acc = pv if acc is None else acc + pv
        acc_ref[u] = alpha * acc_ref[u] + acc
        m_ref[u] = m_new
    mx_ref[...] = run

    @pl.when(ki == pl.num_programs(2) - 1)
    def _():
        lam = lam_ref[0]
        outs = []
        for hl in range(2):
            a1 = acc_ref[2 * hl]
            a2 = acc_ref[2 * hl + 1]
            o = (a1[:DIFF_V] / a1[DIFF_V:DIFF_V + 1]
                 - lam * (a2[:DIFF_V] / a2[DIFF_V:DIFF_V + 1]))
            ms = jnp.mean(o * o, axis=0, keepdims=True)
            outs.append(o * lax.rsqrt(ms + EPS) * subln_ref[...] * out_scale)
        o_ref[...] = jnp.concatenate(outs, axis=0)


def _attn_call(dqt, dk, vt_aug, lam, subln, lam_init, n_q, tq, tk, nk, q_off=0, k_off=0):
    return pl.pallas_call(
        functools.partial(_attn_kernel, out_scale=1.0 - lam_init, key_chunk=min(tk, ATTN_KEY_CHUNK)),
        out_shape=jax.ShapeDtypeStruct((DIFF_VW, n_q), F32),
        grid=(DIFF_HEADS // 2, n_q // tq, nk),
        in_specs=[
            pl.BlockSpec(memory_space=pltpu.SMEM),
            pl.BlockSpec((LANES, tq), lambda h, i, k: (h, q_off + i)),
            pl.BlockSpec((tk, LANES), lambda h, i, k: (k_off + k, h)),
            pl.BlockSpec((tk, LANES), lambda h, i, k: (k_off + jnp.minimum(k + 1, nk - 1), h)),
            pl.BlockSpec((2, ATTN_VT_ROWS, tk), lambda h, i, k: (h, 0, k_off + k)),
            pl.BlockSpec((DIFF_V, 1), lambda h, i, k: (0, 0)),
        ],
        out_specs=pl.BlockSpec((LANES, tq), lambda h, i, k: (h, i)),
        scratch_shapes=[pltpu.VMEM((4, LANES, tq), BF16),
                        pltpu.VMEM((4, 1, tq), F32),
                        pltpu.VMEM((4, ATTN_VT_ROWS, tq), F32),
                        pltpu.VMEM((2, tk, tq), F32),
                        pltpu.VMEM((8, tq), F32)],
        compiler_params=_params(("arbitrary", "arbitrary", "arbitrary")),
        name="diff_attn",
    )(lam, dqt, dk, dk, vt_aug, subln.reshape(DIFF_V, 1))


def _merge_kernel(x_ref, mod_ref, g_ref, flat_ref, ftail_ref, of_ref, ob_ref, rg_ref, dlat_ref, dtail_ref,
                  wgt_ref, wbf_ref, wbr_ref, wbd_ref, wo_ref, bd_ref, o_ref, *, n_lat_tiles):
    x = x_ref[...]
    hb = _norm_mod(x, g_ref[...], mod_ref[0, 3:4, :], mod_ref[0, 4:5, :]).astype(BF16)
    gates = _sigmoid(_dot(hb, wgt_ref[...]))
    is_tail = pl.program_id(0) >= n_lat_tiles
    f = jnp.where(is_tail, ftail_ref[...], flat_ref[...])
    d = jnp.where(is_tail, dtail_ref[...], dlat_ref[...]).T
    r = of_ref[...] + ob_ref[...]
    rr_hi, rr_lo = _split_bf16(r * r)
    ms = _dot(rr_hi, bd_ref[...]) + _dot(rr_lo, bd_ref[...])
    rg = rg_ref[...]
    yr = r * lax.rsqrt(ms + EPS) * (rg * _sigmoid(rg))
    mixed = (gates[:, :D_MODEL] * _dot(f.astype(BF16), wbf_ref[...])
             + gates[:, D_MODEL:2 * D_MODEL] * _dot(yr.astype(BF16), wbr_ref[...])
             + gates[:, 2 * D_MODEL:] * _dot(d.astype(BF16), wbd_ref[...]))
    y = _dot(mixed.astype(BF16), wo_ref[...])
    o_ref[...] = x + mod_ref[0, 5:6, :] * y


def _merge_call(x, mods, g, f_lat, f_tail, o_f, o_b, rg, d_lat, d_tail, wgt, wbf, wbr, wbd, wo,
                n_lat_tiles, n_tiles):
    tm = TOKEN_TILE
    bd = np.kron(np.eye(RET_HEADS, dtype=np.float32), np.full((RET_V, RET_V), 1.0 / RET_V, np.float32))

    def lat_spec(w):
        return pl.BlockSpec((tm, w), lambda i: (jnp.minimum(i, n_lat_tiles - 1), 0))

    return pl.pallas_call(
        functools.partial(_merge_kernel, n_lat_tiles=n_lat_tiles),
        out_shape=jax.ShapeDtypeStruct((n_tiles * tm, D_MODEL), F32),
        grid=(n_tiles,),
        in_specs=[
            _row_spec(D_MODEL), _mod_spec(n_lat_tiles), _const_spec((1, D_MODEL)),
            lat_spec(F_W), _const_spec((tm, F_W)),
            _row_spec(RET_VW), _row_spec(RET_VW), _row_spec(RET_VW),
            pl.BlockSpec((DIFF_VW, tm), lambda i: (0, jnp.minimum(i, n_lat_tiles - 1))),
            _const_spec((DIFF_VW, tm)),
            _const_spec((D_MODEL, GATE_W)), _const_spec((F_W, D_MODEL)), _const_spec((RET_VW, D_MODEL)),
            _const_spec((DIFF_VW, D_MODEL)), _const_spec((D_MODEL, D_MODEL)), _const_spec((RET_VW, RET_VW)),
        ],
        out_specs=_row_spec(D_MODEL),
        compiler_params=_params(("arbitrary",)),
        name="merge",
    )(x, mods, g.reshape(1, D_MODEL), f_lat, f_tail, o_f, o_b, rg, d_lat, d_tail, wgt, wbf, wbr, wbd, wo,
      jnp.asarray(bd, BF16))


def _pick_tile(n, candidates):
    for c in candidates:
        if n % c == 0:
            return c
    raise ValueError(f"no tile for {n}")


def kernel(x, c, ctx, c_ctx, w_ada, b_ada, norm_g, ffn_w1, ffn_w3, ffn_w2, w_in, ret_decay_logit,
           diff_lambda, diff_subln, w_branch_f, w_branch_r, w_branch_d, w_out, final_g):
    batch, seq, d = x.shape
    ctx_len = ctx.shape[1]
    tm = TOKEN_TILE
    assert batch == 1 and d == D_MODEL
    assert seq % max(DFT_N1 * 8, tm) == 0 and ctx_len % 256 == 0 and ctx_len <= tm
    total = seq + ctx_len
    n_lat_tiles = seq // tm
    n_tiles = n_lat_tiles + 1
    n_rows = n_tiles * tm
    n_lat_chunks = seq // RET_CHUNK
    n_ctx_chunks = ctx_len // RET_CHUNK
    n_pad_chunks = (n_rows - total) // RET_CHUNK

    cc = jnp.zeros((8, D_MODEL), F32).at[0].set(c[0]).at[1].set(c_ctx)
    mods_all = _ada_call(cc, w_ada, b_ada)[:, :2].reshape(DEPTH, 2, N_MOD, D_MODEL)

    tables = _rope_tables(seq, n_rows)
    twc, tws = _twiddles(seq)
    log_g2_all = jax.nn.log_sigmoid(ret_decay_logit.astype(F32))
    lv = diff_lambda.astype(F32)
    w_aug_all, wgt_all = _prep_proj_weights(w_in)
    w1_all, w3_all, w2_all = ffn_w1.astype(BF16), ffn_w3.astype(BF16), ffn_w2.astype(BF16)
    wbf_all, wbr_all = w_branch_f.astype(BF16), w_branch_r.astype(BF16)
    wbd_all, wo_all = w_branch_d.astype(BF16), w_out.astype(BF16)

    tq = _pick_tile(seq, (1024, 512, 256, 128))
    tk = _pick_tile(total, (1280, 640, 256, 128))
    tail_pad = ((0, tm - ctx_len), (0, 0))

    xs = jnp.concatenate([x[0], ctx[0], jnp.zeros((n_rows - total, D_MODEL), F32)], axis=0)
    for l in range(DEPTH):
        last = l == DEPTH - 1
        lam_init = 0.8 - 0.6 * math.exp(-0.3 * l)
        mods = mods_all[l]
        lam = (jnp.exp(jnp.sum(lv[l, 0] * lv[l, 1])) - jnp.exp(jnp.sum(lv[l, 2] * lv[l, 3]))
               + lam_init).reshape(1)

        xs = _ffn_call(xs, mods, norm_g[l, 0], w1_all[l, 0], w3_all[l, 0], w2_all[l, 0], 0,
                       n_lat_tiles, n_tiles)

        uf, rq, rkt, rv, rg, dqt, dk, vt_aug = _proj_call(xs, mods, norm_g[l, 1], w_aug_all[l], tables,
                                                           n_lat_tiles, n_tiles)

        f_lat = _fourier_latent(uf[:seq], twc, tws)
        o_f, o_b = _retention_call(rq, rkt, rv, log_g2_all[l], n_lat_chunks, n_ctx_chunks, n_pad_chunks)
        d_lat = _attn_call(dqt, dk, vt_aug, lam, diff_subln[l], lam_init, seq, tq, tk, total // tk)

        if last:
            f_tail = jnp.zeros((tm, F_W), F32)
            d_tail = jnp.zeros((DIFF_VW, tm), F32)
            n_out = n_lat_tiles
        else:
            f_tail = jnp.pad(_fourier_ctx(uf[seq:total]), tail_pad)
            d_ctx = _attn_call(dqt, dk, vt_aug, lam, diff_subln[l], lam_init, ctx_len, ctx_len, ctx_len, 1,
                               q_off=seq // ctx_len, k_off=seq // ctx_len)
            d_tail = jnp.pad(d_ctx, tail_pad[::-1])
            n_out = n_tiles

        xs = _merge_call(xs, mods, norm_g[l, 1], f_lat, f_tail, o_f, o_b, rg, d_lat, d_tail, wgt_all[l],
                         wbf_all[l], wbr_all[l], wbd_all[l], wo_all[l], n_lat_tiles, n_out)

        xs = _ffn_call(xs, mods, norm_g[l, 2], w1_all[l, 1], w3_all[l, 1], w2_all[l, 1], 6,
                       n_lat_tiles, n_out, final_g=final_g if last else None)

    return xs.reshape(1, seq, D_MODEL)
```

```python
import functools
import math

import numpy as np
import jax
import jax.numpy as jnp
from jax import lax
from jax.experimental import pallas as pl
from jax.experimental.pallas import tpu as pltpu

D_MODEL = 1024
DEPTH = 4
GRID_W = 64
D_FF = 2816
N_MOD = 9
FOURIER_GROUPS = 4
FOURIER_CH = 64
RET_HEADS = 6
RET_QK = 32
RET_V = 64
RET_CHUNK = 128
DIFF_HEADS = 6
DIFF_QK = 32
DIFF_V = 64
ROPE_BASE = 10000.0
EPS = 1e-6
F_W = FOURIER_GROUPS * FOURIER_CH
RET_QW = RET_HEADS * RET_QK
RET_VW = RET_HEADS * RET_V
DIFF_QW = DIFF_HEADS * 2 * DIFF_QK
DIFF_VW = DIFF_HEADS * DIFF_V
GATE_W = 3 * D_MODEL

LANES = 128
VMEM_LIMIT_BYTES = 56 * 1024 * 1024

RET_QP = 2 * LANES
RET_STEP_CHUNKS = 2
TOKEN_TILE = 512
DFT_N1 = 128
ATTN_KEY_CHUNK = 256
ATTN_VT_ROWS = DIFF_V + 16

BF16 = jnp.bfloat16
F32 = jnp.float32
LOG2E = math.log2(math.e)


def _dot(a, b):
    return jnp.dot(a, b, preferred_element_type=F32)


def _split_bf16(x):
    hi = x.astype(BF16)
    lo = (x - hi.astype(F32)).astype(BF16)
    return hi, lo


def _dot3(a, b):
    ah, al = _split_bf16(a)
    bh, bl = _split_bf16(b)
    return _dot(ah, bh) + _dot(al, bh) + _dot(ah, bl)


def _norm_mod(x, g, shift, scale):
    ms = jnp.mean(x * x, axis=-1, keepdims=True)
    y = x * lax.rsqrt(ms + EPS) * g
    return y * (1.0 + scale) + shift


def _sigmoid(x):
    return 1.0 / (1.0 + jnp.exp(-x))


def _const_spec(shape):
    nd = len(shape)
    return pl.BlockSpec(shape, lambda *_: (0,) * nd, pipeline_mode=pl.Buffered(1))


def _params(sem):
    return pltpu.CompilerParams(dimension_semantics=sem, vmem_limit_bytes=VMEM_LIMIT_BYTES)


def _row_spec(width):
    return pl.BlockSpec((TOKEN_TILE, width), lambda i: (i, 0))


def _ada_kernel(cc_ref, w_ref, b_ref, o_ref):
    cc = cc_ref[...]
    s = cc * _sigmoid(cc)
    o_ref[0] = _dot3(s, w_ref[0]) + b_ref[0]


def _ada_call(cc, w_ada, b_ada):
    depth, d, n = w_ada.shape
    tn = 1152
    return pl.pallas_call(
        _ada_kernel,
        out_shape=jax.ShapeDtypeStruct((depth, 8, n), F32),
        grid=(depth, n // tn),
        in_specs=[
            pl.BlockSpec((8, d), lambda l, j: (0, 0)),
            pl.BlockSpec((1, d, tn), lambda l, j: (l, 0, j)),
            pl.BlockSpec((1, 1, tn), lambda l, j: (l, 0, j)),
        ],
        out_specs=pl.BlockSpec((1, 8, tn), lambda l, j: (l, 0, j)),
        compiler_params=_params(("arbitrary", "arbitrary")),
        name="adaln",
    )(cc, w_ada, b_ada.reshape(depth, 1, n))


def _mod_spec(n_lat_tiles):
    return pl.BlockSpec((1, N_MOD, D_MODEL), lambda i: (jnp.where(i >= n_lat_tiles, 1, 0), 0, 0))


def _ffn_tile(x, mod_ref, g_ref, w1_ref, w3_ref, w2_ref, base):
    shift = mod_ref[0, base:base + 1, :]
    scale = mod_ref[0, base + 1:base + 2, :]
    gate = mod_ref[0, base + 2:base + 3, :]
    hb = _norm_mod(x, g_ref[...], shift, scale).astype(BF16)
    a = _dot(hb, w1_ref[...])
    b = _dot(hb, w3_ref[...])
    u = (a * _sigmoid(a) * b).astype(BF16)
    return x + (0.5 * gate) * _dot(u, w2_ref[...])


def _ffn_kernel(x_ref, mod_ref, g_ref, w1_ref, w3_ref, w2_ref, o_ref, *, base):
    o_ref[...] = _ffn_tile(x_ref[...], mod_ref, g_ref, w1_ref, w3_ref, w2_ref, base)


def _ffn_final_kernel(x_ref, mod_ref, g_ref, w1_ref, w3_ref, w2_ref, fg_ref, o_ref, *, base):
    y = _ffn_tile(x_ref[...], mod_ref, g_ref, w1_ref, w3_ref, w2_ref, base)
    ms = jnp.mean(y * y, axis=-1, keepdims=True)
    o_ref[...] = y * lax.rsqrt(ms + EPS) * fg_ref[...]


def _ffn_call(x, mods, g, w1, w3, w2, base, n_lat_tiles, n_tiles, final_g=None):
    in_specs = [
        _row_spec(D_MODEL), _mod_spec(n_lat_tiles), _const_spec((1, D_MODEL)),
        _const_spec((D_MODEL, D_FF)), _const_spec((D_MODEL, D_FF)), _const_spec((D_FF, D_MODEL)),
    ]
    args = [x, mods, g.reshape(1, D_MODEL), w1, w3, w2]
    body = _ffn_kernel
    if final_g is not None:
        in_specs.append(_const_spec((1, D_MODEL)))
        args.append(final_g.reshape(1, D_MODEL))
        body = _ffn_final_kernel
    return pl.pallas_call(
        functools.partial(body, base=base),
        out_shape=jax.ShapeDtypeStruct((n_tiles * TOKEN_TILE, D_MODEL), F32),
        grid=(n_tiles,),
        in_specs=in_specs,
        out_specs=_row_spec(D_MODEL),
        compiler_params=_params(("arbitrary",)),
        name="ffn",
    )(*args)


_C_F, _C_RQ, _C_RK, _C_RV, _C_RG = 0, 256, 512, 768, 1152
_C_DQ, _C_DK, _C_DV = 1536, 1920, 2304
_C_RQR, _C_RKR, _C_DQR, _C_DKR = 2688, 2944, 3200, 3584
PROJ_W = 3968


def _rotate_half_cols(w, block):
    depth, d, width = w.shape
    w5 = w.reshape(depth, d, width // block, 2, block // 2)
    return jnp.concatenate([-w5[:, :, :, 1:2], w5[:, :, :, 0:1]], axis=3).reshape(depth, d, width)


def _prep_proj_weights(w_in):
    cuts = np.cumsum([F_W, RET_QW, RET_QW, RET_VW, RET_VW, DIFF_QW, DIFF_QW, DIFF_VW])
    wf, wrq, wrk, wrv, wrg, wdq, wdk, wdv, wgt = jnp.split(w_in, cuts, axis=2)
    z = jnp.zeros(w_in.shape[:2] + (RET_QP - RET_QW,), w_in.dtype)
    parts = [wf, wrq, z, wrk, z, wrv, wrg, wdq, wdk, wdv,
             _rotate_half_cols(wrq, RET_QK), z, _rotate_half_cols(wrk, RET_QK), z,
             _rotate_half_cols(wdq, DIFF_QK // 2), _rotate_half_cols(wdk, DIFF_QK // 2)]
    return jnp.concatenate(parts, axis=2).astype(BF16), wgt.astype(BF16)


def _rope_tables(seq, n_rows):
    pos = jnp.arange(seq, dtype=F32)
    inv_r = ROPE_BASE ** (-jnp.arange(0, RET_QK, 2, dtype=F32) / RET_QK)
    ang_r = pos[:, None] * inv_r[None, :]
    cos_r = jnp.tile(jnp.cos(ang_r), (1, 2 * RET_HEADS))
    sin_r = jnp.tile(jnp.sin(ang_r), (1, 2 * RET_HEADS))
    rows = jnp.repeat(jnp.arange(seq // GRID_W, dtype=F32), GRID_W)
    cols = jnp.tile(jnp.arange(GRID_W, dtype=F32), seq // GRID_W)
    dim = DIFF_QK // 2
    inv_d = ROPE_BASE ** (-jnp.arange(0, dim, 2, dtype=F32) / dim)
    a_row = rows[:, None] * inv_d[None, :]
    a_col = cols[:, None] * inv_d[None, :]
    cos_hm = jnp.concatenate([jnp.cos(a_row)] * 2 + [jnp.cos(a_col)] * 2, axis=1)
    sin_hm = jnp.concatenate([jnp.sin(a_row)] * 2 + [jnp.sin(a_col)] * 2, axis=1)
    cos_d = jnp.tile(cos_hm, (1, 2 * DIFF_HEADS))
    sin_d = jnp.tile(sin_hm, (1, 2 * DIFF_HEADS))

    def finish(t, width, fill):
        t = jnp.pad(t, ((0, 0), (0, width - t.shape[1])), constant_values=fill)
        return jnp.pad(t, ((0, n_rows - seq), (0, 0)), constant_values=fill)

    return (finish(cos_r, RET_QP, 1.0), finish(sin_r, RET_QP, 0.0),
            finish(cos_d, DIFF_QW, 1.0), finish(sin_d, DIFF_QW, 0.0))


def _proj_kernel(x_ref, mod_ref, g_ref, w_ref, cr_ref, sr_ref, cd_ref, sd_ref,
                 uf_ref, rq_ref, rkt_ref, rv_ref, rg_ref, dqt_ref, dk_ref, vta_ref):
    x = x_ref[...]
    hb = _norm_mod(x, g_ref[...], mod_ref[0, 3:4, :], mod_ref[0, 4:5, :]).astype(BF16)
    p = _dot(hb, w_ref[...])
    cr, sr, cd, sd = cr_ref[...], sr_ref[...], cd_ref[...], sd_ref[...]
    uf_ref[...] = p[:, _C_F:_C_F + F_W]
    rq = p[:, _C_RQ:_C_RQ + RET_QP] * cr + p[:, _C_RQR:_C_RQR + RET_QP] * sr
    rk = p[:, _C_RK:_C_RK + RET_QP] * cr + p[:, _C_RKR:_C_RKR + RET_QP] * sr
    rq_ref[...] = rq.astype(BF16)
    rkt_ref[...] = (rk * (RET_QK ** -0.5)).T.astype(BF16)
    rv_ref[...] = p[:, _C_RV:_C_RV + RET_VW].astype(BF16)
    rg_ref[...] = p[:, _C_RG:_C_RG + RET_VW]
    dq = p[:, _C_DQ:_C_DQ + DIFF_QW] * cd + p[:, _C_DQR:_C_DQR + DIFF_QW] * sd
    dk = p[:, _C_DK:_C_DK + DIFF_QW] * cd + p[:, _C_DKR:_C_DKR + DIFF_QW] * sd
    dqt_ref[...] = (dq * ((DIFF_QK ** -0.5) * LOG2E)).T.astype(BF16)
    dk_ref[...] = dk.astype(BF16)
    vt = p[:, _C_DV:_C_DV + DIFF_VW].T
    ones = jnp.ones((ATTN_VT_ROWS - DIFF_V, x.shape[0]), F32)
    for h in range(DIFF_HEADS):
        vta_ref[h] = jnp.concatenate([vt[h * DIFF_V:(h + 1) * DIFF_V], ones], axis=0).astype(BF16)


def _proj_call(x, mods, g, w_aug, tables, n_lat_tiles, n_tiles):
    tm = TOKEN_TILE
    t = n_tiles * tm
    out_shape = (
        jax.ShapeDtypeStruct((t, F_W), F32),
        jax.ShapeDtypeStruct((t, RET_QP), BF16),
        jax.ShapeDtypeStruct((RET_QP, t), BF16),
        jax.ShapeDtypeStruct((t, RET_VW), BF16),
        jax.ShapeDtypeStruct((t, RET_VW), F32),
        jax.ShapeDtypeStruct((DIFF_QW, t), BF16),
        jax.ShapeDtypeStruct((t, DIFF_QW), BF16),
        jax.ShapeDtypeStruct((DIFF_HEADS, ATTN_VT_ROWS, t), BF16),
    )
    out_specs = (
        _row_spec(F_W), _row_spec(RET_QP),
        pl.BlockSpec((RET_QP, tm), lambda i: (0, i)),
        _row_spec(RET_VW), _row_spec(RET_VW),
        pl.BlockSpec((DIFF_QW, tm), lambda i: (0, i)),
        _row_spec(DIFF_QW),
        pl.BlockSpec((DIFF_HEADS, ATTN_VT_ROWS, tm), lambda i: (0, 0, i)),
    )
    return pl.pallas_call(
        _proj_kernel,
        out_shape=out_shape,
        grid=(n_tiles,),
        in_specs=[
            _row_spec(D_MODEL), _mod_spec(n_lat_tiles), _const_spec((1, D_MODEL)),
            _const_spec((D_MODEL, PROJ_W)),
            _row_spec(RET_QP), _row_spec(RET_QP), _row_spec(DIFF_QW), _row_spec(DIFF_QW),
        ],
        out_specs=out_specs,
        compiler_params=_params(("arbitrary",)),
        name="mixer_proj",
    )(x, mods, g.reshape(1, D_MODEL), w_aug, *tables)


def _dft_mats(n):
    k = np.arange(n)
    ang = 2.0 * np.pi * ((k[:, None] * k[None, :]) % n) / n
    return np.cos(ang).astype(np.float32), np.sin(ang).astype(np.float32)


def _channel_dft_mats():
    c, s = _dft_mats(FOURIER_CH)
    eye = np.eye(FOURIER_GROUPS, dtype=np.float32)
    return np.kron(eye, c), np.kron(eye, s)


def _fourier_stage1_kernel(x_ref, cc_ref, sc_ref, c1_ref, s1_ref, twc_ref, tws_ref, tr_ref, ti_ref, *, nb):
    cc, sc = cc_ref[...], sc_ref[...]
    zr, zi = [], []
    for j in range(nb):
        u = x_ref[:, j * F_W:(j + 1) * F_W]
        zr.append(_dot3(u, cc))
        zi.append(-_dot3(u, sc))
    zr = jnp.concatenate(zr, axis=1) if nb > 1 else zr[0]
    zi = jnp.concatenate(zi, axis=1) if nb > 1 else zi[0]
    c1, s1 = c1_ref[...], s1_ref[...]
    tr = _dot3(c1, zr) + _dot3(s1, zi)
    ti = _dot3(c1, zi) - _dot3(s1, zr)
    twc, tws = twc_ref[...], tws_ref[...]
    tr_ref[...] = tr * twc + ti * tws
    ti_ref[...] = ti * twc - tr * tws


def _fourier_stage2_kernel(tr_ref, ti_ref, c2_ref, s2_ref, o_ref, *, kb):
    c2, s2 = c2_ref[...], s2_ref[...]
    for j in range(kb):
        o_ref[:, j, :] = _dot3(c2, tr_ref[j]) + _dot3(s2, ti_ref[j])


def _fourier_latent(u_all, seq, twc, tws):
    n1, n2 = DFT_N1, seq // DFT_N1
    nb = min(8, n2)
    kb = 8
    cc, sc = _channel_dft_mats()
    c1, s1 = _dft_mats(n1)
    c2, s2 = _dft_mats(n2)
    x2 = u_all[:seq].reshape(n1, n2 * F_W)
    blk = pl.BlockSpec((n1, nb * F_W), lambda i: (0, i))
    tr, ti = pl.pallas_call(
        functools.partial(_fourier_stage1_kernel, nb=nb),
        out_shape=(jax.ShapeDtypeStruct((n1, n2 * F_W), F32),) * 2,
        grid=(n2 // nb,),
        in_specs=[blk, _const_spec((F_W, F_W)), _const_spec((F_W, F_W)),
                  _const_spec((n1, n1)), _const_spec((n1, n1)), blk, blk],
        out_specs=(blk, blk),
        compiler_params=_params(("arbitrary",)),
        name="fourier_stage1",
    )(x2, cc, sc, c1, s1, twc, tws)
    tr3 = tr.reshape(n1, n2, F_W)
    ti3 = ti.reshape(n1, n2, F_W)
    o3 = pl.pallas_call(
        functools.partial(_fourier_stage2_kernel, kb=kb),
        out_shape=jax.ShapeDtypeStruct((n2, n1, F_W), F32),
        grid=(n1 // kb,),
        in_specs=[pl.BlockSpec((kb, n2, F_W), lambda i: (i, 0, 0))] * 2
        + [_const_spec((n2, n2)), _const_spec((n2, n2))],
        out_specs=pl.BlockSpec((n2, kb, F_W), lambda i: (0, i, 0)),
        compiler_params=_params(("arbitrary",)),
        name="fourier_stage2",
    )(tr3, ti3, c2, s2)
    return o3.reshape(seq, F_W)


def _twiddles(seq):
    n1, n2 = DFT_N1, seq // DFT_N1
    k1 = jnp.arange(n1, dtype=jnp.int32)[:, None]
    m2 = jnp.arange(n2, dtype=jnp.int32)[None, :]
    ang = (2.0 * math.pi / seq) * ((k1 * m2) % seq).astype(F32)
    scale = 1.0 / math.sqrt(seq * FOURIER_CH)
    twc = jnp.repeat(jnp.cos(ang) * scale, F_W, axis=1)
    tws = jnp.repeat(jnp.sin(ang) * scale, F_W, axis=1)
    return twc, tws


def _fourier_ctx_kernel(u_ref, cc_ref, sc_ref, cl_ref, sl_ref, o_ref, *, scale):
    u = u_ref[...]
    a = _dot3(u, cc_ref[...])
    b = _dot3(u, sc_ref[...])
    o_ref[...] = (_dot3(cl_ref[...], a) - _dot3(sl_ref[...], b)) * scale


def _fourier_ctx(u):
    n = u.shape[0]
    cc, sc = _channel_dft_mats()
    cl, sl = _dft_mats(n)
    return pl.pallas_call(
        functools.partial(_fourier_ctx_kernel, scale=1.0 / math.sqrt(n * FOURIER_CH)),
        out_shape=jax.ShapeDtypeStruct((n, F_W), F32),
        name="fourier_ctx",
    )(u, cc, sc, cl, sl)


def _ret_chunk_local(q, kt, v, d_ref, kdec, bdmask, hmask_ref, vmask_ref):
    pieces = []
    vparts = []
    for h in range(RET_HEADS):
        qh = q * hmask_ref[h]
        pieces.append((_dot(qh, kt) * d_ref[h]).astype(BF16))
        vparts.append(v * vmask_ref[h])
    inner = jnp.concatenate(pieces, axis=1)
    vbd = jnp.concatenate(vparts, axis=0)
    kd = (kt.astype(F32) * kdec).astype(BF16)
    return _dot(inner, vbd), bdmask * _dot(kd, v)


def _ret_dir(q_ref, kt_ref, v_ref, o_ref, s_ref, d_ref, qdec, kdec, cd, bdmask, hmask_ref, vmask_ref, order):
    c = RET_CHUNK
    local = []
    for half in order:
        rows = slice(half * c, (half + 1) * c)
        q = q_ref[rows, :]
        local.append((rows, q) + _ret_chunk_local(q, kt_ref[:, rows], v_ref[rows, :], d_ref, kdec, bdmask,
                                                  hmask_ref, vmask_ref))
    s = s_ref[...]
    for rows, q, o_intra, inc in local:
        o_ref[rows, :] = o_intra + _dot(q, s.astype(BF16)) * qdec
        s = s * cd + inc
    s_ref[...] = s


def _ret_kernel(logg_ref, lgv_ref, lgk_ref, bdmask_ref, hmask_ref, vmask_ref,
                qf_ref, ktf_ref, vf_ref, qb_ref, ktb_ref, vb_ref,
                of_ref, ob_ref,
                sf_ref, sb_ref, df_ref, db_ref, qdf_ref, qdb_ref, kdf_ref, kdb_ref, cdf_ref, cdb_ref):
    c = RET_CHUNK

    @pl.when(pl.program_id(0) == 0)
    def _():
        sf_ref[...] = jnp.zeros_like(sf_ref)
        sb_ref[...] = jnp.zeros_like(sb_ref)
        ii = lax.broadcasted_iota(jnp.int32, (c, c), 0).astype(F32)
        jj = lax.broadcasted_iota(jnp.int32, (c, c), 1).astype(F32)
        for h in range(RET_HEADS):
            df_ref[h] = jnp.where(ii >= jj, jnp.exp(logg_ref[0, h] * jnp.maximum(ii - jj, 0.0)), 0.0)
            db_ref[h] = jnp.where(jj >= ii, jnp.exp(logg_ref[1, h] * jnp.maximum(jj - ii, 0.0)), 0.0)
        ri = lax.broadcasted_iota(jnp.int32, (c, RET_VW), 0).astype(F32)
        qdf_ref[...] = jnp.exp(lgv_ref[0] * (ri + 1.0))
        qdb_ref[...] = jnp.exp(lgv_ref[1] * (c - ri))
        cj = lax.broadcasted_iota(jnp.int32, (RET_QP, c), 1).astype(F32)
        kdf_ref[...] = jnp.exp(lgk_ref[0] * (c - 1.0 - cj))
        kdb_ref[...] = jnp.exp(lgk_ref[1] * cj)
        cdf_ref[...] = jnp.exp(lgv_ref[0] * float(c))
        cdb_ref[...] = jnp.exp(lgv_ref[1] * float(c))

    bdmask = bdmask_ref[...]
    halves = list(range(RET_STEP_CHUNKS))
    _ret_dir(qf_ref, ktf_ref, vf_ref, of_ref, sf_ref, df_ref,
             qdf_ref[...], kdf_ref[...], cdf_ref[...], bdmask, hmask_ref, vmask_ref, halves)
    _ret_dir(qb_ref, ktb_ref, vb_ref, ob_ref, sb_ref, db_ref,
             qdb_ref[...], kdb_ref[...], cdb_ref[...], bdmask, hmask_ref, vmask_ref, halves[::-1])


def _ret_masks():
    hm = np.zeros((RET_HEADS, 1, RET_QP), np.float32)
    vm = np.zeros((RET_HEADS, 1, RET_VW), np.float32)
    bd = np.zeros((RET_QP, RET_VW), np.float32)
    for h in range(RET_HEADS):
        hm[h, 0, h * RET_QK:(h + 1) * RET_QK] = 1.0
        vm[h, 0, h * RET_V:(h + 1) * RET_V] = 1.0
        bd[h * RET_QK:(h + 1) * RET_QK, h * RET_V:(h + 1) * RET_V] = 1.0
    return jnp.asarray(bd), jnp.asarray(hm, BF16), jnp.asarray(vm, BF16)


def _retention_call(rq, rkt, rv, log_g2, n_lat, n_ctx, n_pad):
    c = RET_CHUNK
    g = RET_STEP_CHUNKS
    assert n_lat % g == 0 and n_ctx % g == 0 and n_pad % g == 0
    n_lat, n_ctx, n_pad = n_lat // g, n_ctx // g, n_pad // g
    n_real = n_lat + n_ctx
    n = n_real + n_pad
    rows = g * c

    def fwd(i):
        return jnp.where(i < n_ctx, n_lat + i, jnp.where(i < n_real, i - n_ctx, i))

    def bwd(i):
        return jnp.where(i < n_real, n_real - 1 - i, i)

    lgv = jnp.repeat(log_g2, RET_V, axis=1).reshape(2, 1, RET_VW)
    lgk = jnp.pad(jnp.repeat(log_g2, RET_QK, axis=1), ((0, 0), (0, RET_QP - RET_QW)))
    lgk = jnp.broadcast_to(lgk[:, :, None], (2, RET_QP, c))
    bd, hm, vm = _ret_masks()

    def specs(ix):
        return [pl.BlockSpec((rows, RET_QP), lambda i: (ix(i), 0)),
                pl.BlockSpec((RET_QP, rows), lambda i: (0, ix(i))),
                pl.BlockSpec((rows, RET_VW), lambda i: (ix(i), 0))]

    vmem = pltpu.VMEM
    return pl.pallas_call(
        _ret_kernel,
        out_shape=(jax.ShapeDtypeStruct((n * rows, RET_VW), F32),) * 2,
        grid=(n,),
        in_specs=[pl.BlockSpec(memory_space=pltpu.SMEM),
                  _const_spec((2, 1, RET_VW)), _const_spec((2, RET_QP, c)),
                  _const_spec((RET_QP, RET_VW)), _const_spec((RET_HEADS, 1, RET_QP)),
                  _const_spec((RET_HEADS, 1, RET_VW))] + specs(fwd) + specs(bwd),
        out_specs=(pl.BlockSpec((rows, RET_VW), lambda i: (fwd(i), 0)),
                   pl.BlockSpec((rows, RET_VW), lambda i: (bwd(i), 0))),
        scratch_shapes=[vmem((RET_QP, RET_VW), F32), vmem((RET_QP, RET_VW), F32),
                        vmem((RET_HEADS, c, c), F32), vmem((RET_HEADS, c, c), F32),
                        vmem((c, RET_VW), F32), vmem((c, RET_VW), F32),
                        vmem((RET_QP, c), F32), vmem((RET_QP, c), F32),
                        vmem((1, RET_VW), F32), vmem((1, RET_VW), F32)],
        compiler_params=_params(("arbitrary",)),
        name="retention",
    )(log_g2, lgv, lgk, bd, hm, vm, rq, rkt, rv, rq, rkt, rv)


def _chunk_scores(k_chunk, qm, s_out_ref, rows, run_max):
    s = _dot(k_chunk, qm)
    s_out_ref[rows, :] = s
    cm = jnp.max(s.reshape(s.shape[0] // 8, 8, s.shape[1]), axis=0)
    return cm if run_max is None else jnp.maximum(run_max, cm)


def _attn_kernel(lam_ref, qt_ref, k_ref, kn_ref, vt_ref, subln_ref, o_ref,
                 qm_ref, m_ref, acc_ref, s_ref, mx_ref, *, out_scale, key_chunk):
    ki = pl.program_id(2)
    tk = k_ref.shape[0]
    n_chunks = tk // key_chunk

    def chunk(c):
        return slice(c * key_chunk, (c + 1) * key_chunk)

    @pl.when(ki == 0)
    def _():
        qt = qt_ref[...]
        row = lax.broadcasted_iota(jnp.int32, qt.shape, 0)
        for j in range(4):
            lo = j * DIFF_QK
            qm_ref[j] = jnp.where((row >= lo) & (row < lo + DIFF_QK), qt, jnp.zeros_like(qt))
        m_ref[...] = jnp.full(m_ref.shape, -jnp.inf, F32)
        acc_ref[...] = jnp.zeros_like(acc_ref)
        run = None
        for c in range(n_chunks):
            run = _chunk_scores(k_ref[chunk(c), :], qm_ref[0], s_ref.at[0], chunk(c), run)
        mx_ref[...] = run

    run = mx_ref[...]
    for u in range(4):
        cur, nxt = u % 2, (u + 1) % 2
        m_prev = m_ref[u]
        m_new = jnp.maximum(m_prev, jnp.max(run, axis=0, keepdims=True))
        alpha = jnp.exp2(m_prev - m_new)
        vt = vt_ref[u // 2]
        run = None
        acc = None
        p_prev = None
        for c in range(n_chunks):
            k_chunk = k_ref[chunk(c), :] if u < 3 else kn_ref[chunk(c), :]
            run = _chunk_scores(k_chunk, qm_ref[(u + 1) % 4], s_ref.at[nxt], chunk(c), run)
            p = jnp.exp2(s_ref[cur, chunk(c), :] - m_new).astype(BF16)
            if p_prev is not None:
                pv = _dot(vt[:, chunk(c - 1)], p_prev)
                acc = pv if acc is None else acc + pv
            p_prev = p
        pv = _dot(vt[:, chunk(n_chunks - 1)], p_prev)
        acc = pv if acc is None else acc + pv
        acc_ref[u] = alpha * acc_ref[u] + acc
        m_ref[u] = m_new
    mx_ref[...] = run

    @pl.when(ki == pl.num_programs(2) - 1)
    def _():
        lam = lam_ref[0]
        outs = []
        for hl in range(2):
            a1 = acc_ref[2 * hl]
            a2 = acc_ref[2 * hl + 1]
            o = (a1[:DIFF_V] / a1[DIFF_V:DIFF_V + 1]
                 - lam * (a2[:DIFF_V] / a2[DIFF_V:DIFF_V + 1]))
            ms = jnp.mean(o * o, axis=0, keepdims=True)
            outs.append(o * lax.rsqrt(ms + EPS) * subln_ref[...] * out_scale)
        o_ref[...] = jnp.concatenate(outs, axis=0)


def _attn_call(dqt, dk, vt_aug, lam, subln, lam_init, n_q, tq, tk, nk, q_off=0, k_off=0):
    return pl.pallas_call(
        functools.partial(_attn_kernel, out_scale=1.0 - lam_init, key_chunk=min(tk, ATTN_KEY_CHUNK)),
        out_shape=jax.ShapeDtypeStruct((DIFF_VW, n_q), F32),
        grid=(DIFF_HEADS // 2, n_q // tq, nk),
        in_specs=[
            pl.BlockSpec(memory_space=pltpu.SMEM),
            pl.BlockSpec((LANES, tq), lambda h, i, k: (h, q_off + i)),
            pl.BlockSpec((tk, LANES), lambda h, i, k: (k_off + k, h)),
            pl.BlockSpec((tk, LANES), lambda h, i, k: (k_off + jnp.minimum(k + 1, nk - 1), h)),
            pl.BlockSpec((2, ATTN_VT_ROWS, tk), lambda h, i, k: (h, 0, k_off + k)),
            pl.BlockSpec((DIFF_V, 1), lambda h, i, k: (0, 0)),
        ],
        out_specs=pl.BlockSpec((LANES, tq), lambda h, i, k: (h, i)),
        scratch_shapes=[pltpu.VMEM((4, LANES, tq), BF16),
                        pltpu.VMEM((4, 1, tq), F32),
                        pltpu.VMEM((4, ATTN_VT_ROWS, tq), F32),
                        pltpu.VMEM((2, tk, tq), F32),
                        pltpu.VMEM((8, tq), F32)],
        compiler_params=_params(("arbitrary", "arbitrary", "arbitrary")),
        name="diff_attn",
    )(lam, dqt, dk, dk, vt_aug, subln.reshape(DIFF_V, 1))


def _merge_kernel(x_ref, mod_ref, g_ref, flat_ref, ftail_ref, of_ref, ob_ref, rg_ref, dlat_ref, dtail_ref,
                  wgt_ref, wbf_ref, wbr_ref, wbd_ref, wo_ref, bd_ref, o_ref, *, n_lat_tiles):
    x = x_ref[...]
    hb = _norm_mod(x, g_ref[...], mod_ref[0, 3:4, :], mod_ref[0, 4:5, :]).astype(BF16)
    gates = _sigmoid(_dot(hb, wgt_ref[...]))
    is_tail = pl.program_id(0) >= n_lat_tiles
    f = jnp.where(is_tail, ftail_ref[...], flat_ref[...])
    d = jnp.where(is_tail, dtail_ref[...], dlat_ref[...]).T
    r = of_ref[...] + ob_ref[...]
    rr_hi, rr_lo = _split_bf16(r * r)
    ms = _dot(rr_hi, bd_ref[...]) + _dot(rr_lo, bd_ref[...])
    rg = rg_ref[...]
    yr = r * lax.rsqrt(ms + EPS) * (rg * _sigmoid(rg))
    mixed = (gates[:, :D_MODEL] * _dot(f.astype(BF16), wbf_ref[...])
             + gates[:, D_MODEL:2 * D_MODEL] * _dot(yr.astype(BF16), wbr_ref[...])
             + gates[:, 2 * D_MODEL:] * _dot(d.astype(BF16), wbd_ref[...]))
    y = _dot(mixed.astype(BF16), wo_ref[...])
    o_ref[...] = x + mod_ref[0, 5:6, :] * y


def _merge_call(x, mods, g, f_lat, f_tail, o_f, o_b, rg, d_lat, d_tail, wgt, wbf, wbr, wbd, wo,
                n_lat_tiles, n_tiles):
    tm = TOKEN_TILE
    bd = np.kron(np.eye(RET_HEADS, dtype=np.float32), np.full((RET_V, RET_V), 1.0 / RET_V, np.float32))

    def lat_spec(w):
        return pl.BlockSpec((tm, w), lambda i: (jnp.minimum(i, n_lat_tiles - 1), 0))

    return pl.pallas_call(
        functools.partial(_merge_kernel, n_lat_tiles=n_lat_tiles),
        out_shape=jax.ShapeDtypeStruct((n_tiles * tm, D_MODEL), F32),
        grid=(n_tiles,),
        in_specs=[
            _row_spec(D_MODEL), _mod_spec(n_lat_tiles), _const_spec((1, D_MODEL)),
            lat_spec(F_W), _const_spec((tm, F_W)),
            _row_spec(RET_VW), _row_spec(RET_VW), _row_spec(RET_VW),
            pl.BlockSpec((DIFF_VW, tm), lambda i: (0, jnp.minimum(i, n_lat_tiles - 1))),
            _const_spec((DIFF_VW, tm)),
            _const_spec((D_MODEL, GATE_W)), _const_spec((F_W, D_MODEL)), _const_spec((RET_VW, D_MODEL)),
            _const_spec((DIFF_VW, D_MODEL)), _const_spec((D_MODEL, D_MODEL)), _const_spec((RET_VW, RET_VW)),
        ],
        out_specs=_row_spec(D_MODEL),
        compiler_params=_params(("arbitrary",)),
        name="merge",
    )(x, mods, g.reshape(1, D_MODEL), f_lat, f_tail, o_f, o_b, rg, d_lat, d_tail, wgt, wbf, wbr, wbd, wo,
      jnp.asarray(bd, BF16))


def _pick_tile(n, candidates):
    for c in candidates:
        if n % c == 0:
            return c
    raise ValueError(f"no tile for {n}")


def kernel(x, c, ctx, c_ctx, w_ada, b_ada, norm_g, ffn_w1, ffn_w3, ffn_w2, w_in, ret_decay_logit,
           diff_lambda, diff_subln, w_branch_f, w_branch_r, w_branch_d, w_out, final_g):
    batch, seq, d = x.shape
    ctx_len = ctx.shape[1]
    tm = TOKEN_TILE
    assert batch == 1 and d == D_MODEL
    assert seq % max(DFT_N1 * 8, tm) == 0 and ctx_len % 256 == 0 and ctx_len <= tm
    total = seq + ctx_len
    n_lat_tiles = seq // tm
    n_tiles = n_lat_tiles + 1
    n_rows = n_tiles * tm
    n_lat_chunks = seq // RET_CHUNK
    n_ctx_chunks = ctx_len // RET_CHUNK
    n_pad_chunks = (n_rows - total) // RET_CHUNK

    cc = jnp.zeros((8, D_MODEL), F32).at[0].set(c[0]).at[1].set(c_ctx)
    mods_all = _ada_call(cc, w_ada, b_ada)[:, :2].reshape(DEPTH, 2, N_MOD, D_MODEL)

    tables = _rope_tables(seq, n_rows)
    twc, tws = _twiddles(seq)
    log_g2_all = jax.nn.log_sigmoid(ret_decay_logit.astype(F32))
    lv = diff_lambda.astype(F32)
    w_aug_all, wgt_all = _prep_proj_weights(w_in)
    w1_all, w3_all, w2_all = ffn_w1.astype(BF16), ffn_w3.astype(BF16), ffn_w2.astype(BF16)
    wbf_all, wbr_all = w_branch_f.astype(BF16), w_branch_r.astype(BF16)
    wbd_all, wo_all = w_branch_d.astype(BF16), w_out.astype(BF16)

    tq = _pick_tile(seq, (512, 256, 128))
    tk = _pick_tile(total, (3328, 1280, 640, 256, 128))
    tail_pad = ((0, tm - ctx_len), (0, 0))

    xs = jnp.concatenate([x[0], ctx[0], jnp.zeros((n_rows - total, D_MODEL), F32)], axis=0)
    for l in range(DEPTH):
        last = l == DEPTH - 1
        lam_init = 0.8 - 0.6 * math.exp(-0.3 * l)
        mods = mods_all[l]
        lam = (jnp.exp(jnp.sum(lv[l, 0] * lv[l, 1])) - jnp.exp(jnp.sum(lv[l, 2] * lv[l, 3]))
               + lam_init).reshape(1)

        xs = _ffn_call(xs, mods, norm_g[l, 0], w1_all[l, 0], w3_all[l, 0], w2_all[l, 0], 0,
                       n_lat_tiles, n_tiles)

        uf, rq, rkt, rv, rg, dqt, dk, vt_aug = _proj_call(xs, mods, norm_g[l, 1], w_aug_all[l], tables,
                                                           n_lat_tiles, n_tiles)

        f_lat = _fourier_latent(uf, seq, twc, tws)
        o_f, o_b = _retention_call(rq, rkt, rv, log_g2_all[l], n_lat_chunks, n_ctx_chunks, n_pad_chunks)
        d_lat = _attn_call(dqt, dk, vt_aug, lam, diff_subln[l], lam_init, seq, tq, tk, total // tk)

        if last:
            f_tail = jnp.zeros((tm, F_W), F32)
            d_tail = jnp.zeros((DIFF_VW, tm), F32)
            n_out = n_lat_tiles
        else:
            f_tail = jnp.pad(_fourier_ctx(uf[seq:total]), tail_pad)
            d_ctx = _attn_call(dqt, dk, vt_aug, lam, diff_subln[l], lam_init, ctx_len, ctx_len, ctx_len, 1,
                               q_off=seq // ctx_len, k_off=seq // ctx_len)
            d_tail = jnp.pad(d_ctx, tail_pad[::-1])
            n_out = n_tiles

        xs = _merge_call(xs, mods, norm_g[l, 1], f_lat, f_tail, o_f, o_b, rg, d_lat, d_tail, wgt_all[l],
                         wbf_all[l], wbr_all[l], wbd_all[l], wo_all[l], n_lat_tiles, n_out)

        xs = _ffn_call(xs, mods, norm_g[l, 2], w1_all[l, 1], w3_all[l, 1], w2_all[l, 1], 6,
                       n_lat_tiles, n_out, final_g=final_g if last else None)

    return xs.reshape(1, seq, D_MODEL)
```

```python
import functools
import math

import numpy as np
import jax
import jax.numpy as jnp
from jax import lax
from jax.experimental import pallas as pl
from jax.experimental.pallas import tpu as pltpu

D_MODEL = 1024
DEPTH = 4
GRID_W = 64
D_FF = 2816
N_MOD = 9
FOURIER_GROUPS = 4
FOURIER_CH = 64
RET_HEADS = 6
RET_QK = 32
RET_V = 64
RET_CHUNK = 128
DIFF_HEADS = 6
DIFF_QK = 32
DIFF_V = 64
ROPE_BASE = 10000.0
EPS = 1e-6
F_W = FOURIER_GROUPS * FOURIER_CH
RET_QW = RET_HEADS * RET_QK
RET_VW = RET_HEADS * RET_V
DIFF_QW = DIFF_HEADS * 2 * DIFF_QK
DIFF_VW = DIFF_HEADS * DIFF_V
GATE_W = 3 * D_MODEL

LANES = 128
VMEM_LIMIT_BYTES = 56 * 1024 * 1024

RET_QP = 2 * LANES
RET_STEP_CHUNKS = 2
TOKEN_TILE = 512
DFT_N1 = 128
ATTN_KEY_CHUNK = 256
ATTN_VT_ROWS = DIFF_V + 16

BF16 = jnp.bfloat16
F32 = jnp.float32
LOG2E = math.log2(math.e)


def _dot(a, b):
    return jnp.dot(a, b, preferred_element_type=F32)


def _split_bf16(x):
    hi = x.astype(BF16)
    lo = (x - hi.astype(F32)).astype(BF16)
    return hi, lo


def _dot3(a, b):
    ah, al = _split_bf16(a)
    bh, bl = _split_bf16(b)
    return _dot(ah, bh) + _dot(al, bh) + _dot(ah, bl)


def _norm_mod(x, g, shift, scale):
    ms = jnp.mean(x * x, axis=-1, keepdims=True)
    y = x * lax.rsqrt(ms + EPS) * g
    return y * (1.0 + scale) + shift


def _sigmoid(x):
    return 1.0 / (1.0 + jnp.exp(-x))


def _const_spec(shape):
    nd = len(shape)
    return pl.BlockSpec(shape, lambda *_: (0,) * nd, pipeline_mode=pl.Buffered(1))


def _params(sem):
    return pltpu.CompilerParams(dimension_semantics=sem, vmem_limit_bytes=VMEM_LIMIT_BYTES)


def _row_spec(width):
    return pl.BlockSpec((TOKEN_TILE, width), lambda i: (i, 0))


def _ada_kernel(cc_ref, w_ref, b_ref, o_ref):
    cc = cc_ref[...]
    s = cc * _sigmoid(cc)
    o_ref[0] = _dot3(s, w_ref[0]) + b_ref[0]


def _ada_call(cc, w_ada, b_ada):
    depth, d, n = w_ada.shape
    tn = 1152
    return pl.pallas_call(
        _ada_kernel,
        out_shape=jax.ShapeDtypeStruct((depth, 8, n), F32),
        grid=(depth, n // tn),
        in_specs=[
            pl.BlockSpec((8, d), lambda l, j: (0, 0)),
            pl.BlockSpec((1, d, tn), lambda l, j: (l, 0, j)),
            pl.BlockSpec((1, 1, tn), lambda l, j: (l, 0, j)),
        ],
        out_specs=pl.BlockSpec((1, 8, tn), lambda l, j: (l, 0, j)),
        compiler_params=_params(("arbitrary", "arbitrary")),
        name="adaln",
    )(cc, w_ada, b_ada.reshape(depth, 1, n))


def _mod_spec(n_lat_tiles):
    return pl.BlockSpec((1, N_MOD, D_MODEL), lambda i: (jnp.where(i >= n_lat_tiles, 1, 0), 0, 0))


def _ffn_tile(x, mod_ref, g_ref, w1_ref, w3_ref, w2_ref, base):
    shift = mod_ref[0, base:base + 1, :]
    scale = mod_ref[0, base + 1:base + 2, :]
    gate = mod_ref[0, base + 2:base + 3, :]
    hb = _norm_mod(x, g_ref[...], shift, scale).astype(BF16)
    a = _dot(hb, w1_ref[...])
    b = _dot(hb, w3_ref[...])
    u = (a * _sigmoid(a) * b).astype(BF16)
    return x + (0.5 * gate) * _dot(u, w2_ref[...])


def _ffn_kernel(x_ref, mod_ref, g_ref, w1_ref, w3_ref, w2_ref, o_ref, *, base):
    o_ref[...] = _ffn_tile(x_ref[...], mod_ref, g_ref, w1_ref, w3_ref, w2_ref, base)


def _ffn_final_kernel(x_ref, mod_ref, g_ref, w1_ref, w3_ref, w2_ref, fg_ref, o_ref, *, base):
    y = _ffn_tile(x_ref[...], mod_ref, g_ref, w1_ref, w3_ref, w2_ref, base)
    ms = jnp.mean(y * y, axis=-1, keepdims=True)
    o_ref[...] = y * lax.rsqrt(ms + EPS) * fg_ref[...]


def _ffn_call(x, mods, g, w1, w3, w2, base, n_lat_tiles, n_tiles, final_g=None):
    in_specs = [
        _row_spec(D_MODEL), _mod_spec(n_lat_tiles), _const_spec((1, D_MODEL)),
        _const_spec((D_MODEL, D_FF)), _const_spec((D_MODEL, D_FF)), _const_spec((D_FF, D_MODEL)),
    ]
    args = [x, mods, g.reshape(1, D_MODEL), w1, w3, w2]
    body = _ffn_kernel
    if final_g is not None:
        in_specs.append(_const_spec((1, D_MODEL)))
        args.append(final_g.reshape(1, D_MODEL))
        body = _ffn_final_kernel
    return pl.pallas_call(
        functools.partial(body, base=base),
        out_shape=jax.ShapeDtypeStruct((n_tiles * TOKEN_TILE, D_MODEL), F32),
        grid=(n_tiles,),
        in_specs=in_specs,
        out_specs=_row_spec(D_MODEL),
        compiler_params=_params(("arbitrary",)),
        name="ffn",
    )(*args)


_C_F, _C_RQ, _C_RK, _C_RV, _C_RG = 0, 256, 512, 768, 1152
_C_DQ, _C_DK, _C_DV = 1536, 1920, 2304
_C_RQR, _C_RKR, _C_DQR, _C_DKR = 2688, 2944, 3200, 3584
PROJ_W = 3968


def _rotate_half_cols(w, block):
    depth, d, width = w.shape
    w5 = w.reshape(depth, d, width // block, 2, block // 2)
    return jnp.concatenate([-w5[:, :, :, 1:2], w5[:, :, :, 0:1]], axis=3).reshape(depth, d, width)


def _prep_proj_weights(w_in):
    cuts = np.cumsum([F_W, RET_QW, RET_QW, RET_VW, RET_VW, DIFF_QW, DIFF_QW, DIFF_VW])
    wf, wrq, wrk, wrv, wrg, wdq, wdk, wdv, wgt = jnp.split(w_in, cuts, axis=2)
    z = jnp.zeros(w_in.shape[:2] + (RET_QP - RET_QW,), w_in.dtype)
    parts = [wf, wrq, z, wrk, z, wrv, wrg, wdq, wdk, wdv,
             _rotate_half_cols(wrq, RET_QK), z, _rotate_half_cols(wrk, RET_QK), z,
             _rotate_half_cols(wdq, DIFF_QK // 2), _rotate_half_cols(wdk, DIFF_QK // 2)]
    return jnp.concatenate(parts, axis=2).astype(BF16), wgt.astype(BF16)


def _rope_tables(seq, n_rows):
    pos = jnp.arange(seq, dtype=F32)
    inv_r = ROPE_BASE ** (-jnp.arange(0, RET_QK, 2, dtype=F32) / RET_QK)
    ang_r = pos[:, None] * inv_r[None, :]
    cos_r = jnp.tile(jnp.cos(ang_r), (1, 2 * LANES // RET_QK))
    sin_r = jnp.tile(jnp.sin(ang_r), (1, 2 * LANES // RET_QK))
    rows = jnp.repeat(jnp.arange(seq // GRID_W, dtype=F32), GRID_W)
    cols = jnp.tile(jnp.arange(GRID_W, dtype=F32), seq // GRID_W)
    dim = DIFF_QK // 2
    inv_d = ROPE_BASE ** (-jnp.arange(0, dim, 2, dtype=F32) / dim)
    a_row = rows[:, None] * inv_d[None, :]
    a_col = cols[:, None] * inv_d[None, :]
    cos_hm = jnp.concatenate([jnp.cos(a_row)] * 2 + [jnp.cos(a_col)] * 2, axis=1)
    sin_hm = jnp.concatenate([jnp.sin(a_row)] * 2 + [jnp.sin(a_col)] * 2, axis=1)
    cos_d = jnp.tile(cos_hm, (1, LANES // DIFF_QK))
    sin_d = jnp.tile(sin_hm, (1, LANES // DIFF_QK))

    def finish(t, fill):
        return jnp.pad(t, ((0, n_rows - seq), (0, 0)), constant_values=fill)

    return finish(cos_r, 1.0), finish(sin_r, 0.0), finish(cos_d, 1.0), finish(sin_d, 0.0)


def _proj_kernel(x_ref, mod_ref, g_ref, w_ref, cr_ref, sr_ref, cd_ref, sd_ref,
                 uf_ref, rq_ref, rkt_ref, rv_ref, rg_ref, dqt_ref, dk_ref, vta_ref):
    x = x_ref[...]
    hb = _norm_mod(x, g_ref[...], mod_ref[0, 3:4, :], mod_ref[0, 4:5, :]).astype(BF16)
    p = _dot(hb, w_ref[...])
    cr = jnp.concatenate([cr_ref[...]] * (RET_QP // LANES), axis=1)
    sr = jnp.concatenate([sr_ref[...]] * (RET_QP // LANES), axis=1)
    cd = jnp.concatenate([cd_ref[...]] * (DIFF_QW // LANES), axis=1)
    sd = jnp.concatenate([sd_ref[...]] * (DIFF_QW // LANES), axis=1)
    uf_ref[...] = p[:, _C_F:_C_F + F_W]
    rq = p[:, _C_RQ:_C_RQ + RET_QP] * cr + p[:, _C_RQR:_C_RQR + RET_QP] * sr
    rk = p[:, _C_RK:_C_RK + RET_QP] * cr + p[:, _C_RKR:_C_RKR + RET_QP] * sr
    rq_ref[...] = rq.astype(BF16)
    rkt_ref[...] = (rk * (RET_QK ** -0.5)).T.astype(BF16)
    rv_ref[...] = p[:, _C_RV:_C_RV + RET_VW].astype(BF16)
    rg_ref[...] = p[:, _C_RG:_C_RG + RET_VW]
    dq = p[:, _C_DQ:_C_DQ + DIFF_QW] * cd + p[:, _C_DQR:_C_DQR + DIFF_QW] * sd
    dk = p[:, _C_DK:_C_DK + DIFF_QW] * cd + p[:, _C_DKR:_C_DKR + DIFF_QW] * sd
    dqt_ref[...] = (dq * ((DIFF_QK ** -0.5) * LOG2E)).T.astype(BF16)
    dk_ref[...] = dk.astype(BF16)
    vt = p[:, _C_DV:_C_DV + DIFF_VW].T
    ones = jnp.ones((ATTN_VT_ROWS - DIFF_V, x.shape[0]), F32)
    for h in range(DIFF_HEADS):
        vta_ref[h] = jnp.concatenate([vt[h * DIFF_V:(h + 1) * DIFF_V], ones], axis=0).astype(BF16)


def _proj_call(x, mods, g, w_aug, tables, n_lat_tiles, n_tiles):
    tm = TOKEN_TILE
    t = n_tiles * tm
    out_shape = (
        jax.ShapeDtypeStruct((t, F_W), F32),
        jax.ShapeDtypeStruct((t, RET_QP), BF16),
        jax.ShapeDtypeStruct((RET_QP, t), BF16),
        jax.ShapeDtypeStruct((t, RET_VW), BF16),
        jax.ShapeDtypeStruct((t, RET_VW), F32),
        jax.ShapeDtypeStruct((DIFF_QW, t), BF16),
        jax.ShapeDtypeStruct((t, DIFF_QW), BF16),
        jax.ShapeDtypeStruct((DIFF_HEADS, ATTN_VT_ROWS, t), BF16),
    )
    out_specs = (
        _row_spec(F_W), _row_spec(RET_QP),
        pl.BlockSpec((RET_QP, tm), lambda i: (0, i)),
        _row_spec(RET_VW), _row_spec(RET_VW),
        pl.BlockSpec((DIFF_QW, tm), lambda i: (0, i)),
        _row_spec(DIFF_QW),
        pl.BlockSpec((DIFF_HEADS, ATTN_VT_ROWS, tm), lambda i: (0, 0, i)),
    )
    return pl.pallas_call(
        _proj_kernel,
        out_shape=out_shape,
        grid=(n_tiles,),
        in_specs=[
            _row_spec(D_MODEL), _mod_spec(n_lat_tiles), _const_spec((1, D_MODEL)),
            _const_spec((D_MODEL, PROJ_W)),
            _row_spec(LANES), _row_spec(LANES), _row_spec(LANES), _row_spec(LANES),
        ],
        out_specs=out_specs,
        compiler_params=_params(("arbitrary",)),
        name="mixer_proj",
    )(x, mods, g.reshape(1, D_MODEL), w_aug, *tables)


def _dft_mats(n):
    k = np.arange(n)
    ang = 2.0 * np.pi * ((k[:, None] * k[None, :]) % n) / n
    return np.cos(ang).astype(np.float32), np.sin(ang).astype(np.float32)


def _channel_dft_mats():
    c, s = _dft_mats(FOURIER_CH)
    eye = np.eye(FOURIER_GROUPS, dtype=np.float32)
    return np.kron(eye, c), np.kron(eye, s)


def _fourier_stage1_kernel(x_ref, cc_ref, sc_ref, c1_ref, s1_ref, twc_ref, tws_ref, tr_ref, ti_ref, *, nb):
    cc, sc = cc_ref[...], sc_ref[...]
    zr, zi = [], []
    for j in range(nb):
        u = x_ref[:, j * F_W:(j + 1) * F_W]
        zr.append(_dot3(u, cc))
        zi.append(-_dot3(u, sc))
    zr = jnp.concatenate(zr, axis=1) if nb > 1 else zr[0]
    zi = jnp.concatenate(zi, axis=1) if nb > 1 else zi[0]
    c1, s1 = c1_ref[...], s1_ref[...]
    tr = _dot3(c1, zr) + _dot3(s1, zi)
    ti = _dot3(c1, zi) - _dot3(s1, zr)
    twc, tws = twc_ref[...], tws_ref[...]
    tr_ref[...] = tr * twc + ti * tws
    ti_ref[...] = ti * twc - tr * tws


def _fourier_stage2_kernel(tr_ref, ti_ref, c2_ref, s2_ref, o_ref, *, kb):
    c2, s2 = c2_ref[...], s2_ref[...]
    for j in range(kb):
        o_ref[:, j, :] = _dot3(c2, tr_ref[j]) + _dot3(s2, ti_ref[j])


def _fourier_latent(u_all, seq, twc, tws):
    n1, n2 = DFT_N1, seq // DFT_N1
    nb = min(8, n2)
    kb = 8
    cc, sc = _channel_dft_mats()
    c1, s1 = _dft_mats(n1)
    c2, s2 = _dft_mats(n2)
    x2 = u_all[:seq].reshape(n1, n2 * F_W)
    blk = pl.BlockSpec((n1, nb * F_W), lambda i: (0, i))
    tr, ti = pl.pallas_call(
        functools.partial(_fourier_stage1_kernel, nb=nb),
        out_shape=(jax.ShapeDtypeStruct((n1, n2 * F_W), F32),) * 2,
        grid=(n2 // nb,),
        in_specs=[blk, _const_spec((F_W, F_W)), _const_spec((F_W, F_W)),
                  _const_spec((n1, n1)), _const_spec((n1, n1)), blk, blk],
        out_specs=(blk, blk),
        compiler_params=_params(("arbitrary",)),
        name="fourier_stage1",
    )(x2, cc, sc, c1, s1, twc, tws)
    tr3 = tr.reshape(n1, n2, F_W)
    ti3 = ti.reshape(n1, n2, F_W)
    o3 = pl.pallas_call(
        functools.partial(_fourier_stage2_kernel, kb=kb),
        out_shape=jax.ShapeDtypeStruct((n2, n1, F_W), F32),
        grid=(n1 // kb,),
        in_specs=[pl.BlockSpec((kb, n2, F_W), lambda i: (i, 0, 0))] * 2
        + [_const_spec((n2, n2)), _const_spec((n2, n2))],
        out_specs=pl.BlockSpec((n2, kb, F_W), lambda i: (0, i, 0)),
        compiler_params=_params(("arbitrary",)),
        name="fourier_stage2",
    )(tr3, ti3, c2, s2)
    return o3.reshape(seq, F_W)


def _twiddles(seq):
    n1, n2 = DFT_N1, seq // DFT_N1
    k1 = jnp.arange(n1, dtype=jnp.int32)[:, None]
    m2 = jnp.arange(n2, dtype=jnp.int32)[None, :]
    ang = (2.0 * math.pi / seq) * ((k1 * m2) % seq).astype(F32)
    scale = 1.0 / math.sqrt(seq * FOURIER_CH)
    twc = jnp.repeat(jnp.cos(ang) * scale, F_W, axis=1)
    tws = jnp.repeat(jnp.sin(ang) * scale, F_W, axis=1)
    return twc, tws


def _fourier_ctx_kernel(u_ref, cc_ref, sc_ref, cl_ref, sl_ref, o_ref, *, scale):
    u = u_ref[...]
    a = _dot3(u, cc_ref[...])
    b = _dot3(u, sc_ref[...])
    o_ref[...] = (_dot3(cl_ref[...], a) - _dot3(sl_ref[...], b)) * scale


def _fourier_ctx(u):
    n = u.shape[0]
    cc, sc = _channel_dft_mats()
    cl, sl = _dft_mats(n)
    return pl.pallas_call(
        functools.partial(_fourier_ctx_kernel, scale=1.0 / math.sqrt(n * FOURIER_CH)),
        out_shape=jax.ShapeDtypeStruct((n, F_W), F32),
        name="fourier_ctx",
    )(u, cc, sc, cl, sl)


def _ret_chunk_local(q, kt, v, d_ref, kdec, bdmask, hmask_ref, vmask_ref):
    pieces = []
    vparts = []
    for h in range(RET_HEADS):
        qh = q * hmask_ref[h]
        pieces.append((_dot(qh, kt) * d_ref[h]).astype(BF16))
        vparts.append(v * vmask_ref[h])
    inner = jnp.concatenate(pieces, axis=1)
    vbd = jnp.concatenate(vparts, axis=0)
    kd = (kt.astype(F32) * kdec).astype(BF16)
    return _dot(inner, vbd), bdmask * _dot(kd, v)


def _ret_dir(q_ref, kt_ref, v_ref, o_ref, s_ref, d_ref, qdec, kdec, cd, bdmask, hmask_ref, vmask_ref, order):
    c = RET_CHUNK
    local = []
    for half in order:
        rows = slice(half * c, (half + 1) * c)
        q = q_ref[rows, :]
        local.append((rows, q) + _ret_chunk_local(q, kt_ref[:, rows], v_ref[rows, :], d_ref, kdec, bdmask,
                                                  hmask_ref, vmask_ref))
    s = s_ref[...]
    for rows, q, o_intra, inc in local:
        o_ref[rows, :] = o_intra + _dot(q, s.astype(BF16)) * qdec
        s = s * cd + inc
    s_ref[...] = s


def _ret_kernel(logg_ref, lgv_ref, lgk_ref, bdmask_ref, hmask_ref, vmask_ref,
                qf_ref, ktf_ref, vf_ref, qb_ref, ktb_ref, vb_ref,
                of_ref, ob_ref,
                sf_ref, sb_ref, df_ref, db_ref, qdf_ref, qdb_ref, kdf_ref, kdb_ref, cdf_ref, cdb_ref):
    c = RET_CHUNK

    @pl.when(pl.program_id(0) == 0)
    def _():
        sf_ref[...] = jnp.zeros_like(sf_ref)
        sb_ref[...] = jnp.zeros_like(sb_ref)
        ii = lax.broadcasted_iota(jnp.int32, (c, c), 0).astype(F32)
        jj = lax.broadcasted_iota(jnp.int32, (c, c), 1).astype(F32)
        for h in range(RET_HEADS):
            df_ref[h] = jnp.where(ii >= jj, jnp.exp(logg_ref[0, h] * jnp.maximum(ii - jj, 0.0)), 0.0)
            db_ref[h] = jnp.where(jj >= ii, jnp.exp(logg_ref[1, h] * jnp.maximum(jj - ii, 0.0)), 0.0)
        ri = lax.broadcasted_iota(jnp.int32, (c, RET_VW), 0).astype(F32)
        qdf_ref[...] = jnp.exp(lgv_ref[0] * (ri + 1.0))
        qdb_ref[...] = jnp.exp(lgv_ref[1] * (c - ri))
        cj = lax.broadcasted_iota(jnp.int32, (RET_QP, c), 1).astype(F32)
        kdf_ref[...] = jnp.exp(lgk_ref[0] * (c - 1.0 - cj))
        kdb_ref[...] = jnp.exp(lgk_ref[1] * cj)
        cdf_ref[...] = jnp.exp(lgv_ref[0] * float(c))
        cdb_ref[...] = jnp.exp(lgv_ref[1] * float(c))

    bdmask = bdmask_ref[...]
    halves = list(range(RET_STEP_CHUNKS))
    _ret_dir(qf_ref, ktf_ref, vf_ref, of_ref, sf_ref, df_ref,
             qdf_ref[...], kdf_ref[...], cdf_ref[...], bdmask, hmask_ref, vmask_ref, halves)
    _ret_dir(qb_ref, ktb_ref, vb_ref, ob_ref, sb_ref, db_ref,
             qdb_ref[...], kdb_ref[...], cdb_ref[...], bdmask, hmask_ref, vmask_ref, halves[::-1])


def _ret_masks():
    hm = np.zeros((RET_HEADS, 1, RET_QP), np.float32)
    vm = np.zeros((RET_HEADS, 1, RET_VW), np.float32)
    bd = np.zeros((RET_QP, RET_VW), np.float32)
    for h in range(RET_HEADS):
        hm[h, 0, h * RET_QK:(h + 1) * RET_QK] = 1.0
        vm[h, 0, h * RET_V:(h + 1) * RET_V] = 1.0
        bd[h * RET_QK:(h + 1) * RET_QK, h * RET_V:(h + 1) * RET_V] = 1.0
    return jnp.asarray(bd), jnp.asarray(hm, BF16), jnp.asarray(vm, BF16)


def _retention_call(rq, rkt, rv, log_g2, n_lat, n_ctx, n_pad):
    c = RET_CHUNK
    g = RET_STEP_CHUNKS
    assert n_lat % g == 0 and n_ctx % g == 0 and n_pad % g == 0
    n_lat, n_ctx, n_pad = n_lat // g, n_ctx // g, n_pad // g
    n_real = n_lat + n_ctx
    n = n_real + n_pad
    rows = g * c

    def fwd(i):
        return jnp.where(i < n_ctx, n_lat + i, jnp.where(i < n_real, i - n_ctx, i))

    def bwd(i):
        return jnp.where(i < n_real, n_real - 1 - i, i)

    lgv = jnp.repeat(log_g2, RET_V, axis=1).reshape(2, 1, RET_VW)
    lgk = jnp.pad(jnp.repeat(log_g2, RET_QK, axis=1), ((0, 0), (0, RET_QP - RET_QW)))
    lgk = jnp.broadcast_to(lgk[:, :, None], (2, RET_QP, c))
    bd, hm, vm = _ret_masks()

    def specs(ix):
        return [pl.BlockSpec((rows, RET_QP), lambda i: (ix(i), 0)),
                pl.BlockSpec((RET_QP, rows), lambda i: (0, ix(i))),
                pl.BlockSpec((rows, RET_VW), lambda i: (ix(i), 0))]

    vmem = pltpu.VMEM
    return pl.pallas_call(
        _ret_kernel,
        out_shape=(jax.ShapeDtypeStruct((n * rows, RET_VW), F32),) * 2,
        grid=(n,),
        in_specs=[pl.BlockSpec(memory_space=pltpu.SMEM),
                  _const_spec((2, 1, RET_VW)), _const_spec((2, RET_QP, c)),
                  _const_spec((RET_QP, RET_VW)), _const_spec((RET_HEADS, 1, RET_QP)),
                  _const_spec((RET_HEADS, 1, RET_VW))] + specs(fwd) + specs(bwd),
        out_specs=(pl.BlockSpec((rows, RET_VW), lambda i: (fwd(i), 0)),
                   pl.BlockSpec((rows, RET_VW), lambda i: (bwd(i), 0))),
        scratch_shapes=[vmem((RET_QP, RET_VW), F32), vmem((RET_QP, RET_VW), F32),
                        vmem((RET_HEADS, c, c), F32), vmem((RET_HEADS, c, c), F32),
                        vmem((c, RET_VW), F32), vmem((c, RET_VW), F32),
                        vmem((RET_QP, c), F32), vmem((RET_QP, c), F32),
                        vmem((1, RET_VW), F32), vmem((1, RET_VW), F32)],
        compiler_params=_params(("arbitrary",)),
        name="retention",
    )(log_g2, lgv, lgk, bd, hm, vm, rq, rkt, rv, rq, rkt, rv)


def _chunk_scores(k_chunk, qm, s_out_ref, rows, run_max):
    s = _dot(k_chunk, qm)
    s_out_ref[rows, :] = s
    cm = jnp.max(s.reshape(s.shape[0] // 8, 8, s.shape[1]), axis=0)
    return cm if run_max is None else jnp.maximum(run_max, cm)


def _mask_map(qt, j):
    row = lax.broadcasted_iota(jnp.int32, qt.shape, 0)
    lo = j * DIFF_QK
    return jnp.where((row >= lo) & (row < lo + DIFF_QK), qt, jnp.zeros_like(qt))


def _attn_kernel(lam_ref, qt_ref, qtn_ref, k0_ref, kn_ref, vt_ref, subln_ref, o_ref,
                 qm_ref, m_ref, acc_ref, s_ref, mx_ref, kcur_ref, *, out_scale, key_chunk):
    ki = pl.program_id(2)
    is_last = ki == pl.num_programs(2) - 1
    tk = kn_ref.shape[0]
    n_chunks = tk // key_chunk

    def chunk(c):
        return slice(c * key_chunk, (c + 1) * key_chunk)

    @pl.when(ki == 0)
    def _():
        qt = qt_ref[...]
        for j in range(4):
            qm_ref[j] = _mask_map(qt, j)
        m_ref[...] = jnp.full(m_ref.shape, -jnp.inf, F32)
        acc_ref[...] = jnp.zeros_like(acc_ref)

    @pl.when((ki == 0) & (pl.program_id(0) == 0) & (pl.program_id(1) == 0))
    def _():
        kcur_ref[...] = k0_ref[...]
        run = None
        for c in range(n_chunks):
            run = _chunk_scores(k0_ref[chunk(c), :], qm_ref[0], s_ref.at[0], chunk(c), run)
        mx_ref[...] = run

    q_ahead = jnp.where(is_last, _mask_map(qtn_ref[...], 0), qm_ref[0])

    run = mx_ref[...]
    for u in range(4):
        cur, nxt = u % 2, (u + 1) % 2
        m_prev = m_ref[u]
        m_new = jnp.maximum(m_prev, jnp.max(run, axis=0, keepdims=True))
        alpha = jnp.exp2(m_prev - m_new)
        vt = vt_ref[u // 2]
        run = None
        acc = None
        p_prev = None
        for c in range(n_chunks):
            if u < 3:
                run = _chunk_scores(kcur_ref[chunk(c), :], qm_ref[u + 1], s_ref.at[nxt], chunk(c), run)
            else:
                run = _chunk_scores(kn_ref[chunk(c), :], q_ahead, s_ref.at[nxt], chunk(c), run)
            p = jnp.exp2(s_ref[cur, chunk(c), :] - m_new).astype(BF16)
            if p_prev is not None:
                pv = _dot(vt[:, chunk(c - 1)], p_prev)
                acc = pv if acc is None else acc + pv
            p_prev = p
        pv = _dot(vt[:, chunk(n_chunks - 1)], p_prev)
        acc = pv if acc is None else acc + pv
        acc_ref[u] = alpha * acc_ref[u] + acc
        m_ref[u] = m_new
    mx_ref[...] = run
    kcur_ref[...] = kn_ref[...]

    @pl.when(is_last)
    def _():
        lam = lam_ref[0]
        outs = []
        for hl in range(2):
            a1 = acc_ref[2 * hl]
            a2 = acc_ref[2 * hl + 1]
            o = (a1[:DIFF_V] / a1[DIFF_V:DIFF_V + 1]
                 - lam * (a2[:DIFF_V] / a2[DIFF_V:DIFF_V + 1]))
            ms = jnp.mean(o * o, axis=0, keepdims=True)
            outs.append(o * lax.rsqrt(ms + EPS) * subln_ref[...] * out_scale)
        o_ref[...] = jnp.concatenate(outs, axis=0)


def _attn_call(dqt, dk, vt_aug, lam, subln, lam_init, n_q, tq, tk, nk, q_off=0, k_off=0):
    n_hp = DIFF_HEADS // 2
    n_i = n_q // tq

    def next_pair(h, i):
        wrap = i + 1 >= n_i
        return jnp.where(wrap, jnp.minimum(h + 1, n_hp - 1), h), jnp.where(wrap, 0, i + 1)

    def q_next_map(h, i, k):
        hn, i_n = next_pair(h, i)
        return hn, q_off + i_n

    def k_next_map(h, i, k):
        last = k + 1 >= nk
        return k_off + jnp.where(last, 0, k + 1), jnp.where(last, next_pair(h, i)[0], h)

    return pl.pallas_call(
        functools.partial(_attn_kernel, out_scale=1.0 - lam_init, key_chunk=min(tk, ATTN_KEY_CHUNK)),
        out_shape=jax.ShapeDtypeStruct((DIFF_VW, n_q), F32),
        grid=(n_hp, n_i, nk),
        in_specs=[
            pl.BlockSpec(memory_space=pltpu.SMEM),
            pl.BlockSpec((LANES, tq), lambda h, i, k: (h, q_off + i)),
            pl.BlockSpec((LANES, tq), q_next_map),
            pl.BlockSpec((tk, LANES), lambda h, i, k: (k_off, 0)),
            pl.BlockSpec((tk, LANES), k_next_map),
            pl.BlockSpec((2, ATTN_VT_ROWS, tk), lambda h, i, k: (h, 0, k_off + k)),
            pl.BlockSpec((DIFF_V, 1), lambda h, i, k: (0, 0)),
        ],
        out_specs=pl.BlockSpec((LANES, tq), lambda h, i, k: (h, i)),
        scratch_shapes=[pltpu.VMEM((4, LANES, tq), BF16),
                        pltpu.VMEM((4, 1, tq), F32),
                        pltpu.VMEM((4, ATTN_VT_ROWS, tq), F32),
                        pltpu.VMEM((2, tk, tq), F32),
                        pltpu.VMEM((8, tq), F32),
                        pltpu.VMEM((tk, LANES), BF16)],
        compiler_params=_params(("arbitrary", "arbitrary", "arbitrary")),
        name="diff_attn",
    )(lam, dqt, dqt, dk, dk, vt_aug, subln.reshape(DIFF_V, 1))


def _merge_kernel(x_ref, mod_ref, g_ref, flat_ref, ftail_ref, of_ref, ob_ref, rg_ref, dlat_ref, dtail_ref,
                  wgt_ref, wbf_ref, wbr_ref, wbd_ref, wo_ref, bd_ref, o_ref, *, n_lat_tiles):
    x = x_ref[...]
    hb = _norm_mod(x, g_ref[...], mod_ref[0, 3:4, :], mod_ref[0, 4:5, :]).astype(BF16)
    gates = _sigmoid(_dot(hb, wgt_ref[...]))
    is_tail = pl.program_id(0) >= n_lat_tiles
    f = jnp.where(is_tail, ftail_ref[...], flat_ref[...])
    d = jnp.where(is_tail, dtail_ref[...], dlat_ref[...]).T
    r = of_ref[...] + ob_ref[...]
    rr_hi, rr_lo = _split_bf16(r * r)
    ms = _dot(rr_hi, bd_ref[...]) + _dot(rr_lo, bd_ref[...])
    rg = rg_ref[...]
    yr = r * lax.rsqrt(ms + EPS) * (rg * _sigmoid(rg))
    mixed = (gates[:, :D_MODEL] * _dot(f.astype(BF16), wbf_ref[...])
             + gates[:, D_MODEL:2 * D_MODEL] * _dot(yr.astype(BF16), wbr_ref[...])
             + gates[:, 2 * D_MODEL:] * _dot(d.astype(BF16), wbd_ref[...]))
    y = _dot(mixed.astype(BF16), wo_ref[...])
    o_ref[...] = x + mod_ref[0, 5:6, :] * y


def _merge_call(x, mods, g, f_lat, f_tail, o_f, o_b, rg, d_lat, d_tail, wgt, wbf, wbr, wbd, wo,
                n_lat_tiles, n_tiles):
    tm = TOKEN_TILE
    bd = np.kron(np.eye(RET_HEADS, dtype=np.float32), np.full((RET_V, RET_V), 1.0 / RET_V, np.float32))

    def lat_spec(w):
        return pl.BlockSpec((tm, w), lambda i: (jnp.minimum(i, n_lat_tiles - 1), 0))

    return pl.pallas_call(
        functools.partial(_merge_kernel, n_lat_tiles=n_lat_tiles),
        out_shape=jax.ShapeDtypeStruct((n_tiles * tm, D_MODEL), F32),
        grid=(n_tiles,),
        in_specs=[
            _row_spec(D_MODEL), _mod_spec(n_lat_tiles), _const_spec((1, D_MODEL)),
            lat_spec(F_W), _const_spec((tm, F_W)),
            _row_spec(RET_VW), _row_spec(RET_VW), _row_spec(RET_VW),
            pl.BlockSpec((DIFF_VW, tm), lambda i: (0, jnp.minimum(i, n_lat_tiles - 1))),
            _const_spec((DIFF_VW, tm)),
            _const_spec((D_MODEL, GATE_W)), _const_spec((F_W, D_MODEL)), _const_spec((RET_VW, D_MODEL)),
            _const_spec((DIFF_VW, D_MODEL)), _const_spec((D_MODEL, D_MODEL)), _const_spec((RET_VW, RET_VW)),
        ],
        out_specs=_row_spec(D_MODEL),
        compiler_params=_params(("arbitrary",)),
        name="merge",
    )(x, mods, g.reshape(1, D_MODEL), f_lat, f_tail, o_f, o_b, rg, d_lat, d_tail, wgt, wbf, wbr, wbd, wo,
      jnp.asarray(bd, BF16))


def _pick_tile(n, candidates):
    for c in candidates:
        if n % c == 0:
            return c
    raise ValueError(f"no tile for {n}")


def kernel(x, c, ctx, c_ctx, w_ada, b_ada, norm_g, ffn_w1, ffn_w3, ffn_w2, w_in, ret_decay_logit,
           diff_lambda, diff_subln, w_branch_f, w_branch_r, w_branch_d, w_out, final_g):
    batch, seq, d = x.shape
    ctx_len = ctx.shape[1]
    tm = TOKEN_TILE
    assert batch == 1 and d == D_MODEL
    assert seq % max(DFT_N1 * 8, tm) == 0 and ctx_len % 256 == 0 and ctx_len <= tm
    total = seq + ctx_len
    n_lat_tiles = seq // tm
    n_tiles = n_lat_tiles + 1
    n_rows = n_tiles * tm
    n_lat_chunks = seq // RET_CHUNK
    n_ctx_chunks = ctx_len // RET_CHUNK
    n_pad_chunks = (n_rows - total) // RET_CHUNK

    cc = jnp.zeros((8, D_MODEL), F32).at[0].set(c[0]).at[1].set(c_ctx)
    mods_all = _ada_call(cc, w_ada, b_ada)[:, :2].reshape(DEPTH, 2, N_MOD, D_MODEL)

    tables = _rope_tables(seq, n_rows)
    twc, tws = _twiddles(seq)
    log_g2_all = jax.nn.log_sigmoid(ret_decay_logit.astype(F32))
    lv = diff_lambda.astype(F32)
    w_aug_all, wgt_all = _prep_proj_weights(w_in)
    w1_all, w3_all, w2_all = ffn_w1.astype(BF16), ffn_w3.astype(BF16), ffn_w2.astype(BF16)
    wbf_all, wbr_all = w_branch_f.astype(BF16), w_branch_r.astype(BF16)
    wbd_all, wo_all = w_branch_d.astype(BF16), w_out.astype(BF16)

    tq = _pick_tile(seq, (512, 256, 128))
    tk = _pick_tile(total, (3328, 1280, 640, 256, 128))
    tail_pad = ((0, tm - ctx_len), (0, 0))

    xs = jnp.concatenate([x[0], ctx[0], jnp.zeros((n_rows - total, D_MODEL), F32)], axis=0)
    for l in range(DEPTH):
        last = l == DEPTH - 1
        lam_init = 0.8 - 0.6 * math.exp(-0.3 * l)
        mods = mods_all[l]
        lam = (jnp.exp(jnp.sum(lv[l, 0] * lv[l, 1])) - jnp.exp(jnp.sum(lv[l, 2] * lv[l, 3]))
               + lam_init).reshape(1)

        xs = _ffn_call(xs, mods, norm_g[l, 0], w1_all[l, 0], w3_all[l, 0], w2_all[l, 0], 0,
                       n_lat_tiles, n_tiles)

        uf, rq, rkt, rv, rg, dqt, dk, vt_aug = _proj_call(xs, mods, norm_g[l, 1], w_aug_all[l], tables,
                                                           n_lat_tiles, n_tiles)

        f_lat = _fourier_latent(uf, seq, twc, tws)
        o_f, o_b = _retention_call(rq, rkt, rv, log_g2_all[l], n_lat_chunks, n_ctx_chunks, n_pad_chunks)
        d_lat = _attn_call(dqt, dk, vt_aug, lam, diff_subln[l], lam_init, seq, tq, tk, total // tk)

        if last:
            f_tail = jnp.zeros((tm, F_W), F32)
            d_tail = jnp.zeros((DIFF_VW, tm), F32)
            n_out = n_lat_tiles
        else:
            f_tail = jnp.pad(_fourier_ctx(uf[seq:total]), tail_pad)
            d_ctx = _attn_call(dqt, dk, vt_aug, lam, diff_subln[l], lam_init, ctx_len, ctx_len, ctx_len, 1,
                               q_off=seq // ctx_len, k_off=seq // ctx_len)
            d_tail = jnp.pad(d_ctx, tail_pad[::-1])
            n_out = n_tiles

        xs = _merge_call(xs, mods, norm_g[l, 1], f_lat, f_tail, o_f, o_b, rg, d_lat, d_tail, wgt_all[l],
                         wbf_all[l], wbr_all[l], wbd_all[l], wo_all[l], n_lat_tiles, n_out)

        xs = _ffn_call(xs, mods, norm_g[l, 2], w1_all[l, 1], w3_all[l, 1], w2_all[l, 1], 6,
                       n_lat_tiles, n_out, final_g=final_g if last else None)

    return xs.reshape(1, seq, D_MODEL)
```

```python
import functools
import math

import numpy as np
import jax
import jax.numpy as jnp
from jax import lax
from jax.experimental import pallas as pl
from jax.experimental.pallas import tpu as pltpu

D_MODEL = 1024
DEPTH = 4
GRID_W = 64
D_FF = 2816
N_MOD = 9
FOURIER_GROUPS = 4
FOURIER_CH = 64
RET_HEADS = 6
RET_QK = 32
RET_V = 64
RET_CHUNK = 128
DIFF_HEADS = 6
DIFF_QK = 32
DIFF_V = 64
ROPE_BASE = 10000.0
EPS = 1e-6
F_W = FOURIER_GROUPS * FOURIER_CH
RET_QW = RET_HEADS * RET_QK
RET_VW = RET_HEADS * RET_V
DIFF_QW = DIFF_HEADS * 2 * DIFF_QK
DIFF_VW = DIFF_HEADS * DIFF_V
GATE_W = 3 * D_MODEL

LANES = 128
VMEM_LIMIT_BYTES = 56 * 1024 * 1024

RET_QP = 2 * LANES
RET_STEP_CHUNKS = 2
TOKEN_TILE = 512
DFT_N1 = 128
ATTN_KEY_CHUNK = 256
ATTN_VT_ROWS = DIFF_V + 16

BF16 = jnp.bfloat16
F32 = jnp.float32
LOG2E = math.log2(math.e)


def _dot(a, b):
    return jnp.dot(a, b, preferred_element_type=F32)


def _split_bf16(x):
    hi = x.astype(BF16)
    lo = (x - hi.astype(F32)).astype(BF16)
    return hi, lo


def _dot3_split(a, b):
    (ah, al), (bh, bl) = a, b
    return _dot(ah, bh) + _dot(al, bh) + _dot(ah, bl)


def _dot3(a, b):
    return _dot3_split(_split_bf16(a), _split_bf16(b))


def _norm_mod(x, g, shift, scale):
    ms = jnp.mean(x * x, axis=-1, keepdims=True)
    y = x * lax.rsqrt(ms + EPS) * g
    return y * (1.0 + scale) + shift


def _sigmoid(x):
    return 1.0 / (1.0 + jnp.exp(-x))


def _const_spec(shape):
    nd = len(shape)
    return pl.BlockSpec(shape, lambda *_: (0,) * nd, pipeline_mode=pl.Buffered(1))


def _params(sem):
    return pltpu.CompilerParams(dimension_semantics=sem, vmem_limit_bytes=VMEM_LIMIT_BYTES)


def _row_spec(width):
    return pl.BlockSpec((TOKEN_TILE, width), lambda i: (i, 0))


def _ada_kernel(cc_ref, w_ref, b_ref, o_ref):
    cc = cc_ref[...]
    s = cc * _sigmoid(cc)
    o_ref[0] = _dot3(s, w_ref[0]) + b_ref[0]


def _ada_call(cc, w_ada, b_ada):
    depth, d, n = w_ada.shape
    tn = 1152
    return pl.pallas_call(
        _ada_kernel,
        out_shape=jax.ShapeDtypeStruct((depth, 8, n), F32),
        grid=(depth, n // tn),
        in_specs=[
            pl.BlockSpec((8, d), lambda l, j: (0, 0)),
            pl.BlockSpec((1, d, tn), lambda l, j: (l, 0, j)),
            pl.BlockSpec((1, 1, tn), lambda l, j: (l, 0, j)),
        ],
        out_specs=pl.BlockSpec((1, 8, tn), lambda l, j: (l, 0, j)),
        compiler_params=_params(("arbitrary", "arbitrary")),
        name="adaln",
    )(cc, w_ada, b_ada.reshape(depth, 1, n))


def _mod_spec(n_lat_tiles):
    return pl.BlockSpec((1, N_MOD, D_MODEL), lambda i: (jnp.where(i >= n_lat_tiles, 1, 0), 0, 0))


def _ffn_tile(x, mod_ref, g_ref, w1_ref, w3_ref, w2_ref, base):
    shift = mod_ref[0, base:base + 1, :]
    scale = mod_ref[0, base + 1:base + 2, :]
    gate = mod_ref[0, base + 2:base + 3, :]
    hb = _norm_mod(x, g_ref[...], shift, scale).astype(BF16)
    a = _dot(hb, w1_ref[...])
    b = _dot(hb, w3_ref[...])
    u = (a * _sigmoid(a) * b).astype(BF16)
    return x + (0.5 * gate) * _dot(u, w2_ref[...])


def _ffn_kernel(x_ref, mod_ref, g_ref, w1_ref, w3_ref, w2_ref, o_ref, *, base):
    o_ref[...] = _ffn_tile(x_ref[...], mod_ref, g_ref, w1_ref, w3_ref, w2_ref, base)


def _ffn_final_kernel(x_ref, mod_ref, g_ref, w1_ref, w3_ref, w2_ref, fg_ref, o_ref, *, base):
    y = _ffn_tile(x_ref[...], mod_ref, g_ref, w1_ref, w3_ref, w2_ref, base)
    ms = jnp.mean(y * y, axis=-1, keepdims=True)
    o_ref[...] = y * lax.rsqrt(ms + EPS) * fg_ref[...]


def _ffn_call(x, mods, g, w1, w3, w2, base, n_lat_tiles, n_tiles, final_g=None):
    in_specs = [
        _row_spec(D_MODEL), _mod_spec(n_lat_tiles), _const_spec((1, D_MODEL)),
        _const_spec((D_MODEL, D_FF)), _const_spec((D_MODEL, D_FF)), _const_spec((D_FF, D_MODEL)),
    ]
    args = [x, mods, g.reshape(1, D_MODEL), w1, w3, w2]
    body = _ffn_kernel
    if final_g is not None:
        in_specs.append(_const_spec((1, D_MODEL)))
        args.append(final_g.reshape(1, D_MODEL))
        body = _ffn_final_kernel
    return pl.pallas_call(
        functools.partial(body, base=base),
        out_shape=jax.ShapeDtypeStruct((n_tiles * TOKEN_TILE, D_MODEL), F32),
        grid=(n_tiles,),
        in_specs=in_specs,
        out_specs=_row_spec(D_MODEL),
        compiler_params=_params(("arbitrary",)),
        name="ffn",
    )(*args)


_C_F, _C_RQ, _C_RK, _C_RV, _C_RG = 0, 256, 512, 768, 1152
_C_DQ, _C_DK, _C_DV = 1536, 1920, 2304
_C_RQR, _C_RKR, _C_DQR, _C_DKR = 2688, 2944, 3200, 3584
PROJ_W = 3968


def _rotate_half_cols(w, block):
    depth, d, width = w.shape
    w5 = w.reshape(depth, d, width // block, 2, block // 2)
    return jnp.concatenate([-w5[:, :, :, 1:2], w5[:, :, :, 0:1]], axis=3).reshape(depth, d, width)


def _prep_proj_weights(w_in):
    cuts = np.cumsum([F_W, RET_QW, RET_QW, RET_VW, RET_VW, DIFF_QW, DIFF_QW, DIFF_VW])
    wf, wrq, wrk, wrv, wrg, wdq, wdk, wdv, wgt = jnp.split(w_in, cuts, axis=2)
    z = jnp.zeros(w_in.shape[:2] + (RET_QP - RET_QW,), w_in.dtype)
    parts = [wf, wrq, z, wrk, z, wrv, wrg, wdq, wdk, wdv,
             _rotate_half_cols(wrq, RET_QK), z, _rotate_half_cols(wrk, RET_QK), z,
             _rotate_half_cols(wdq, DIFF_QK // 2), _rotate_half_cols(wdk, DIFF_QK // 2)]
    return jnp.concatenate(parts, axis=2).astype(BF16), wgt.astype(BF16)


def _rope_tables(seq, n_rows):
    pos = jnp.arange(seq, dtype=F32)
    inv_r = ROPE_BASE ** (-jnp.arange(0, RET_QK, 2, dtype=F32) / RET_QK)
    ang_r = pos[:, None] * inv_r[None, :]
    cos_r = jnp.tile(jnp.cos(ang_r), (1, 2 * LANES // RET_QK))
    sin_r = jnp.tile(jnp.sin(ang_r), (1, 2 * LANES // RET_QK))
    rows = jnp.repeat(jnp.arange(seq // GRID_W, dtype=F32), GRID_W)
    cols = jnp.tile(jnp.arange(GRID_W, dtype=F32), seq // GRID_W)
    dim = DIFF_QK // 2
    inv_d = ROPE_BASE ** (-jnp.arange(0, dim, 2, dtype=F32) / dim)
    a_row = rows[:, None] * inv_d[None, :]
    a_col = cols[:, None] * inv_d[None, :]
    cos_hm = jnp.concatenate([jnp.cos(a_row)] * 2 + [jnp.cos(a_col)] * 2, axis=1)
    sin_hm = jnp.concatenate([jnp.sin(a_row)] * 2 + [jnp.sin(a_col)] * 2, axis=1)
    cos_d = jnp.tile(cos_hm, (1, LANES // DIFF_QK))
    sin_d = jnp.tile(sin_hm, (1, LANES // DIFF_QK))

    def finish(t, fill):
        return jnp.pad(t, ((0, n_rows - seq), (0, 0)), constant_values=fill)

    return finish(cos_r, 1.0), finish(sin_r, 0.0), finish(cos_d, 1.0), finish(sin_d, 0.0)


def _proj_kernel(x_ref, mod_ref, g_ref, w_ref, cr_ref, sr_ref, cd_ref, sd_ref,
                 uf_ref, rq_ref, rkt_ref, rv_ref, rg_ref, dqt_ref, dk_ref, vta_ref):
    x = x_ref[...]
    hb = _norm_mod(x, g_ref[...], mod_ref[0, 3:4, :], mod_ref[0, 4:5, :]).astype(BF16)
    p = _dot(hb, w_ref[...])
    cr = jnp.concatenate([cr_ref[...]] * (RET_QP // LANES), axis=1)
    sr = jnp.concatenate([sr_ref[...]] * (RET_QP // LANES), axis=1)
    cd = jnp.concatenate([cd_ref[...]] * (DIFF_QW // LANES), axis=1)
    sd = jnp.concatenate([sd_ref[...]] * (DIFF_QW // LANES), axis=1)
    uf_ref[...] = p[:, _C_F:_C_F + F_W]
    rq = p[:, _C_RQ:_C_RQ + RET_QP] * cr + p[:, _C_RQR:_C_RQR + RET_QP] * sr
    rk = p[:, _C_RK:_C_RK + RET_QP] * cr + p[:, _C_RKR:_C_RKR + RET_QP] * sr
    rq_ref[...] = rq.astype(BF16)
    rkt_ref[...] = (rk * (RET_QK ** -0.5)).T.astype(BF16)
    rv_ref[...] = p[:, _C_RV:_C_RV + RET_VW].astype(BF16)
    rg_ref[...] = p[:, _C_RG:_C_RG + RET_VW]
    dq = p[:, _C_DQ:_C_DQ + DIFF_QW] * cd + p[:, _C_DQR:_C_DQR + DIFF_QW] * sd
    dk = p[:, _C_DK:_C_DK + DIFF_QW] * cd + p[:, _C_DKR:_C_DKR + DIFF_QW] * sd
    dqt_ref[...] = (dq * ((DIFF_QK ** -0.5) * LOG2E)).T.astype(BF16)
    dk_ref[...] = dk.astype(BF16)
    vt = p[:, _C_DV:_C_DV + DIFF_VW].T
    ones = jnp.ones((ATTN_VT_ROWS - DIFF_V, x.shape[0]), F32)
    for h in range(DIFF_HEADS):
        vta_ref[h] = jnp.concatenate([vt[h * DIFF_V:(h + 1) * DIFF_V], ones], axis=0).astype(BF16)


def _proj_call(x, mods, g, w_aug, tables, n_lat_tiles, n_tiles):
    tm = TOKEN_TILE
    t = n_tiles * tm
    out_shape = (
        jax.ShapeDtypeStruct((t, F_W), F32),
        jax.ShapeDtypeStruct((t, RET_QP), BF16),
        jax.ShapeDtypeStruct((RET_QP, t), BF16),
        jax.ShapeDtypeStruct((t, RET_VW), BF16),
        jax.ShapeDtypeStruct((t, RET_VW), F32),
        jax.ShapeDtypeStruct((DIFF_QW, t), BF16),
        jax.ShapeDtypeStruct((t, DIFF_QW), BF16),
        jax.ShapeDtypeStruct((DIFF_HEADS, ATTN_VT_ROWS, t), BF16),
    )
    out_specs = (
        _row_spec(F_W), _row_spec(RET_QP),
        pl.BlockSpec((RET_QP, tm), lambda i: (0, i)),
        _row_spec(RET_VW), _row_spec(RET_VW),
        pl.BlockSpec((DIFF_QW, tm), lambda i: (0, i)),
        _row_spec(DIFF_QW),
        pl.BlockSpec((DIFF_HEADS, ATTN_VT_ROWS, tm), lambda i: (0, 0, i)),
    )
    return pl.pallas_call(
        _proj_kernel,
        out_shape=out_shape,
        grid=(n_tiles,),
        in_specs=[
            _row_spec(D_MODEL), _mod_spec(n_lat_tiles), _const_spec((1, D_MODEL)),
            _const_spec((D_MODEL, PROJ_W)),
            _row_spec(LANES), _row_spec(LANES), _row_spec(LANES), _row_spec(LANES),
        ],
        out_specs=out_specs,
        compiler_params=_params(("arbitrary",)),
        name="mixer_proj",
    )(x, mods, g.reshape(1, D_MODEL), w_aug, *tables)


def _dft_mats(n):
    k = np.arange(n)
    ang = 2.0 * np.pi * ((k[:, None] * k[None, :]) % n) / n
    return np.cos(ang).astype(np.float32), np.sin(ang).astype(np.float32)


def _channel_dft_mats():
    c, s = _dft_mats(FOURIER_CH)
    eye = np.eye(FOURIER_GROUPS, dtype=np.float32)
    return np.kron(eye, c), np.kron(eye, s)


def _fourier_stage1_kernel(x_ref, cc_ref, sc_ref, c1_ref, s1_ref, twc_ref, tws_ref, tr_ref, ti_ref, *, nb):
    cc, sc = _split_bf16(cc_ref[...]), _split_bf16(sc_ref[...])
    zr, zi = [], []
    for j in range(nb):
        u = _split_bf16(x_ref[:, j * F_W:(j + 1) * F_W])
        zr.append(_dot3_split(u, cc))
        zi.append(-_dot3_split(u, sc))
    zr = _split_bf16(jnp.concatenate(zr, axis=1) if nb > 1 else zr[0])
    zi = _split_bf16(jnp.concatenate(zi, axis=1) if nb > 1 else zi[0])
    c1, s1 = _split_bf16(c1_ref[...]), _split_bf16(s1_ref[...])
    tr = _dot3_split(c1, zr) + _dot3_split(s1, zi)
    ti = _dot3_split(c1, zi) - _dot3_split(s1, zr)
    twc, tws = twc_ref[0], tws_ref[0]
    for j in range(nb):
        cols = slice(j * F_W, (j + 1) * F_W)
        c, s = twc[:, j:j + 1], tws[:, j:j + 1]
        tr_ref[:, cols] = tr[:, cols] * c + ti[:, cols] * s
        ti_ref[:, cols] = ti[:, cols] * c - tr[:, cols] * s


def _fourier_stage2_kernel(tr_ref, ti_ref, c2_ref, s2_ref, o_ref, *, kb):
    c2, s2 = c2_ref[...], s2_ref[...]
    for j in range(kb):
        o_ref[:, j, :] = _dot3(c2, tr_ref[j]) + _dot3(s2, ti_ref[j])


def _fourier_latent(u_all, seq, twc, tws):
    n1, n2 = DFT_N1, seq // DFT_N1
    nb = _fourier_block(n2)
    kb = 8
    cc, sc = _channel_dft_mats()
    c1, s1 = _dft_mats(n1)
    c2, s2 = _dft_mats(n2)
    x2 = u_all[:seq].reshape(n1, n2 * F_W)
    blk = pl.BlockSpec((n1, nb * F_W), lambda i: (0, i))
    tw_blk = pl.BlockSpec((1, n1, nb), lambda i: (i, 0, 0))
    tr, ti = pl.pallas_call(
        functools.partial(_fourier_stage1_kernel, nb=nb),
        out_shape=(jax.ShapeDtypeStruct((n1, n2 * F_W), F32),) * 2,
        grid=(n2 // nb,),
        in_specs=[blk, _const_spec((F_W, F_W)), _const_spec((F_W, F_W)),
                  _const_spec((n1, n1)), _const_spec((n1, n1)), tw_blk, tw_blk],
        out_specs=(blk, blk),
        compiler_params=_params(("arbitrary",)),
        name="fourier_stage1",
    )(x2, cc, sc, c1, s1, twc, tws)
    tr3 = tr.reshape(n1, n2, F_W)
    ti3 = ti.reshape(n1, n2, F_W)
    o3 = pl.pallas_call(
        functools.partial(_fourier_stage2_kernel, kb=kb),
        out_shape=jax.ShapeDtypeStruct((n2, n1, F_W), F32),
        grid=(n1 // kb,),
        in_specs=[pl.BlockSpec((kb, n2, F_W), lambda i: (i, 0, 0))] * 2
        + [_const_spec((n2, n2)), _const_spec((n2, n2))],
        out_specs=pl.BlockSpec((n2, kb, F_W), lambda i: (0, i, 0)),
        compiler_params=_params(("arbitrary",)),
        name="fourier_stage2",
    )(tr3, ti3, c2, s2)
    return o3.reshape(seq, F_W)


def _twiddles(seq):
    n1, n2 = DFT_N1, seq // DFT_N1
    k1 = jnp.arange(n1, dtype=jnp.int32)[:, None]
    m2 = jnp.arange(n2, dtype=jnp.int32)[None, :]
    ang = (2.0 * math.pi / seq) * ((k1 * m2) % seq).astype(F32)
    scale = 1.0 / math.sqrt(seq * FOURIER_CH)
    nb = _fourier_block(n2)

    def blocked(t):
        return t.reshape(n1, n2 // nb, nb).transpose(1, 0, 2)

    return blocked(jnp.cos(ang) * scale), blocked(jnp.sin(ang) * scale)


def _fourier_block(n2):
    return min(8, n2)


def _fourier_ctx_kernel(u_ref, cc_ref, sc_ref, cl_ref, sl_ref, o_ref, *, scale):
    u = u_ref[...]
    a = _dot3(u, cc_ref[...])
    b = _dot3(u, sc_ref[...])
    o_ref[...] = (_dot3(cl_ref[...], a) - _dot3(sl_ref[...], b)) * scale


def _fourier_ctx(u):
    n = u.shape[0]
    cc, sc = _channel_dft_mats()
    cl, sl = _dft_mats(n)
    return pl.pallas_call(
        functools.partial(_fourier_ctx_kernel, scale=1.0 / math.sqrt(n * FOURIER_CH)),
        out_shape=jax.ShapeDtypeStruct((n, F_W), F32),
        name="fourier_ctx",
    )(u, cc, sc, cl, sl)


def _ret_chunk_local(q, kt, v, d_ref, kdec, bdmask, hmask_ref, vmask_ref):
    pieces = []
    vparts = []
    for h in range(RET_HEADS):
        qh = q * hmask_ref[h]
        pieces.append((_dot(qh, kt) * d_ref[h]).astype(BF16))
        vparts.append(v * vmask_ref[h])
    inner = jnp.concatenate(pieces, axis=1)
    vbd = jnp.concatenate(vparts, axis=0)
    kd = (kt.astype(F32) * kdec).astype(BF16)
    return _dot(inner, vbd), bdmask * _dot(kd, v)


def _ret_dir(q_ref, kt_ref, v_ref, o_ref, s_ref, d_ref, qdec, kdec, cd, bdmask, hmask_ref, vmask_ref, order):
    c = RET_CHUNK
    local = []
    for half in order:
        rows = slice(half * c, (half + 1) * c)
        q = q_ref[rows, :]
        local.append((rows, q) + _ret_chunk_local(q, kt_ref[:, rows], v_ref[rows, :], d_ref, kdec, bdmask,
                                                  hmask_ref, vmask_ref))
    s = s_ref[...]
    for rows, q, o_intra, inc in local:
        o_ref[rows, :] = o_intra + _dot(q, s.astype(BF16)) * qdec
        s = s * cd + inc
    s_ref[...] = s


def _ret_kernel(logg_ref, lgv_ref, lgk_ref, bdmask_ref, hmask_ref, vmask_ref,
                qf_ref, ktf_ref, vf_ref, qb_ref, ktb_ref, vb_ref,
                of_ref, ob_ref,
                sf_ref, sb_ref, df_ref, db_ref, qdf_ref, qdb_ref, kdf_ref, kdb_ref, cdf_ref, cdb_ref):
    c = RET_CHUNK

    @pl.when(pl.program_id(0) == 0)
    def _():
        sf_ref[...] = jnp.zeros_like(sf_ref)
        sb_ref[...] = jnp.zeros_like(sb_ref)
        ii = lax.broadcasted_iota(jnp.int32, (c, c), 0).astype(F32)
        jj = lax.broadcasted_iota(jnp.int32, (c, c), 1).astype(F32)
        for h in range(RET_HEADS):
            df_ref[h] = jnp.where(ii >= jj, jnp.exp(logg_ref[0, h] * jnp.maximum(ii - jj, 0.0)), 0.0)
            db_ref[h] = jnp.where(jj >= ii, jnp.exp(logg_ref[1, h] * jnp.maximum(jj - ii, 0.0)), 0.0)
        ri = lax.broadcasted_iota(jnp.int32, (c, RET_VW), 0).astype(F32)
        qdf_ref[...] = jnp.exp(lgv_ref[0] * (ri + 1.0))
        qdb_ref[...] = jnp.exp(lgv_ref[1] * (c - ri))
        cj = lax.broadcasted_iota(jnp.int32, (RET_QP, c), 1).astype(F32)
        kdf_ref[...] = jnp.exp(lgk_ref[0] * (c - 1.0 - cj))
        kdb_ref[...] = jnp.exp(lgk_ref[1] * cj)
        cdf_ref[...] = jnp.exp(lgv_ref[0] * float(c))
        cdb_ref[...] = jnp.exp(lgv_ref[1] * float(c))

    bdmask = bdmask_ref[...]
    halves = list(range(RET_STEP_CHUNKS))
    _ret_dir(qf_ref, ktf_ref, vf_ref, of_ref, sf_ref, df_ref,
             qdf_ref[...], kdf_ref[...], cdf_ref[...], bdmask, hmask_ref, vmask_ref, halves)
    _ret_dir(qb_ref, ktb_ref, vb_ref, ob_ref, sb_ref, db_ref,
             qdb_ref[...], kdb_ref[...], cdb_ref[...], bdmask, hmask_ref, vmask_ref, halves[::-1])


def _ret_masks():
    hm = np.zeros((RET_HEADS, 1, RET_QP), np.float32)
    vm = np.zeros((RET_HEADS, 1, RET_VW), np.float32)
    bd = np.zeros((RET_QP, RET_VW), np.float32)
    for h in range(RET_HEADS):
        hm[h, 0, h * RET_QK:(h + 1) * RET_QK] = 1.0
        vm[h, 0, h * RET_V:(h + 1) * RET_V] = 1.0
        bd[h * RET_QK:(h + 1) * RET_QK, h * RET_V:(h + 1) * RET_V] = 1.0
    return jnp.asarray(bd), jnp.asarray(hm, BF16), jnp.asarray(vm, BF16)


def _retention_call(rq, rkt, rv, log_g2, n_lat, n_ctx, n_pad):
    c = RET_CHUNK
    g = RET_STEP_CHUNKS
    assert n_lat % g == 0 and n_ctx % g == 0 and n_pad % g == 0
    n_lat, n_ctx, n_pad = n_lat // g, n_ctx // g, n_pad // g
    n_real = n_lat + n_ctx
    n = n_real + n_pad
    rows = g * c

    def fwd(i):
        return jnp.where(i < n_ctx, n_lat + i, jnp.where(i < n_real, i - n_ctx, i))

    def bwd(i):
        return jnp.where(i < n_real, n_real - 1 - i, i)

    lgv = jnp.repeat(log_g2, RET_V, axis=1).reshape(2, 1, RET_VW)
    lgk = jnp.pad(jnp.repeat(log_g2, RET_QK, axis=1), ((0, 0), (0, RET_QP - RET_QW)))
    lgk = jnp.broadcast_to(lgk[:, :, None], (2, RET_QP, c))
    bd, hm, vm = _ret_masks()

    def specs(ix):
        return [pl.BlockSpec((rows, RET_QP), lambda i: (ix(i), 0)),
                pl.BlockSpec((RET_QP, rows), lambda i: (0, ix(i))),
                pl.BlockSpec((rows, RET_VW), lambda i: (ix(i), 0))]

    vmem = pltpu.VMEM
    return pl.pallas_call(
        _ret_kernel,
        out_shape=(jax.ShapeDtypeStruct((n * rows, RET_VW), F32),) * 2,
        grid=(n,),
        in_specs=[pl.BlockSpec(memory_space=pltpu.SMEM),
                  _const_spec((2, 1, RET_VW)), _const_spec((2, RET_QP, c)),
                  _const_spec((RET_QP, RET_VW)), _const_spec((RET_HEADS, 1, RET_QP)),
                  _const_spec((RET_HEADS, 1, RET_VW))] + specs(fwd) + specs(bwd),
        out_specs=(pl.BlockSpec((rows, RET_VW), lambda i: (fwd(i), 0)),
                   pl.BlockSpec((rows, RET_VW), lambda i: (bwd(i), 0))),
        scratch_shapes=[vmem((RET_QP, RET_VW), F32), vmem((RET_QP, RET_VW), F32),
                        vmem((RET_HEADS, c, c), F32), vmem((RET_HEADS, c, c), F32),
                        vmem((c, RET_VW), F32), vmem((c, RET_VW), F32),
                        vmem((RET_QP, c), F32), vmem((RET_QP, c), F32),
                        vmem((1, RET_VW), F32), vmem((1, RET_VW), F32)],
        compiler_params=_params(("arbitrary",)),
        name="retention",
    )(log_g2, lgv, lgk, bd, hm, vm, rq, rkt, rv, rq, rkt, rv)


def _chunk_scores(k_chunk, qm, s_out_ref, rows, run_max):
    s = _dot(k_chunk, qm)
    s_out_ref[rows, :] = s
    cm = jnp.max(s.reshape(s.shape[0] // 8, 8, s.shape[1]), axis=0)
    return cm if run_max is None else jnp.maximum(run_max, cm)


def _mask_map(qt, j):
    row = lax.broadcasted_iota(jnp.int32, qt.shape, 0)
    lo = j * DIFF_QK
    return jnp.where((row >= lo) & (row < lo + DIFF_QK), qt, jnp.zeros_like(qt))


def _attn_kernel(lam_ref, qt_ref, qtn_ref, k0_ref, kn_ref, vt_ref, subln_ref, o_ref,
                 qm_ref, m_ref, acc_ref, s_ref, mx_ref, kcur_ref, *, out_scale, key_chunk):
    ki = pl.program_id(2)
    is_last = ki == pl.num_programs(2) - 1
    tk = kn_ref.shape[0]
    n_chunks = tk // key_chunk

    def chunk(c):
        return slice(c * key_chunk, (c + 1) * key_chunk)

    @pl.when(ki == 0)
    def _():
        qt = qt_ref[...]
        for j in range(4):
            qm_ref[j] = _mask_map(qt, j)
        m_ref[...] = jnp.full(m_ref.shape, -jnp.inf, F32)
        acc_ref[...] = jnp.zeros_like(acc_ref)

    @pl.when((ki == 0) & (pl.program_id(0) == 0) & (pl.program_id(1) == 0))
    def _():
        kcur_ref[...] = k0_ref[...]
        run = None
        for c in range(n_chunks):
            run = _chunk_scores(k0_ref[chunk(c), :], qm_ref[0], s_ref.at[0], chunk(c), run)
        mx_ref[...] = run

    q_ahead = jnp.where(is_last, _mask_map(qtn_ref[...], 0), qm_ref[0])

    run = mx_ref[...]
    for u in range(4):
        cur, nxt = u % 2, (u + 1) % 2
        m_prev = m_ref[u]
        m_new = jnp.maximum(m_prev, jnp.max(run, axis=0, keepdims=True))
        alpha = jnp.exp2(m_prev - m_new)
        vt = vt_ref[u // 2]
        run = None
        acc = None
        p_prev = None
        for c in range(n_chunks):
            if u < 3:
                run = _chunk_scores(kcur_ref[chunk(c), :], qm_ref[u + 1], s_ref.at[nxt], chunk(c), run)
            else:
                run = _chunk_scores(kn_ref[chunk(c), :], q_ahead, s_ref.at[nxt], chunk(c), run)
            p = jnp.exp2(s_ref[cur, chunk(c), :] - m_new).astype(BF16)
            if p_prev is not None:
                pv = _dot(vt[:, chunk(c - 1)], p_prev)
                acc = pv if acc is None else acc + pv
            p_prev = p
        pv = _dot(vt[:, chunk(n_chunks - 1)], p_prev)
        acc = pv if acc is None else acc + pv
        acc_ref[u] = alpha * acc_ref[u] + acc
        m_ref[u] = m_new
    mx_ref[...] = run
    kcur_ref[...] = kn_ref[...]

    @pl.when(is_last)
    def _():
        lam = lam_ref[0]
        outs = []
        for hl in range(2):
            a1 = acc_ref[2 * hl]
            a2 = acc_ref[2 * hl + 1]
            o = (a1[:DIFF_V] / a1[DIFF_V:DIFF_V + 1]
                 - lam * (a2[:DIFF_V] / a2[DIFF_V:DIFF_V + 1]))
            ms = jnp.mean(o * o, axis=0, keepdims=True)
            outs.append(o * lax.rsqrt(ms + EPS) * subln_ref[...] * out_scale)
        o_ref[...] = jnp.concatenate(outs, axis=0)


def _attn_call(dqt, dk, vt_aug, lam, subln, lam_init, n_q, tq, tk, nk, q_off=0, k_off=0):
    n_hp = DIFF_HEADS // 2
    n_i = n_q // tq

    def next_pair(h, i):
        wrap = i + 1 >= n_i
        return jnp.where(wrap, jnp.minimum(h + 1, n_hp - 1), h), jnp.where(wrap, 0, i + 1)

    def q_next_map(h, i, k):
        hn, i_n = next_pair(h, i)
        return hn, q_off + i_n

    def k_next_map(h, i, k):
        last = k + 1 >= nk
        return k_off + jnp.where(last, 0, k + 1), jnp.where(last, next_pair(h, i)[0], h)

    return pl.pallas_call(
        functools.partial(_attn_kernel, out_scale=1.0 - lam_init, key_chunk=min(tk, ATTN_KEY_CHUNK)),
        out_shape=jax.ShapeDtypeStruct((DIFF_VW, n_q), F32),
        grid=(n_hp, n_i, nk),
        in_specs=[
            pl.BlockSpec(memory_space=pltpu.SMEM),
            pl.BlockSpec((LANES, tq), lambda h, i, k: (h, q_off + i)),
            pl.BlockSpec((LANES, tq), q_next_map),
            pl.BlockSpec((tk, LANES), lambda h, i, k: (k_off, 0)),
            pl.BlockSpec((tk, LANES), k_next_map),
            pl.BlockSpec((2, ATTN_VT_ROWS, tk), lambda h, i, k: (h, 0, k_off + k)),
            pl.BlockSpec((DIFF_V, 1), lambda h, i, k: (0, 0)),
        ],
        out_specs=pl.BlockSpec((LANES, tq), lambda h, i, k: (h, i)),
        scratch_shapes=[pltpu.VMEM((4, LANES, tq), BF16),
                        pltpu.VMEM((4, 1, tq), F32),
                        pltpu.VMEM((4, ATTN_VT_ROWS, tq), F32),
                        pltpu.VMEM((2, tk, tq), F32),
                        pltpu.VMEM((8, tq), F32),
                        pltpu.VMEM((tk, LANES), BF16)],
        compiler_params=_params(("arbitrary", "arbitrary", "arbitrary")),
        name="diff_attn",
    )(lam, dqt, dqt, dk, dk, vt_aug, subln.reshape(DIFF_V, 1))


def _merge_kernel(x_ref, mod_ref, g_ref, flat_ref, ftail_ref, of_ref, ob_ref, rg_ref, dlat_ref, dtail_ref,
                  wgt_ref, wbf_ref, wbr_ref, wbd_ref, wo_ref, bd_ref, o_ref, *, n_lat_tiles):
    x = x_ref[...]
    hb = _norm_mod(x, g_ref[...], mod_ref[0, 3:4, :], mod_ref[0, 4:5, :]).astype(BF16)
    gates = _sigmoid(_dot(hb, wgt_ref[...]))
    is_tail = pl.program_id(0) >= n_lat_tiles
    f = jnp.where(is_tail, ftail_ref[...], flat_ref[...])
    d = jnp.where(is_tail, dtail_ref[...], dlat_ref[...]).T
    r = of_ref[...] + ob_ref[...]
    rr_hi, rr_lo = _split_bf16(r * r)
    ms = _dot(rr_hi, bd_ref[...]) + _dot(rr_lo, bd_ref[...])
    rg = rg_ref[...]
    yr = r * lax.rsqrt(ms + EPS) * (rg * _sigmoid(rg))
    mixed = (gates[:, :D_MODEL] * _dot(f.astype(BF16), wbf_ref[...])
             + gates[:, D_MODEL:2 * D_MODEL] * _dot(yr.astype(BF16), wbr_ref[...])
             + gates[:, 2 * D_MODEL:] * _dot(d.astype(BF16), wbd_ref[...]))
    y = _dot(mixed.astype(BF16), wo_ref[...])
    o_ref[...] = x + mod_ref[0, 5:6, :] * y


def _merge_call(x, mods, g, f_lat, f_tail, o_f, o_b, rg, d_lat, d_tail, wgt, wbf, wbr, wbd, wo,
                n_lat_tiles, n_tiles):
    tm = TOKEN_TILE
    bd = np.kron(np.eye(RET_HEADS, dtype=np.float32), np.full((RET_V, RET_V), 1.0 / RET_V, np.float32))

    def lat_spec(w):
        return pl.BlockSpec((tm, w), lambda i: (jnp.minimum(i, n_lat_tiles - 1), 0))

    return pl.pallas_call(
        functools.partial(_merge_kernel, n_lat_tiles=n_lat_tiles),
        out_shape=jax.ShapeDtypeStruct((n_tiles * tm, D_MODEL), F32),
        grid=(n_tiles,),
        in_specs=[
            _row_spec(D_MODEL), _mod_spec(n_lat_tiles), _const_spec((1, D_MODEL)),
            lat_spec(F_W), _const_spec((tm, F_W)),
            _row_spec(RET_VW), _row_spec(RET_VW), _row_spec(RET_VW),
            pl.BlockSpec((DIFF_VW, tm), lambda i: (0, jnp.minimum(i, n_lat_tiles - 1))),
            _const_spec((DIFF_VW, tm)),
            _const_spec((D_MODEL, GATE_W)), _const_spec((F_W, D_MODEL)), _const_spec((RET_VW, D_MODEL)),
            _const_spec((DIFF_VW, D_MODEL)), _const_spec((D_MODEL, D_MODEL)), _const_spec((RET_VW, RET_VW)),
        ],
        out_specs=_row_spec(D_MODEL),
        compiler_params=_params(("arbitrary",)),
        name="merge",
    )(x, mods, g.reshape(1, D_MODEL), f_lat, f_tail, o_f, o_b, rg, d_lat, d_tail, wgt, wbf, wbr, wbd, wo,
      jnp.asarray(bd, BF16))


def _pick_tile(n, candidates):
    for c in candidates:
        if n % c == 0:
            return c
    raise ValueError(f"no tile for {n}")


def kernel(x, c, ctx, c_ctx, w_ada, b_ada, norm_g, ffn_w1, ffn_w3, ffn_w2, w_in, ret_decay_logit,
           diff_lambda, diff_subln, w_branch_f, w_branch_r, w_branch_d, w_out, final_g):
    batch, seq, d = x.shape
    ctx_len = ctx.shape[1]
    tm = TOKEN_TILE
    assert batch == 1 and d == D_MODEL
    assert seq % max(DFT_N1 * 8, tm) == 0 and ctx_len % 256 == 0 and ctx_len <= tm
    total = seq + ctx_len
    n_lat_tiles = seq // tm
    n_tiles = n_lat_tiles + 1
    n_rows = n_tiles * tm
    n_lat_chunks = seq // RET_CHUNK
    n_ctx_chunks = ctx_len // RET_CHUNK
    n_pad_chunks = (n_rows - total) // RET_CHUNK

    cc = jnp.zeros((8, D_MODEL), F32).at[0].set(c[0]).at[1].set(c_ctx)
    mods_all = _ada_call(cc, w_ada, b_ada)[:, :2].reshape(DEPTH, 2, N_MOD, D_MODEL)

    tables = _rope_tables(seq, n_rows)
    twc, tws = _twiddles(seq)
    log_g2_all = jax.nn.log_sigmoid(ret_decay_logit.astype(F32))
    lv = diff_lambda.astype(F32)
    w_aug_all, wgt_all = _prep_proj_weights(w_in)
    w1_all, w3_all, w2_all = ffn_w1.astype(BF16), ffn_w3.astype(BF16), ffn_w2.astype(BF16)
    wbf_all, wbr_all = w_branch_f.astype(BF16), w_branch_r.astype(BF16)
    wbd_all, wo_all = w_branch_d.astype(BF16), w_out.astype(BF16)

    tq = _pick_tile(seq, (512, 256, 128))
    tk = _pick_tile(total, (3328, 1280, 640, 256, 128))
    tail_pad = ((0, tm - ctx_len), (0, 0))

    xs = jnp.concatenate([x[0], ctx[0], jnp.zeros((n_rows - total, D_MODEL), F32)], axis=0)
    for l in range(DEPTH):
        last = l == DEPTH - 1
        lam_init = 0.8 - 0.6 * math.exp(-0.3 * l)
        mods = mods_all[l]
        lam = (jnp.exp(jnp.sum(lv[l, 0] * lv[l, 1])) - jnp.exp(jnp.sum(lv[l, 2] * lv[l, 3]))
               + lam_init).reshape(1)

        xs = _ffn_call(xs, mods, norm_g[l, 0], w1_all[l, 0], w3_all[l, 0], w2_all[l, 0], 0,
                       n_lat_tiles, n_tiles)

        uf, rq, rkt, rv, rg, dqt, dk, vt_aug = _proj_call(xs, mods, norm_g[l, 1], w_aug_all[l], tables,
                                                           n_lat_tiles, n_tiles)

        f_lat = _fourier_latent(uf, seq, twc, tws)
        o_f, o_b = _retention_call(rq, rkt, rv, log_g2_all[l], n_lat_chunks, n_ctx_chunks, n_pad_chunks)
        d_lat = _attn_call(dqt, dk, vt_aug, lam, diff_subln[l], lam_init, seq, tq, tk, total // tk)

        if last:
            f_tail = jnp.zeros((tm, F_W), F32)
            d_tail = jnp.zeros((DIFF_VW, tm), F32)
            n_out = n_lat_tiles
        else:
            f_tail = jnp.pad(_fourier_ctx(uf[seq:total]), tail_pad)
            d_ctx = _attn_call(dqt, dk, vt_aug, lam, diff_subln[l], lam_init, ctx_len, ctx_len, ctx_len, 1,
                               q_off=seq // ctx_len, k_off=seq // ctx_len)
            d_tail = jnp.pad(d_ctx, tail_pad[::-1])
            n_out = n_tiles

        xs = _merge_call(xs, mods, norm_g[l, 1], f_lat, f_tail, o_f, o_b, rg, d_lat, d_tail, wgt_all[l],
                         wbf_all[l], wbr_all[l], wbd_all[l], wo_all[l], n_lat_tiles, n_out)

        xs = _ffn_call(xs, mods, norm_g[l, 2], w1_all[l, 1], w3_all[l, 1], w2_all[l, 1], 6,
                       n_lat_tiles, n_out, final_g=final_g if last else None)

    return xs.reshape(1, seq, D_MODEL)
```

```python
import functools
import math

import numpy as np
import jax
import jax.numpy as jnp
from jax import lax
from jax.experimental import pallas as pl
from jax.experimental.pallas import tpu as pltpu

D_MODEL = 1024
DEPTH = 4
GRID_W = 64
D_FF = 2816
N_MOD = 9
FOURIER_GROUPS = 4
FOURIER_CH = 64
RET_HEADS = 6
RET_QK = 32
RET_V = 64
RET_CHUNK = 128
DIFF_HEADS = 6
DIFF_QK = 32
DIFF_V = 64
ROPE_BASE = 10000.0
EPS = 1e-6
F_W = FOURIER_GROUPS * FOURIER_CH
RET_QW = RET_HEADS * RET_QK
RET_VW = RET_HEADS * RET_V
DIFF_QW = DIFF_HEADS * 2 * DIFF_QK
DIFF_VW = DIFF_HEADS * DIFF_V
GATE_W = 3 * D_MODEL

LANES = 128
VMEM_LIMIT_BYTES = 56 * 1024 * 1024

RET_QP = 2 * LANES
RET_STEP_CHUNKS = 2
TOKEN_TILE = 512
DFT_N1 = 128
ATTN_KEY_CHUNK = 256
ATTN_VT_ROWS = DIFF_V + 16

BF16 = jnp.bfloat16
F32 = jnp.float32
LOG2E = math.log2(math.e)


def _dot(a, b):
    return jnp.dot(a, b, preferred_element_type=F32)


def _split_bf16(x):
    hi = x.astype(BF16)
    lo = (x - hi.astype(F32)).astype(BF16)
    return hi, lo


def _dot3_split(a, b):
    (ah, al), (bh, bl) = a, b
    return _dot(ah, bh) + _dot(al, bh) + _dot(ah, bl)


def _dot3(a, b):
    return _dot3_split(_split_bf16(a), _split_bf16(b))


def _norm_mod(x, g, shift, scale):
    ms = jnp.mean(x * x, axis=-1, keepdims=True)
    y = x * lax.rsqrt(ms + EPS) * g
    return y * (1.0 + scale) + shift


def _sigmoid(x):
    return 1.0 / (1.0 + jnp.exp(-x))


def _const_spec(shape):
    nd = len(shape)
    return pl.BlockSpec(shape, lambda *_: (0,) * nd, pipeline_mode=pl.Buffered(1))


def _stacked_spec(shape, lead):
    block = (None,) * len(lead) + tuple(shape)
    index = tuple(lead) + (0,) * len(shape)
    return pl.BlockSpec(block, lambda *_: index, pipeline_mode=pl.Buffered(1))


def _params(sem):
    return pltpu.CompilerParams(dimension_semantics=sem, vmem_limit_bytes=VMEM_LIMIT_BYTES)


def _row_spec(width):
    return pl.BlockSpec((TOKEN_TILE, width), lambda i: (i, 0))


def _ada_kernel(cc_ref, w_ref, b_ref, o_ref):
    cc = cc_ref[...]
    s = cc * _sigmoid(cc)
    o_ref[0] = _dot3(s, w_ref[0]) + b_ref[0]


def _ada_call(cc, w_ada, b_ada):
    depth, d, n = w_ada.shape
    tn = 1152
    return pl.pallas_call(
        _ada_kernel,
        out_shape=jax.ShapeDtypeStruct((depth, 8, n), F32),
        grid=(depth, n // tn),
        in_specs=[
            pl.BlockSpec((8, d), lambda l, j: (0, 0)),
            pl.BlockSpec((1, d, tn), lambda l, j: (l, 0, j)),
            pl.BlockSpec((1, 1, tn), lambda l, j: (l, 0, j)),
        ],
        out_specs=pl.BlockSpec((1, 8, tn), lambda l, j: (l, 0, j)),
        compiler_params=_params(("arbitrary", "arbitrary")),
        name="adaln",
    )(cc, w_ada, b_ada.reshape(depth, 1, n))


def _mod_spec(n_lat_tiles):
    return pl.BlockSpec((1, N_MOD, D_MODEL), lambda i: (jnp.where(i >= n_lat_tiles, 1, 0), 0, 0))


def _ffn_tile(x, mod_ref, g_ref, w1_ref, w3_ref, w2_ref, base):
    shift = mod_ref[0, base:base + 1, :]
    scale = mod_ref[0, base + 1:base + 2, :]
    gate = mod_ref[0, base + 2:base + 3, :]
    hb = _norm_mod(x, g_ref[...], shift, scale).astype(BF16)
    a = _dot(hb, w1_ref[...])
    b = _dot(hb, w3_ref[...])
    u = (a * _sigmoid(a) * b).astype(BF16)
    return x + (0.5 * gate) * _dot(u, w2_ref[...])


def _ffn_kernel(x_ref, mod_ref, g_ref, w1_ref, w3_ref, w2_ref, o_ref, *, base):
    o_ref[...] = _ffn_tile(x_ref[...], mod_ref, g_ref, w1_ref, w3_ref, w2_ref, base)


def _ffn_final_kernel(x_ref, mod_ref, g_ref, w1_ref, w3_ref, w2_ref, fg_ref, o_ref, *, base):
    y = _ffn_tile(x_ref[...], mod_ref, g_ref, w1_ref, w3_ref, w2_ref, base)
    ms = jnp.mean(y * y, axis=-1, keepdims=True)
    o_ref[...] = y * lax.rsqrt(ms + EPS) * fg_ref[...]


def _ffn_call(x, mods, g, w1, w3, w2, widx, base, n_lat_tiles, n_tiles, final_g=None):
    in_specs = [
        _row_spec(D_MODEL), _mod_spec(n_lat_tiles), _const_spec((1, D_MODEL)),
        _stacked_spec((D_MODEL, D_FF), widx), _stacked_spec((D_MODEL, D_FF), widx),
        _stacked_spec((D_FF, D_MODEL), widx),
    ]
    args = [x, mods, g.reshape(1, D_MODEL), w1, w3, w2]
    body = _ffn_kernel
    if final_g is not None:
        in_specs.append(_const_spec((1, D_MODEL)))
        args.append(final_g.reshape(1, D_MODEL))
        body = _ffn_final_kernel
    return pl.pallas_call(
        functools.partial(body, base=base),
        out_shape=jax.ShapeDtypeStruct((n_tiles * TOKEN_TILE, D_MODEL), F32),
        grid=(n_tiles,),
        in_specs=in_specs,
        out_specs=_row_spec(D_MODEL),
        compiler_params=_params(("arbitrary",)),
        name="ffn",
    )(*args)


_C_F, _C_RQ, _C_RK, _C_RV, _C_RG = 0, 256, 512, 768, 1152
_C_DQ, _C_DK, _C_DV = 1536, 1920, 2304
_C_RQR, _C_RKR, _C_DQR, _C_DKR = 2688, 2944, 3200, 3584
PROJ_W = 3968


def _rotate_half_cols(w, block):
    depth, d, width = w.shape
    w5 = w.reshape(depth, d, width // block, 2, block // 2)
    return jnp.concatenate([-w5[:, :, :, 1:2], w5[:, :, :, 0:1]], axis=3).reshape(depth, d, width)


def _prep_proj_weights(w_in):
    cuts = np.cumsum([F_W, RET_QW, RET_QW, RET_VW, RET_VW, DIFF_QW, DIFF_QW, DIFF_VW])
    wf, wrq, wrk, wrv, wrg, wdq, wdk, wdv, wgt = jnp.split(w_in, cuts, axis=2)
    z = jnp.zeros(w_in.shape[:2] + (RET_QP - RET_QW,), w_in.dtype)
    parts = [wf, wrq, z, wrk, z, wrv, wrg, wdq, wdk, wdv,
             _rotate_half_cols(wrq, RET_QK), z, _rotate_half_cols(wrk, RET_QK), z,
             _rotate_half_cols(wdq, DIFF_QK // 2), _rotate_half_cols(wdk, DIFF_QK // 2)]
    return jnp.concatenate(parts, axis=2).astype(BF16), wgt.astype(BF16)


def _rope_tables(seq, n_rows):
    pos = jnp.arange(seq, dtype=F32)
    inv_r = ROPE_BASE ** (-jnp.arange(0, RET_QK, 2, dtype=F32) / RET_QK)
    ang_r = pos[:, None] * inv_r[None, :]
    cos_r = jnp.tile(jnp.cos(ang_r), (1, 2 * LANES // RET_QK))
    sin_r = jnp.tile(jnp.sin(ang_r), (1, 2 * LANES // RET_QK))
    rows = jnp.repeat(jnp.arange(seq // GRID_W, dtype=F32), GRID_W)
    cols = jnp.tile(jnp.arange(GRID_W, dtype=F32), seq // GRID_W)
    dim = DIFF_QK // 2
    inv_d = ROPE_BASE ** (-jnp.arange(0, dim, 2, dtype=F32) / dim)
    a_row = rows[:, None] * inv_d[None, :]
    a_col = cols[:, None] * inv_d[None, :]
    cos_hm = jnp.concatenate([jnp.cos(a_row)] * 2 + [jnp.cos(a_col)] * 2, axis=1)
    sin_hm = jnp.concatenate([jnp.sin(a_row)] * 2 + [jnp.sin(a_col)] * 2, axis=1)
    cos_d = jnp.tile(cos_hm, (1, LANES // DIFF_QK))
    sin_d = jnp.tile(sin_hm, (1, LANES // DIFF_QK))

    def finish(t, fill):
        return jnp.pad(t, ((0, n_rows - seq), (0, 0)), constant_values=fill)

    return finish(cos_r, 1.0), finish(sin_r, 0.0), finish(cos_d, 1.0), finish(sin_d, 0.0)


def _proj_kernel(x_ref, mod_ref, g_ref, w_ref, cr_ref, sr_ref, cd_ref, sd_ref,
                 uf_ref, rq_ref, rkt_ref, rv_ref, rg_ref, dqt_ref, dk_ref, vta_ref):
    x = x_ref[...]
    hb = _norm_mod(x, g_ref[...], mod_ref[0, 3:4, :], mod_ref[0, 4:5, :]).astype(BF16)
    p = _dot(hb, w_ref[...])
    cr = jnp.concatenate([cr_ref[...]] * (RET_QP // LANES), axis=1)
    sr = jnp.concatenate([sr_ref[...]] * (RET_QP // LANES), axis=1)
    cd = jnp.concatenate([cd_ref[...]] * (DIFF_QW // LANES), axis=1)
    sd = jnp.concatenate([sd_ref[...]] * (DIFF_QW // LANES), axis=1)
    uf_ref[...] = p[:, _C_F:_C_F + F_W]
    rq = p[:, _C_RQ:_C_RQ + RET_QP] * cr + p[:, _C_RQR:_C_RQR + RET_QP] * sr
    rk = p[:, _C_RK:_C_RK + RET_QP] * cr + p[:, _C_RKR:_C_RKR + RET_QP] * sr
    rq_ref[...] = rq.astype(BF16)
    rkt_ref[...] = (rk * (RET_QK ** -0.5)).T.astype(BF16)
    rv_ref[...] = p[:, _C_RV:_C_RV + RET_VW].astype(BF16)
    rg_ref[...] = p[:, _C_RG:_C_RG + RET_VW]
    dq = p[:, _C_DQ:_C_DQ + DIFF_QW] * cd + p[:, _C_DQR:_C_DQR + DIFF_QW] * sd
    dk = p[:, _C_DK:_C_DK + DIFF_QW] * cd + p[:, _C_DKR:_C_DKR + DIFF_QW] * sd
    dqt_ref[...] = (dq * ((DIFF_QK ** -0.5) * LOG2E)).T.astype(BF16)
    dk_ref[...] = dk.astype(BF16)
    vt = p[:, _C_DV:_C_DV + DIFF_VW].T
    ones = jnp.ones((ATTN_VT_ROWS - DIFF_V, x.shape[0]), F32)
    for h in range(DIFF_HEADS):
        vta_ref[h] = jnp.concatenate([vt[h * DIFF_V:(h + 1) * DIFF_V], ones], axis=0).astype(BF16)


def _proj_call(x, mods, g, w_aug, layer, tables, n_lat_tiles, n_tiles):
    tm = TOKEN_TILE
    t = n_tiles * tm
    out_shape = (
        jax.ShapeDtypeStruct((t, F_W), F32),
        jax.ShapeDtypeStruct((t, RET_QP), BF16),
        jax.ShapeDtypeStruct((RET_QP, t), BF16),
        jax.ShapeDtypeStruct((t, RET_VW), BF16),
        jax.ShapeDtypeStruct((t, RET_VW), F32),
        jax.ShapeDtypeStruct((DIFF_QW, t), BF16),
        jax.ShapeDtypeStruct((t, DIFF_QW), BF16),
        jax.ShapeDtypeStruct((DIFF_HEADS, ATTN_VT_ROWS, t), BF16),
    )
    out_specs = (
        _row_spec(F_W), _row_spec(RET_QP),
        pl.BlockSpec((RET_QP, tm), lambda i: (0, i)),
        _row_spec(RET_VW), _row_spec(RET_VW),
        pl.BlockSpec((DIFF_QW, tm), lambda i: (0, i)),
        _row_spec(DIFF_QW),
        pl.BlockSpec((DIFF_HEADS, ATTN_VT_ROWS, tm), lambda i: (0, 0, i)),
    )
    return pl.pallas_call(
        _proj_kernel,
        out_shape=out_shape,
        grid=(n_tiles,),
        in_specs=[
            _row_spec(D_MODEL), _mod_spec(n_lat_tiles), _const_spec((1, D_MODEL)),
            _stacked_spec((D_MODEL, PROJ_W), (layer,)),
            _row_spec(LANES), _row_spec(LANES), _row_spec(LANES), _row_spec(LANES),
        ],
        out_specs=out_specs,
        compiler_params=_params(("arbitrary",)),
        name="mixer_proj",
    )(x, mods, g.reshape(1, D_MODEL), w_aug, *tables)


def _dft_mats(n):
    k = np.arange(n)
    ang = 2.0 * np.pi * ((k[:, None] * k[None, :]) % n) / n
    return np.cos(ang).astype(np.float32), np.sin(ang).astype(np.float32)


def _channel_dft_mats():
    c, s = _dft_mats(FOURIER_CH)
    eye = np.eye(FOURIER_GROUPS, dtype=np.float32)
    return np.kron(eye, c), np.kron(eye, s)


def _fourier_stage1_kernel(x_ref, cc_ref, sc_ref, c1_ref, s1_ref, twc_ref, tws_ref, tr_ref, ti_ref, *, nb):
    cc, sc = _split_bf16(cc_ref[...]), _split_bf16(sc_ref[...])
    zr, zi = [], []
    for j in range(nb):
        u = _split_bf16(x_ref[:, j * F_W:(j + 1) * F_W])
        zr.append(_dot3_split(u, cc))
        zi.append(-_dot3_split(u, sc))
    zr = _split_bf16(jnp.concatenate(zr, axis=1) if nb > 1 else zr[0])
    zi = _split_bf16(jnp.concatenate(zi, axis=1) if nb > 1 else zi[0])
    c1, s1 = _split_bf16(c1_ref[...]), _split_bf16(s1_ref[...])
    tr = _dot3_split(c1, zr) + _dot3_split(s1, zi)
    ti = _dot3_split(c1, zi) - _dot3_split(s1, zr)
    twc, tws = twc_ref[0], tws_ref[0]
    for j in range(nb):
        cols = slice(j * F_W, (j + 1) * F_W)
        c, s = twc[:, j:j + 1], tws[:, j:j + 1]
        tr_ref[:, cols] = tr[:, cols] * c + ti[:, cols] * s
        ti_ref[:, cols] = ti[:, cols] * c - tr[:, cols] * s


def _fourier_stage2_kernel(tr_ref, ti_ref, c2_ref, s2_ref, o_ref, *, kb):
    c2, s2 = c2_ref[...], s2_ref[...]
    for j in range(kb):
        o_ref[:, j, :] = _dot3(c2, tr_ref[j]) + _dot3(s2, ti_ref[j])


def _fourier_latent(u_all, seq, twc, tws):
    n1, n2 = DFT_N1, seq // DFT_N1
    nb = _fourier_block(n2)
    kb = 8
    cc, sc = _channel_dft_mats()
    c1, s1 = _dft_mats(n1)
    c2, s2 = _dft_mats(n2)
    x2 = u_all[:seq].reshape(n1, n2 * F_W)
    blk = pl.BlockSpec((n1, nb * F_W), lambda i: (0, i))
    tw_blk = pl.BlockSpec((1, n1, nb), lambda i: (i, 0, 0))
    tr, ti = pl.pallas_call(
        functools.partial(_fourier_stage1_kernel, nb=nb),
        out_shape=(jax.ShapeDtypeStruct((n1, n2 * F_W), F32),) * 2,
        grid=(n2 // nb,),
        in_specs=[blk, _const_spec((F_W, F_W)), _const_spec((F_W, F_W)),
                  _const_spec((n1, n1)), _const_spec((n1, n1)), tw_blk, tw_blk],
        out_specs=(blk, blk),
        compiler_params=_params(("arbitrary",)),
        name="fourier_stage1",
    )(x2, cc, sc, c1, s1, twc, tws)
    tr3 = tr.reshape(n1, n2, F_W)
    ti3 = ti.reshape(n1, n2, F_W)
    o3 = pl.pallas_call(
        functools.partial(_fourier_stage2_kernel, kb=kb),
        out_shape=jax.ShapeDtypeStruct((n2, n1, F_W), F32),
        grid=(n1 // kb,),
        in_specs=[pl.BlockSpec((kb, n2, F_W), lambda i: (i, 0, 0))] * 2
        + [_const_spec((n2, n2)), _const_spec((n2, n2))],
        out_specs=pl.BlockSpec((n2, kb, F_W), lambda i: (0, i, 0)),
        compiler_params=_params(("arbitrary",)),
        name="fourier_stage2",
    )(tr3, ti3, c2, s2)
    return o3.reshape(seq, F_W)


def _twiddles(seq):
    n1, n2 = DFT_N1, seq // DFT_N1
    k1 = jnp.arange(n1, dtype=jnp.int32)[:, None]
    m2 = jnp.arange(n2, dtype=jnp.int32)[None, :]
    ang = (2.0 * math.pi / seq) * ((k1 * m2) % seq).astype(F32)
    scale = 1.0 / math.sqrt(seq * FOURIER_CH)
    nb = _fourier_block(n2)

    def blocked(t):
        return t.reshape(n1, n2 // nb, nb).transpose(1, 0, 2)

    return blocked(jnp.cos(ang) * scale), blocked(jnp.sin(ang) * scale)


def _fourier_block(n2):
    return min(8, n2)


def _fourier_ctx_kernel(u_ref, cc_ref, sc_ref, cl_ref, sl_ref, o_ref, *, scale):
    u = u_ref[...]
    a = _dot3(u, cc_ref[...])
    b = _dot3(u, sc_ref[...])
    o_ref[...] = (_dot3(cl_ref[...], a) - _dot3(sl_ref[...], b)) * scale


def _fourier_ctx(u):
    n = u.shape[0]
    cc, sc = _channel_dft_mats()
    cl, sl = _dft_mats(n)
    return pl.pallas_call(
        functools.partial(_fourier_ctx_kernel, scale=1.0 / math.sqrt(n * FOURIER_CH)),
        out_shape=jax.ShapeDtypeStruct((n, F_W), F32),
        name="fourier_ctx",
    )(u, cc, sc, cl, sl)


def _ret_chunk_local(q, kt, v, d_ref, kdec, bdmask, hmask_ref, vmask_ref):
    pieces = []
    vparts = []
    for h in range(RET_HEADS):
        qh = q * hmask_ref[h]
        pieces.append((_dot(qh, kt) * d_ref[h]).astype(BF16))
        vparts.append(v * vmask_ref[h])
    inner = jnp.concatenate(pieces, axis=1)
    vbd = jnp.concatenate(vparts, axis=0)
    kd = (kt.astype(F32) * kdec).astype(BF16)
    return _dot(inner, vbd), bdmask * _dot(kd, v)


def _ret_dir(q_ref, kt_ref, v_ref, o_ref, s_ref, d_ref, qdec, kdec, cd, bdmask, hmask_ref, vmask_ref, order):
    c = RET_CHUNK
    local = []
    for half in order:
        rows = slice(half * c, (half + 1) * c)
        q = q_ref[rows, :]
        local.append((rows, q) + _ret_chunk_local(q, kt_ref[:, rows], v_ref[rows, :], d_ref, kdec, bdmask,
                                                  hmask_ref, vmask_ref))
    s = s_ref[...]
    for rows, q, o_intra, inc in local:
        o_ref[rows, :] = o_intra + _dot(q, s.astype(BF16)) * qdec
        s = s * cd + inc
    s_ref[...] = s


def _ret_kernel(logg_ref, lgv_ref, lgk_ref, bdmask_ref, hmask_ref, vmask_ref,
                qf_ref, ktf_ref, vf_ref, qb_ref, ktb_ref, vb_ref,
                of_ref, ob_ref,
                sf_ref, sb_ref, df_ref, db_ref, qdf_ref, qdb_ref, kdf_ref, kdb_ref, cdf_ref, cdb_ref):
    c = RET_CHUNK

    @pl.when(pl.program_id(0) == 0)
    def _():
        sf_ref[...] = jnp.zeros_like(sf_ref)
        sb_ref[...] = jnp.zeros_like(sb_ref)
        ii = lax.broadcasted_iota(jnp.int32, (c, c), 0).astype(F32)
        jj = lax.broadcasted_iota(jnp.int32, (c, c), 1).astype(F32)
        for h in range(RET_HEADS):
            df_ref[h] = jnp.where(ii >= jj, jnp.exp(logg_ref[0, h] * jnp.maximum(ii - jj, 0.0)), 0.0)
            db_ref[h] = jnp.where(jj >= ii, jnp.exp(logg_ref[1, h] * jnp.maximum(jj - ii, 0.0)), 0.0)
        ri = lax.broadcasted_iota(jnp.int32, (c, RET_VW), 0).astype(F32)
        qdf_ref[...] = jnp.exp(lgv_ref[0] * (ri + 1.0))
        qdb_ref[...] = jnp.exp(lgv_ref[1] * (c - ri))
        cj = lax.broadcasted_iota(jnp.int32, (RET_QP, c), 1).astype(F32)
        kdf_ref[...] = jnp.exp(lgk_ref[0] * (c - 1.0 - cj))
        kdb_ref[...] = jnp.exp(lgk_ref[1] * cj)
        cdf_ref[...] = jnp.exp(lgv_ref[0] * float(c))
        cdb_ref[...] = jnp.exp(lgv_ref[1] * float(c))

    bdmask = bdmask_ref[...]
    halves = list(range(RET_STEP_CHUNKS))
    _ret_dir(qf_ref, ktf_ref, vf_ref, of_ref, sf_ref, df_ref,
             qdf_ref[...], kdf_ref[...], cdf_ref[...], bdmask, hmask_ref, vmask_ref, halves)
    _ret_dir(qb_ref, ktb_ref, vb_ref, ob_ref, sb_ref, db_ref,
             qdb_ref[...], kdb_ref[...], cdb_ref[...], bdmask, hmask_ref, vmask_ref, halves[::-1])


def _ret_masks():
    hm = np.zeros((RET_HEADS, 1, RET_QP), np.float32)
    vm = np.zeros((RET_HEADS, 1, RET_VW), np.float32)
    bd = np.zeros((RET_QP, RET_VW), np.float32)
    for h in range(RET_HEADS):
        hm[h, 0, h * RET_QK:(h + 1) * RET_QK] = 1.0
        vm[h, 0, h * RET_V:(h + 1) * RET_V] = 1.0
        bd[h * RET_QK:(h + 1) * RET_QK, h * RET_V:(h + 1) * RET_V] = 1.0
    return jnp.asarray(bd), jnp.asarray(hm, BF16), jnp.asarray(vm, BF16)


def _retention_call(rq, rkt, rv, log_g2, n_lat, n_ctx, n_pad):
    c = RET_CHUNK
    g = RET_STEP_CHUNKS
    assert n_lat % g == 0 and n_ctx % g == 0 and n_pad % g == 0
    n_lat, n_ctx, n_pad = n_lat // g, n_ctx // g, n_pad // g
    n_real = n_lat + n_ctx
    n = n_real + n_pad
    rows = g * c

    def fwd(i):
        return jnp.where(i < n_ctx, n_lat + i, jnp.where(i < n_real, i - n_ctx, i))

    def bwd(i):
        return jnp.where(i < n_real, n_real - 1 - i, i)

    lgv = jnp.repeat(log_g2, RET_V, axis=1).reshape(2, 1, RET_VW)
    lgk = jnp.pad(jnp.repeat(log_g2, RET_QK, axis=1), ((0, 0), (0, RET_QP - RET_QW)))
    lgk = jnp.broadcast_to(lgk[:, :, None], (2, RET_QP, c))
    bd, hm, vm = _ret_masks()

    def specs(ix):
        return [pl.BlockSpec((rows, RET_QP), lambda i: (ix(i), 0)),
                pl.BlockSpec((RET_QP, rows), lambda i: (0, ix(i))),
                pl.BlockSpec((rows, RET_VW), lambda i: (ix(i), 0))]

    vmem = pltpu.VMEM
    return pl.pallas_call(
        _ret_kernel,
        out_shape=(jax.ShapeDtypeStruct((n * rows, RET_VW), F32),) * 2,
        grid=(n,),
        in_specs=[pl.BlockSpec(memory_space=pltpu.SMEM),
                  _const_spec((2, 1, RET_VW)), _const_spec((2, RET_QP, c)),
                  _const_spec((RET_QP, RET_VW)), _const_spec((RET_HEADS, 1, RET_QP)),
                  _const_spec((RET_HEADS, 1, RET_VW))] + specs(fwd) + specs(bwd),
        out_specs=(pl.BlockSpec((rows, RET_VW), lambda i: (fwd(i), 0)),
                   pl.BlockSpec((rows, RET_VW), lambda i: (bwd(i), 0))),
        scratch_shapes=[vmem((RET_QP, RET_VW), F32), vmem((RET_QP, RET_VW), F32),
                        vmem((RET_HEADS, c, c), F32), vmem((RET_HEADS, c, c), F32),
                        vmem((c, RET_VW), F32), vmem((c, RET_VW), F32),
                        vmem((RET_QP, c), F32), vmem((RET_QP, c), F32),
                        vmem((1, RET_VW), F32), vmem((1, RET_VW), F32)],
        compiler_params=_params(("arbitrary",)),
        name="retention",
    )(log_g2, lgv, lgk, bd, hm, vm, rq, rkt, rv, rq, rkt, rv)


def _chunk_scores(k_chunk, qm, s_out_ref, rows, run_max):
    s = _dot(k_chunk, qm)
    s_out_ref[rows, :] = s
    cm = jnp.max(s.reshape(s.shape[0] // 8, 8, s.shape[1]), axis=0)
    return cm if run_max is None else jnp.maximum(run_max, cm)


def _mask_map(qt, j):
    row = lax.broadcasted_iota(jnp.int32, qt.shape, 0)
    lo = j * DIFF_QK
    return jnp.where((row >= lo) & (row < lo + DIFF_QK), qt, jnp.zeros_like(qt))


def _attn_kernel(lam_ref, qt_ref, qtn_ref, k0_ref, kn_ref, vt_ref, subln_ref, o_ref,
                 qm_ref, m_ref, acc_ref, s_ref, mx_ref, kcur_ref, *, out_scale, key_chunk):
    ki = pl.program_id(2)
    is_last = ki == pl.num_programs(2) - 1
    tk = kn_ref.shape[0]
    n_chunks = tk // key_chunk

    def chunk(c):
        return slice(c * key_chunk, (c + 1) * key_chunk)

    @pl.when(ki == 0)
    def _():
        qt = qt_ref[...]
        for j in range(4):
            qm_ref[j] = _mask_map(qt, j)
        m_ref[...] = jnp.full(m_ref.shape, -jnp.inf, F32)
        acc_ref[...] = jnp.zeros_like(acc_ref)

    @pl.when((ki == 0) & (pl.program_id(0) == 0) & (pl.program_id(1) == 0))
    def _():
        kcur_ref[...] = k0_ref[...]
        run = None
        for c in range(n_chunks):
            run = _chunk_scores(k0_ref[chunk(c), :], qm_ref[0], s_ref.at[0], chunk(c), run)
        mx_ref[...] = run

    q_ahead = jnp.where(is_last, _mask_map(qtn_ref[...], 0), qm_ref[0])

    run = mx_ref[...]
    for u in range(4):
        cur, nxt = u % 2, (u + 1) % 2
        m_prev = m_ref[u]
        m_new = jnp.maximum(m_prev, jnp.max(run, axis=0, keepdims=True))
        alpha = jnp.exp2(m_prev - m_new)
        vt = vt_ref[u // 2]
        run = None
        acc = None
        p_prev = None
        for c in range(n_chunks):
            if u < 3:
                run = _chunk_scores(kcur_ref[chunk(c), :], qm_ref[u + 1], s_ref.at[nxt], chunk(c), run)
            else:
                run = _chunk_scores(kn_ref[chunk(c), :], q_ahead, s_ref.at[nxt], chunk(c), run)
            p = jnp.exp2(s_ref[cur, chunk(c), :] - m_new).astype(BF16)
            if p_prev is not None:
                pv = _dot(vt[:, chunk(c - 1)], p_prev)
                acc = pv if acc is None else acc + pv
            p_prev = p
        pv = _dot(vt[:, chunk(n_chunks - 1)], p_prev)
        acc = pv if acc is None else acc + pv
        acc_ref[u] = alpha * acc_ref[u] + acc
        m_ref[u] = m_new
    mx_ref[...] = run
    kcur_ref[...] = kn_ref[...]

    @pl.when(is_last)
    def _():
        lam = lam_ref[0]
        outs = []
        for hl in range(2):
            a1 = acc_ref[2 * hl]
            a2 = acc_ref[2 * hl + 1]
            o = (a1[:DIFF_V] / a1[DIFF_V:DIFF_V + 1]
                 - lam * (a2[:DIFF_V] / a2[DIFF_V:DIFF_V + 1]))
            ms = jnp.mean(o * o, axis=0, keepdims=True)
            outs.append(o * lax.rsqrt(ms + EPS) * subln_ref[...] * out_scale)
        o_ref[...] = jnp.concatenate(outs, axis=0)


def _attn_call(dqt, dk, vt_aug, lam, subln, lam_init, n_q, tq, tk, nk, q_off=0, k_off=0):
    n_hp = DIFF_HEADS // 2
    n_i = n_q // tq

    def next_pair(h, i):
        wrap = i + 1 >= n_i
        return jnp.where(wrap, jnp.minimum(h + 1, n_hp - 1), h), jnp.where(wrap, 0, i + 1)

    def q_next_map(h, i, k):
        hn, i_n = next_pair(h, i)
        return hn, q_off + i_n

    def k_next_map(h, i, k):
        last = k + 1 >= nk
        return k_off + jnp.where(last, 0, k + 1), jnp.where(last, next_pair(h, i)[0], h)

    return pl.pallas_call(
        functools.partial(_attn_kernel, out_scale=1.0 - lam_init, key_chunk=min(tk, ATTN_KEY_CHUNK)),
        out_shape=jax.ShapeDtypeStruct((DIFF_VW, n_q), F32),
        grid=(n_hp, n_i, nk),
        in_specs=[
            pl.BlockSpec(memory_space=pltpu.SMEM),
            pl.BlockSpec((LANES, tq), lambda h, i, k: (h, q_off + i)),
            pl.BlockSpec((LANES, tq), q_next_map),
            pl.BlockSpec((tk, LANES), lambda h, i, k: (k_off, 0)),
            pl.BlockSpec((tk, LANES), k_next_map),
            pl.BlockSpec((2, ATTN_VT_ROWS, tk), lambda h, i, k: (h, 0, k_off + k)),
            pl.BlockSpec((DIFF_V, 1), lambda h, i, k: (0, 0)),
        ],
        out_specs=pl.BlockSpec((LANES, tq), lambda h, i, k: (h, i)),
        scratch_shapes=[pltpu.VMEM((4, LANES, tq), BF16),
                        pltpu.VMEM((4, 1, tq), F32),
                        pltpu.VMEM((4, ATTN_VT_ROWS, tq), F32),
                        pltpu.VMEM((2, tk, tq), F32),
                        pltpu.VMEM((8, tq), F32),
                        pltpu.VMEM((tk, LANES), BF16)],
        compiler_params=_params(("arbitrary", "arbitrary", "arbitrary")),
        name="diff_attn",
    )(lam, dqt, dqt, dk, dk, vt_aug, subln.reshape(DIFF_V, 1))


def _merge_kernel(x_ref, mod_ref, g_ref, flat_ref, ftail_ref, of_ref, ob_ref, rg_ref, dlat_ref, dtail_ref,
                  wgt_ref, wbf_ref, wbr_ref, wbd_ref, wo_ref, bd_ref, o_ref, *, n_lat_tiles):
    x = x_ref[...]
    hb = _norm_mod(x, g_ref[...], mod_ref[0, 3:4, :], mod_ref[0, 4:5, :]).astype(BF16)
    gates = _sigmoid(_dot(hb, wgt_ref[...]))
    is_tail = pl.program_id(0) >= n_lat_tiles
    f = jnp.where(is_tail, ftail_ref[...], flat_ref[...])
    d = jnp.where(is_tail, dtail_ref[...], dlat_ref[...]).T
    r = of_ref[...] + ob_ref[...]
    rr_hi, rr_lo = _split_bf16(r * r)
    ms = _dot(rr_hi, bd_ref[...]) + _dot(rr_lo, bd_ref[...])
    rg = rg_ref[...]
    yr = r * lax.rsqrt(ms + EPS) * (rg * _sigmoid(rg))
    mixed = (gates[:, :D_MODEL] * _dot(f.astype(BF16), wbf_ref[...])
             + gates[:, D_MODEL:2 * D_MODEL] * _dot(yr.astype(BF16), wbr_ref[...])
             + gates[:, 2 * D_MODEL:] * _dot(d.astype(BF16), wbd_ref[...]))
    y = _dot(mixed.astype(BF16), wo_ref[...])
    o_ref[...] = x + mod_ref[0, 5:6, :] * y


def _merge_call(x, mods, g, f_lat, f_tail, o_f, o_b, rg, d_lat, d_tail, wgt, wbf, wbr, wbd, wo, layer,
                n_lat_tiles, n_tiles):
    tm = TOKEN_TILE
    bd = np.kron(np.eye(RET_HEADS, dtype=np.float32), np.full((RET_V, RET_V), 1.0 / RET_V, np.float32))

    def lat_spec(w):
        return pl.BlockSpec((tm, w), lambda i: (jnp.minimum(i, n_lat_tiles - 1), 0))

    return pl.pallas_call(
        functools.partial(_merge_kernel, n_lat_tiles=n_lat_tiles),
        out_shape=jax.ShapeDtypeStruct((n_tiles * tm, D_MODEL), F32),
        grid=(n_tiles,),
        in_specs=[
            _row_spec(D_MODEL), _mod_spec(n_lat_tiles), _const_spec((1, D_MODEL)),
            lat_spec(F_W), _const_spec((tm, F_W)),
            _row_spec(RET_VW), _row_spec(RET_VW), _row_spec(RET_VW),
            pl.BlockSpec((DIFF_VW, tm), lambda i: (0, jnp.minimum(i, n_lat_tiles - 1))),
            _const_spec((DIFF_VW, tm)),
            _stacked_spec((D_MODEL, GATE_W), (layer,)), _stacked_spec((F_W, D_MODEL), (layer,)),
            _stacked_spec((RET_VW, D_MODEL), (layer,)), _stacked_spec((DIFF_VW, D_MODEL), (layer,)),
            _stacked_spec((D_MODEL, D_MODEL), (layer,)), _const_spec((RET_VW, RET_VW)),
        ],
        out_specs=_row_spec(D_MODEL),
        compiler_params=_params(("arbitrary",)),
        name="merge",
    )(x, mods, g.reshape(1, D_MODEL), f_lat, f_tail, o_f, o_b, rg, d_lat, d_tail, wgt, wbf, wbr, wbd, wo,
      jnp.asarray(bd, BF16))


def _pick_tile(n, candidates):
    for c in candidates:
        if n % c == 0:
            return c
    raise ValueError(f"no tile for {n}")


def kernel(x, c, ctx, c_ctx, w_ada, b_ada, norm_g, ffn_w1, ffn_w3, ffn_w2, w_in, ret_decay_logit,
           diff_lambda, diff_subln, w_branch_f, w_branch_r, w_branch_d, w_out, final_g):
    batch, seq, d = x.shape
    ctx_len = ctx.shape[1]
    tm = TOKEN_TILE
    assert batch == 1 and d == D_MODEL
    assert seq % max(DFT_N1 * 8, tm) == 0 and ctx_len % 256 == 0 and ctx_len <= tm
    total = seq + ctx_len
    n_lat_tiles = seq // tm
    n_tiles = n_lat_tiles + 1
    n_rows = n_tiles * tm
    n_lat_chunks = seq // RET_CHUNK
    n_ctx_chunks = ctx_len // RET_CHUNK
    n_pad_chunks = (n_rows - total) // RET_CHUNK

    cc = jnp.zeros((8, D_MODEL), F32).at[0].set(c[0]).at[1].set(c_ctx)
    mods_all = _ada_call(cc, w_ada, b_ada)[:, :2].reshape(DEPTH, 2, N_MOD, D_MODEL)

    tables = _rope_tables(seq, n_rows)
    twc, tws = _twiddles(seq)
    log_g2_all = jax.nn.log_sigmoid(ret_decay_logit.astype(F32))
    lv = diff_lambda.astype(F32)
    w_aug_all, wgt_all = _prep_proj_weights(w_in)
    w1_all, w3_all, w2_all = ffn_w1.astype(BF16), ffn_w3.astype(BF16), ffn_w2.astype(BF16)
    wbf_all, wbr_all = w_branch_f.astype(BF16), w_branch_r.astype(BF16)
    wbd_all, wo_all = w_branch_d.astype(BF16), w_out.astype(BF16)

    tq = _pick_tile(seq, (512, 256, 128))
    tk = _pick_tile(total, (3328, 1280, 640, 256, 128))
    tail_pad = ((0, tm - ctx_len), (0, 0))

    xs = jnp.concatenate([x[0], ctx[0], jnp.zeros((n_rows - total, D_MODEL), F32)], axis=0)
    for l in range(DEPTH):
        last = l == DEPTH - 1
        lam_init = 0.8 - 0.6 * math.exp(-0.3 * l)
        mods = mods_all[l]
        lam = (jnp.exp(jnp.sum(lv[l, 0] * lv[l, 1])) - jnp.exp(jnp.sum(lv[l, 2] * lv[l, 3]))
               + lam_init).reshape(1)

        xs = _ffn_call(xs, mods, norm_g[l, 0], w1_all, w3_all, w2_all, (l, 0), 0, n_lat_tiles, n_tiles)

        uf, rq, rkt, rv, rg, dqt, dk, vt_aug = _proj_call(xs, mods, norm_g[l, 1], w_aug_all, l, tables,
                                                           n_lat_tiles, n_tiles)

        f_lat = _fourier_latent(uf, seq, twc, tws)
        o_f, o_b = _retention_call(rq, rkt, rv, log_g2_all[l], n_lat_chunks, n_ctx_chunks, n_pad_chunks)
        d_lat = _attn_call(dqt, dk, vt_aug, lam, diff_subln[l], lam_init, seq, tq, tk, total // tk)

        if last:
            f_tail = jnp.zeros((tm, F_W), F32)
            d_tail = jnp.zeros((DIFF_VW, tm), F32)
            n_out = n_lat_tiles
        else:
            f_tail = jnp.pad(_fourier_ctx(uf[seq:total]), tail_pad)
            d_ctx = _attn_call(dqt, dk, vt_aug, lam, diff_subln[l], lam_init, ctx_len, ctx_len, ctx_len, 1,
                               q_off=seq // ctx_len, k_off=seq // ctx_len)
            d_tail = jnp.pad(d_ctx, tail_pad[::-1])
            n_out = n_tiles

        xs = _merge_call(xs, mods, norm_g[l, 1], f_lat, f_tail, o_f, o_b, rg, d_lat, d_tail, wgt_all,
                         wbf_all, wbr_all, wbd_all, wo_all, l, n_lat_tiles, n_out)

        xs = _ffn_call(xs, mods, norm_g[l, 2], w1_all, w3_all, w2_all, (l, 1), 6, n_lat_tiles, n_out,
                       final_g=final_g if last else None)

    return xs.reshape(1, seq, D_MODEL)
```

```python
import functools
import math

import numpy as np
import jax
import jax.numpy as jnp
from jax import lax
from jax.experimental import pallas as pl
from jax.experimental.pallas import tpu as pltpu

D_MODEL = 1024
DEPTH = 4
GRID_W = 64
D_FF = 2816
N_MOD = 9
FOURIER_GROUPS = 4
FOURIER_CH = 64
RET_HEADS = 6
RET_QK = 32
RET_V = 64
RET_CHUNK = 128
DIFF_HEADS = 6
DIFF_QK = 32
DIFF_V = 64
ROPE_BASE = 10000.0
EPS = 1e-6
F_W = FOURIER_GROUPS * FOURIER_CH
RET_QW = RET_HEADS * RET_QK
RET_VW = RET_HEADS * RET_V
DIFF_QW = DIFF_HEADS * 2 * DIFF_QK
DIFF_VW = DIFF_HEADS * DIFF_V
GATE_W = 3 * D_MODEL

LANES = 128
SUBLANES = 8
V7X_VMEM_BYTES = 64 * 1024 * 1024
VMEM_LIMIT_BYTES = V7X_VMEM_BYTES * 7 // 8
ADA_COL_TILE = 9 * LANES

RET_QP = 2 * LANES
RET_STEP_CHUNKS = 2
TOKEN_TILE = 512
DFT_N1 = 128
ATTN_KEY_CHUNK = 256
ATTN_VT_ROWS = DIFF_V + 16

BF16 = jnp.bfloat16
F32 = jnp.float32
LOG2E = math.log2(math.e)


def _dot(a, b):
    return jnp.dot(a, b, preferred_element_type=F32)


def _split_bf16(x):
    hi = x.astype(BF16)
    lo = (x - hi.astype(F32)).astype(BF16)
    return hi, lo


def _dot3_split(a, b):
    (ah, al), (bh, bl) = a, b
    return _dot(ah, bh) + _dot(al, bh) + _dot(ah, bl)


def _dot3(a, b):
    return _dot3_split(_split_bf16(a), _split_bf16(b))


def _norm_mod(x, g, shift, scale):
    ms = jnp.mean(x * x, axis=-1, keepdims=True)
    y = x * lax.rsqrt(ms + EPS) * g
    return y * (1.0 + scale) + shift


def _sigmoid(x):
    return 1.0 / (1.0 + jnp.exp(-x))


def _const_spec(shape):
    nd = len(shape)
    return pl.BlockSpec(shape, lambda *_: (0,) * nd, pipeline_mode=pl.Buffered(1))


def _stacked_spec(shape, lead):
    block = (None,) * len(lead) + tuple(shape)
    index = tuple(lead) + (0,) * len(shape)
    return pl.BlockSpec(block, lambda *_: index, pipeline_mode=pl.Buffered(1))


def _params(sem):
    return pltpu.CompilerParams(dimension_semantics=sem, vmem_limit_bytes=VMEM_LIMIT_BYTES)


def _row_spec(width):
    return pl.BlockSpec((TOKEN_TILE, width), lambda i: (i, 0))


def _ada_kernel(cc_ref, w_ref, b_ref, o_ref):
    cc = cc_ref[...]
    s = cc * _sigmoid(cc)
    o_ref[0] = _dot3(s, w_ref[0]) + b_ref[0]


def _ada_call(cc, w_ada, b_ada):
    depth, d, n = w_ada.shape
    tn = ADA_COL_TILE
    assert n % tn == 0 and cc.shape == (SUBLANES, d)
    return pl.pallas_call(
        _ada_kernel,
        out_shape=jax.ShapeDtypeStruct((depth, SUBLANES, n), F32),
        grid=(depth, n // tn),
        in_specs=[
            pl.BlockSpec((SUBLANES, d), lambda l, j: (0, 0)),
            pl.BlockSpec((1, d, tn), lambda l, j: (l, 0, j)),
            pl.BlockSpec((1, 1, tn), lambda l, j: (l, 0, j)),
        ],
        out_specs=pl.BlockSpec((1, SUBLANES, tn), lambda l, j: (l, 0, j)),
        compiler_params=_params(("arbitrary", "arbitrary")),
        name="adaln",
    )(cc, w_ada, b_ada.reshape(depth, 1, n))


def _mod_spec(n_lat_tiles):
    return pl.BlockSpec((1, N_MOD, D_MODEL), lambda i: (jnp.where(i >= n_lat_tiles, 1, 0), 0, 0))


def _ffn_tile(x, mod_ref, g_ref, w1_ref, w3_ref, w2_ref, base):
    shift = mod_ref[0, base:base + 1, :]
    scale = mod_ref[0, base + 1:base + 2, :]
    gate = mod_ref[0, base + 2:base + 3, :]
    hb = _norm_mod(x, g_ref[...], shift, scale).astype(BF16)
    a = _dot(hb, w1_ref[...])
    b = _dot(hb, w3_ref[...])
    u = (a * _sigmoid(a) * b).astype(BF16)
    return x + (0.5 * gate) * _dot(u, w2_ref[...])


def _ffn_kernel(x_ref, mod_ref, g_ref, w1_ref, w3_ref, w2_ref, o_ref, *, base):
    o_ref[...] = _ffn_tile(x_ref[...], mod_ref, g_ref, w1_ref, w3_ref, w2_ref, base)


def _ffn_final_kernel(x_ref, mod_ref, g_ref, w1_ref, w3_ref, w2_ref, fg_ref, o_ref, *, base):
    y = _ffn_tile(x_ref[...], mod_ref, g_ref, w1_ref, w3_ref, w2_ref, base)
    ms = jnp.mean(y * y, axis=-1, keepdims=True)
    o_ref[...] = y * lax.rsqrt(ms + EPS) * fg_ref[...]


def _ffn_call(x, mods, g, w1, w3, w2, widx, base, n_lat_tiles, n_tiles, final_g=None):
    in_specs = [
        _row_spec(D_MODEL), _mod_spec(n_lat_tiles), _const_spec((1, D_MODEL)),
        _stacked_spec((D_MODEL, D_FF), widx), _stacked_spec((D_MODEL, D_FF), widx),
        _stacked_spec((D_FF, D_MODEL), widx),
    ]
    args = [x, mods, g.reshape(1, D_MODEL), w1, w3, w2]
    body = _ffn_kernel
    if final_g is not None:
        in_specs.append(_const_spec((1, D_MODEL)))
        args.append(final_g.reshape(1, D_MODEL))
        body = _ffn_final_kernel
    return pl.pallas_call(
        functools.partial(body, base=base),
        out_shape=jax.ShapeDtypeStruct((n_tiles * TOKEN_TILE, D_MODEL), F32),
        grid=(n_tiles,),
        in_specs=in_specs,
        out_specs=_row_spec(D_MODEL),
        compiler_params=_params(("arbitrary",)),
        name="ffn",
    )(*args)


_C_F, _C_RQ, _C_RK, _C_RV, _C_RG = 0, 256, 512, 768, 1152
_C_DQ, _C_DK, _C_DV = 1536, 1920, 2304
_C_RQR, _C_RKR, _C_DQR, _C_DKR = 2688, 2944, 3200, 3584
PROJ_W = 3968


def _rotate_half_cols(w, block):
    depth, d, width = w.shape
    w5 = w.reshape(depth, d, width // block, 2, block // 2)
    return jnp.concatenate([-w5[:, :, :, 1:2], w5[:, :, :, 0:1]], axis=3).reshape(depth, d, width)


def _prep_proj_weights(w_in):
    cuts = np.cumsum([F_W, RET_QW, RET_QW, RET_VW, RET_VW, DIFF_QW, DIFF_QW, DIFF_VW])
    wf, wrq, wrk, wrv, wrg, wdq, wdk, wdv, wgt = jnp.split(w_in, cuts, axis=2)
    z = jnp.zeros(w_in.shape[:2] + (RET_QP - RET_QW,), w_in.dtype)
    parts = [wf, wrq, z, wrk, z, wrv, wrg, wdq, wdk, wdv,
             _rotate_half_cols(wrq, RET_QK), z, _rotate_half_cols(wrk, RET_QK), z,
             _rotate_half_cols(wdq, DIFF_QK // 2), _rotate_half_cols(wdk, DIFF_QK // 2)]
    return jnp.concatenate(parts, axis=2).astype(BF16), wgt.astype(BF16)


def _rope_tables(seq, n_rows):
    pos = jnp.arange(seq, dtype=F32)
    inv_r = ROPE_BASE ** (-jnp.arange(0, RET_QK, 2, dtype=F32) / RET_QK)
    ang_r = pos[:, None] * inv_r[None, :]
    cos_r = jnp.tile(jnp.cos(ang_r), (1, 2 * LANES // RET_QK))
    sin_r = jnp.tile(jnp.sin(ang_r), (1, 2 * LANES // RET_QK))
    rows = jnp.repeat(jnp.arange(seq // GRID_W, dtype=F32), GRID_W)
    cols = jnp.tile(jnp.arange(GRID_W, dtype=F32), seq // GRID_W)
    dim = DIFF_QK // 2
    inv_d = ROPE_BASE ** (-jnp.arange(0, dim, 2, dtype=F32) / dim)
    a_row = rows[:, None] * inv_d[None, :]
    a_col = cols[:, None] * inv_d[None, :]
    cos_hm = jnp.concatenate([jnp.cos(a_row)] * 2 + [jnp.cos(a_col)] * 2, axis=1)
    sin_hm = jnp.concatenate([jnp.sin(a_row)] * 2 + [jnp.sin(a_col)] * 2, axis=1)
    cos_d = jnp.tile(cos_hm, (1, LANES // DIFF_QK))
    sin_d = jnp.tile(sin_hm, (1, LANES // DIFF_QK))

    def finish(t, fill):
        return jnp.pad(t, ((0, n_rows - seq), (0, 0)), constant_values=fill)

    return finish(cos_r, 1.0), finish(sin_r, 0.0), finish(cos_d, 1.0), finish(sin_d, 0.0)


def _proj_kernel(x_ref, mod_ref, g_ref, w_ref, cr_ref, sr_ref, cd_ref, sd_ref,
                 uf_ref, rq_ref, rkt_ref, rv_ref, rg_ref, dqt_ref, dk_ref, vta_ref):
    x = x_ref[...]
    hb = _norm_mod(x, g_ref[...], mod_ref[0, 3:4, :], mod_ref[0, 4:5, :]).astype(BF16)
    p = _dot(hb, w_ref[...])
    cr = jnp.concatenate([cr_ref[...]] * (RET_QP // LANES), axis=1)
    sr = jnp.concatenate([sr_ref[...]] * (RET_QP // LANES), axis=1)
    cd = jnp.concatenate([cd_ref[...]] * (DIFF_QW // LANES), axis=1)
    sd = jnp.concatenate([sd_ref[...]] * (DIFF_QW // LANES), axis=1)
    uf_ref[...] = p[:, _C_F:_C_F + F_W]
    rq = p[:, _C_RQ:_C_RQ + RET_QP] * cr + p[:, _C_RQR:_C_RQR + RET_QP] * sr
    rk = p[:, _C_RK:_C_RK + RET_QP] * cr + p[:, _C_RKR:_C_RKR + RET_QP] * sr
    rq_ref[...] = rq.astype(BF16)
    rkt_ref[...] = (rk * (RET_QK ** -0.5)).T.astype(BF16)
    rv_ref[...] = p[:, _C_RV:_C_RV + RET_VW].astype(BF16)
    rg_ref[...] = p[:, _C_RG:_C_RG + RET_VW]
    dq = p[:, _C_DQ:_C_DQ + DIFF_QW] * cd + p[:, _C_DQR:_C_DQR + DIFF_QW] * sd
    dk = p[:, _C_DK:_C_DK + DIFF_QW] * cd + p[:, _C_DKR:_C_DKR + DIFF_QW] * sd
    dqt_ref[...] = (dq * ((DIFF_QK ** -0.5) * LOG2E)).T.astype(BF16)
    dk_ref[...] = dk.astype(BF16)
    vt = p[:, _C_DV:_C_DV + DIFF_VW].T
    ones = jnp.ones((ATTN_VT_ROWS - DIFF_V, x.shape[0]), F32)
    for h in range(DIFF_HEADS):
        vta_ref[h] = jnp.concatenate([vt[h * DIFF_V:(h + 1) * DIFF_V], ones], axis=0).astype(BF16)


def _proj_call(x, mods, g, w_aug, layer, tables, n_lat_tiles, n_tiles):
    tm = TOKEN_TILE
    t = n_tiles * tm
    out_shape = (
        jax.ShapeDtypeStruct((t, F_W), F32),
        jax.ShapeDtypeStruct((t, RET_QP), BF16),
        jax.ShapeDtypeStruct((RET_QP, t), BF16),
        jax.ShapeDtypeStruct((t, RET_VW), BF16),
        jax.ShapeDtypeStruct((t, RET_VW), F32),
        jax.ShapeDtypeStruct((DIFF_QW, t), BF16),
        jax.ShapeDtypeStruct((t, DIFF_QW), BF16),
        jax.ShapeDtypeStruct((DIFF_HEADS, ATTN_VT_ROWS, t), BF16),
    )
    out_specs = (
        _row_spec(F_W), _row_spec(RET_QP),
        pl.BlockSpec((RET_QP, tm), lambda i: (0, i)),
        _row_spec(RET_VW), _row_spec(RET_VW),
        pl.BlockSpec((DIFF_QW, tm), lambda i: (0, i)),
        _row_spec(DIFF_QW),
        pl.BlockSpec((DIFF_HEADS, ATTN_VT_ROWS, tm), lambda i: (0, 0, i)),
    )
    return pl.pallas_call(
        _proj_kernel,
        out_shape=out_shape,
        grid=(n_tiles,),
        in_specs=[
            _row_spec(D_MODEL), _mod_spec(n_lat_tiles), _const_spec((1, D_MODEL)),
            _stacked_spec((D_MODEL, PROJ_W), (layer,)),
            _row_spec(LANES), _row_spec(LANES), _row_spec(LANES), _row_spec(LANES),
        ],
        out_specs=out_specs,
        compiler_params=_params(("arbitrary",)),
        name="mixer_proj",
    )(x, mods, g.reshape(1, D_MODEL), w_aug, *tables)


def _dft_mats(n):
    k = np.arange(n)
    ang = 2.0 * np.pi * ((k[:, None] * k[None, :]) % n) / n
    return np.cos(ang).astype(np.float32), np.sin(ang).astype(np.float32)


def _channel_dft_mats():
    c, s = _dft_mats(FOURIER_CH)
    eye = np.eye(FOURIER_GROUPS, dtype=np.float32)
    return np.kron(eye, c), np.kron(eye, s)


def _fourier_stage1_kernel(x_ref, cc_ref, sc_ref, c1_ref, s1_ref, twc_ref, tws_ref, tr_ref, ti_ref, *, nb):
    cc, sc = _split_bf16(cc_ref[...]), _split_bf16(sc_ref[...])
    zr, zi = [], []
    for j in range(nb):
        u = _split_bf16(x_ref[:, j * F_W:(j + 1) * F_W])
        zr.append(_dot3_split(u, cc))
        zi.append(-_dot3_split(u, sc))
    zr = _split_bf16(jnp.concatenate(zr, axis=1) if nb > 1 else zr[0])
    zi = _split_bf16(jnp.concatenate(zi, axis=1) if nb > 1 else zi[0])
    c1, s1 = _split_bf16(c1_ref[...]), _split_bf16(s1_ref[...])
    tr = _dot3_split(c1, zr) + _dot3_split(s1, zi)
    ti = _dot3_split(c1, zi) - _dot3_split(s1, zr)
    twc, tws = twc_ref[0], tws_ref[0]
    for j in range(nb):
        cols = slice(j * F_W, (j + 1) * F_W)
        c, s = twc[:, j:j + 1], tws[:, j:j + 1]
        tr_ref[:, cols] = tr[:, cols] * c + ti[:, cols] * s
        ti_ref[:, cols] = ti[:, cols] * c - tr[:, cols] * s


def _fourier_stage2_kernel(tr_ref, ti_ref, c2_ref, s2_ref, o_ref, *, kb):
    c2, s2 = c2_ref[...], s2_ref[...]
    for j in range(kb):
        o_ref[:, j, :] = _dot3(c2, tr_ref[j]) + _dot3(s2, ti_ref[j])


def _fourier_latent(u_all, seq, twc, tws):
    n1, n2 = DFT_N1, seq // DFT_N1
    nb = _fourier_block(n2)
    kb = SUBLANES
    cc, sc = _channel_dft_mats()
    c1, s1 = _dft_mats(n1)
    c2, s2 = _dft_mats(n2)
    assert u_all.shape[0] % n2 == 0
    x2 = u_all.reshape(u_all.shape[0] // n2, n2 * F_W)
    blk = pl.BlockSpec((n1, nb * F_W), lambda i: (0, i))
    tw_blk = pl.BlockSpec((1, n1, nb), lambda i: (i, 0, 0))
    tr, ti = pl.pallas_call(
        functools.partial(_fourier_stage1_kernel, nb=nb),
        out_shape=(jax.ShapeDtypeStruct((n1, n2 * F_W), F32),) * 2,
        grid=(n2 // nb,),
        in_specs=[blk, _const_spec((F_W, F_W)), _const_spec((F_W, F_W)),
                  _const_spec((n1, n1)), _const_spec((n1, n1)), tw_blk, tw_blk],
        out_specs=(blk, blk),
        compiler_params=_params(("arbitrary",)),
        name="fourier_stage1",
    )(x2, cc, sc, c1, s1, twc, tws)
    tr3 = tr.reshape(n1, n2, F_W)
    ti3 = ti.reshape(n1, n2, F_W)
    o3 = pl.pallas_call(
        functools.partial(_fourier_stage2_kernel, kb=kb),
        out_shape=jax.ShapeDtypeStruct((n2, n1, F_W), F32),
        grid=(n1 // kb,),
        in_specs=[pl.BlockSpec((kb, n2, F_W), lambda i: (i, 0, 0))] * 2
        + [_const_spec((n2, n2)), _const_spec((n2, n2))],
        out_specs=pl.BlockSpec((n2, kb, F_W), lambda i: (0, i, 0)),
        compiler_params=_params(("arbitrary",)),
        name="fourier_stage2",
    )(tr3, ti3, c2, s2)
    return o3.reshape(seq, F_W)


def _twiddles(seq):
    n1, n2 = DFT_N1, seq // DFT_N1
    k1 = jnp.arange(n1, dtype=jnp.int32)[:, None]
    m2 = jnp.arange(n2, dtype=jnp.int32)[None, :]
    ang = (2.0 * math.pi / seq) * ((k1 * m2) % seq).astype(F32)
    scale = 1.0 / math.sqrt(seq * FOURIER_CH)
    nb = _fourier_block(n2)

    def blocked(t):
        return t.reshape(n1, n2 // nb, nb).transpose(1, 0, 2)

    return blocked(jnp.cos(ang) * scale), blocked(jnp.sin(ang) * scale)


def _fourier_block(n2):
    return min(SUBLANES, n2)


def _fourier_ctx_kernel(u_ref, cc_ref, sc_ref, cl_ref, sl_ref, o_ref, *, scale):
    u = u_ref[...]
    a = _dot3(u, cc_ref[...])
    b = _dot3(u, sc_ref[...])
    o_ref[...] = (_dot3(cl_ref[...], a) - _dot3(sl_ref[...], b)) * scale


def _fourier_ctx(u):
    n = u.shape[0]
    cc, sc = _channel_dft_mats()
    cl, sl = _dft_mats(n)
    return pl.pallas_call(
        functools.partial(_fourier_ctx_kernel, scale=1.0 / math.sqrt(n * FOURIER_CH)),
        out_shape=jax.ShapeDtypeStruct((n, F_W), F32),
        name="fourier_ctx",
    )(u, cc, sc, cl, sl)


def _ret_chunk_local(q, kt, v, d_ref, kdec, bdmask, hmask_ref, vmask_ref):
    pieces = []
    vparts = []
    for h in range(RET_HEADS):
        qh = q * hmask_ref[h]
        pieces.append((_dot(qh, kt) * d_ref[h]).astype(BF16))
        vparts.append(v * vmask_ref[h])
    inner = jnp.concatenate(pieces, axis=1)
    vbd = jnp.concatenate(vparts, axis=0)
    kd = (kt.astype(F32) * kdec).astype(BF16)
    return _dot(inner, vbd), bdmask * _dot(kd, v)


def _ret_dir(q_ref, kt_ref, v_ref, o_ref, s_ref, d_ref, qdec, kdec, cd, bdmask, hmask_ref, vmask_ref, order):
    c = RET_CHUNK
    local = []
    for half in order:
        rows = slice(half * c, (half + 1) * c)
        q = q_ref[rows, :]
        local.append((rows, q) + _ret_chunk_local(q, kt_ref[:, rows], v_ref[rows, :], d_ref, kdec, bdmask,
                                                  hmask_ref, vmask_ref))
    s = s_ref[...]
    for rows, q, o_intra, inc in local:
        o_ref[rows, :] = o_intra + _dot(q, s.astype(BF16)) * qdec
        s = s * cd + inc
    s_ref[...] = s


def _ret_kernel(logg_ref, lgv_ref, lgk_ref, bdmask_ref, hmask_ref, vmask_ref,
                qf_ref, ktf_ref, vf_ref, qb_ref, ktb_ref, vb_ref,
                of_ref, ob_ref,
                sf_ref, sb_ref, df_ref, db_ref, qdf_ref, qdb_ref, kdf_ref, kdb_ref, cdf_ref, cdb_ref):
    c = RET_CHUNK

    @pl.when(pl.program_id(0) == 0)
    def _():
        sf_ref[...] = jnp.zeros_like(sf_ref)
        sb_ref[...] = jnp.zeros_like(sb_ref)
        ii = lax.broadcasted_iota(jnp.int32, (c, c), 0).astype(F32)
        jj = lax.broadcasted_iota(jnp.int32, (c, c), 1).astype(F32)
        for h in range(RET_HEADS):
            df_ref[h] = jnp.where(ii >= jj, jnp.exp(logg_ref[0, h] * jnp.maximum(ii - jj, 0.0)), 0.0)
            db_ref[h] = jnp.where(jj >= ii, jnp.exp(logg_ref[1, h] * jnp.maximum(jj - ii, 0.0)), 0.0)
        ri = lax.broadcasted_iota(jnp.int32, (c, RET_VW), 0).astype(F32)
        qdf_ref[...] = jnp.exp(lgv_ref[0] * (ri + 1.0))
        qdb_ref[...] = jnp.exp(lgv_ref[1] * (c - ri))
        cj = lax.broadcasted_iota(jnp.int32, (RET_QP, c), 1).astype(F32)
        kdf_ref[...] = jnp.exp(lgk_ref[0] * (c - 1.0 - cj))
        kdb_ref[...] = jnp.exp(lgk_ref[1] * cj)
        cdf_ref[...] = jnp.exp(lgv_ref[0] * float(c))
        cdb_ref[...] = jnp.exp(lgv_ref[1] * float(c))

    bdmask = bdmask_ref[...]
    halves = list(range(RET_STEP_CHUNKS))
    _ret_dir(qf_ref, ktf_ref, vf_ref, of_ref, sf_ref, df_ref,
             qdf_ref[...], kdf_ref[...], cdf_ref[...], bdmask, hmask_ref, vmask_ref, halves)
    _ret_dir(qb_ref, ktb_ref, vb_ref, ob_ref, sb_ref, db_ref,
             qdb_ref[...], kdb_ref[...], cdb_ref[...], bdmask, hmask_ref, vmask_ref, halves[::-1])


def _ret_masks():
    hm = np.zeros((RET_HEADS, 1, RET_QP), np.float32)
    vm = np.zeros((RET_HEADS, 1, RET_VW), np.float32)
    bd = np.zeros((RET_QP, RET_VW), np.float32)
    for h in range(RET_HEADS):
        hm[h, 0, h * RET_QK:(h + 1) * RET_QK] = 1.0
        vm[h, 0, h * RET_V:(h + 1) * RET_V] = 1.0
        bd[h * RET_QK:(h + 1) * RET_QK, h * RET_V:(h + 1) * RET_V] = 1.0
    return jnp.asarray(bd), jnp.asarray(hm, BF16), jnp.asarray(vm, BF16)


def _retention_call(rq, rkt, rv, log_g2, n_lat, n_ctx, n_pad):
    c = RET_CHUNK
    g = RET_STEP_CHUNKS
    assert n_lat % g == 0 and n_ctx % g == 0 and n_pad % g == 0
    n_lat, n_ctx, n_pad = n_lat // g, n_ctx // g, n_pad // g
    n_real = n_lat + n_ctx
    n = n_real + n_pad
    rows = g * c

    def fwd(i):
        return jnp.where(i < n_ctx, n_lat + i, jnp.where(i < n_real, i - n_ctx, i))

    def bwd(i):
        return jnp.where(i < n_real, n_real - 1 - i, i)

    lgv = jnp.repeat(log_g2, RET_V, axis=1).reshape(2, 1, RET_VW)
    lgk = jnp.pad(jnp.repeat(log_g2, RET_QK, axis=1), ((0, 0), (0, RET_QP - RET_QW)))
    lgk = jnp.broadcast_to(lgk[:, :, None], (2, RET_QP, c))
    bd, hm, vm = _ret_masks()

    def specs(ix):
        return [pl.BlockSpec((rows, RET_QP), lambda i: (ix(i), 0)),
                pl.BlockSpec((RET_QP, rows), lambda i: (0, ix(i))),
                pl.BlockSpec((rows, RET_VW), lambda i: (ix(i), 0))]

    vmem = pltpu.VMEM
    return pl.pallas_call(
        _ret_kernel,
        out_shape=(jax.ShapeDtypeStruct((n * rows, RET_VW), F32),) * 2,
        grid=(n,),
        in_specs=[pl.BlockSpec(memory_space=pltpu.SMEM),
                  _const_spec((2, 1, RET_VW)), _const_spec((2, RET_QP, c)),
                  _const_spec((RET_QP, RET_VW)), _const_spec((RET_HEADS, 1, RET_QP)),
                  _const_spec((RET_HEADS, 1, RET_VW))] + specs(fwd) + specs(bwd),
        out_specs=(pl.BlockSpec((rows, RET_VW), lambda i: (fwd(i), 0)),
                   pl.BlockSpec((rows, RET_VW), lambda i: (bwd(i), 0))),
        scratch_shapes=[vmem((RET_QP, RET_VW), F32), vmem((RET_QP, RET_VW), F32),
                        vmem((RET_HEADS, c, c), F32), vmem((RET_HEADS, c, c), F32),
                        vmem((c, RET_VW), F32), vmem((c, RET_VW), F32),
                        vmem((RET_QP, c), F32), vmem((RET_QP, c), F32),
                        vmem((1, RET_VW), F32), vmem((1, RET_VW), F32)],
        compiler_params=_params(("arbitrary",)),
        name="retention",
    )(log_g2, lgv, lgk, bd, hm, vm, rq, rkt, rv, rq, rkt, rv)


def _chunk_scores(k_chunk, qm, s_out_ref, rows, run_max):
    s = _dot(k_chunk, qm)
    s_out_ref[rows, :] = s
    cm = jnp.max(s.reshape(s.shape[0] // SUBLANES, SUBLANES, s.shape[1]), axis=0)
    return cm if run_max is None else jnp.maximum(run_max, cm)


def _mask_map(qt, j):
    row = lax.broadcasted_iota(jnp.int32, qt.shape, 0)
    lo = j * DIFF_QK
    return jnp.where((row >= lo) & (row < lo + DIFF_QK), qt, jnp.zeros_like(qt))


def _attn_kernel(lam_ref, qt_ref, qtn_ref, k0_ref, kn_ref, vt_ref, subln_ref, o_ref,
                 qm_ref, m_ref, acc_ref, s_ref, mx_ref, kcur_ref, *, out_scale, key_chunk):
    ki = pl.program_id(2)
    is_last = ki == pl.num_programs(2) - 1
    tk = kn_ref.shape[0]
    n_chunks = tk // key_chunk

    def chunk(c):
        return slice(c * key_chunk, (c + 1) * key_chunk)

    @pl.when(ki == 0)
    def _():
        qt = qt_ref[...]
        for j in range(4):
            qm_ref[j] = _mask_map(qt, j)
        m_ref[...] = jnp.full(m_ref.shape, -jnp.inf, F32)
        acc_ref[...] = jnp.zeros_like(acc_ref)

    @pl.when((ki == 0) & (pl.program_id(0) == 0) & (pl.program_id(1) == 0))
    def _():
        kcur_ref[...] = k0_ref[...]
        run = None
        for c in range(n_chunks):
            run = _chunk_scores(k0_ref[chunk(c), :], qm_ref[0], s_ref.at[0], chunk(c), run)
        mx_ref[...] = run

    q_ahead = jnp.where(is_last, _mask_map(qtn_ref[...], 0), qm_ref[0])

    run = mx_ref[...]
    for u in range(4):
        cur, nxt = u % 2, (u + 1) % 2
        m_prev = m_ref[u]
        m_new = jnp.maximum(m_prev, jnp.max(run, axis=0, keepdims=True))
        alpha = jnp.exp2(m_prev - m_new)
        vt = vt_ref[u // 2]
        run = None
        acc = None
        p_prev = None
        for c in range(n_chunks):
            if u < 3:
                run = _chunk_scores(kcur_ref[chunk(c), :], qm_ref[u + 1], s_ref.at[nxt], chunk(c), run)
            else:
                run = _chunk_scores(kn_ref[chunk(c), :], q_ahead, s_ref.at[nxt], chunk(c), run)
            p = jnp.exp2(s_ref[cur, chunk(c), :] - m_new).astype(BF16)
            if p_prev is not None:
                pv = _dot(vt[:, chunk(c - 1)], p_prev)
                acc = pv if acc is None else acc + pv
            p_prev = p
        pv = _dot(vt[:, chunk(n_chunks - 1)], p_prev)
        acc = pv if acc is None else acc + pv
        acc_ref[u] = alpha * acc_ref[u] + acc
        m_ref[u] = m_new
    mx_ref[...] = run
    kcur_ref[...] = kn_ref[...]

    @pl.when(is_last)
    def _():
        lam = lam_ref[0]
        outs = []
        for hl in range(2):
            a1 = acc_ref[2 * hl]
            a2 = acc_ref[2 * hl + 1]
            o = (a1[:DIFF_V] / a1[DIFF_V:DIFF_V + 1]
                 - lam * (a2[:DIFF_V] / a2[DIFF_V:DIFF_V + 1]))
            ms = jnp.mean(o * o, axis=0, keepdims=True)
            outs.append(o * lax.rsqrt(ms + EPS) * subln_ref[...] * out_scale)
        o_ref[...] = jnp.concatenate(outs, axis=0)


def _attn_call(dqt, dk, vt_aug, lam, subln, lam_init, n_q, tq, tk, nk, q_off=0, k_off=0):
    n_hp = DIFF_HEADS // 2
    n_i = n_q // tq

    def next_pair(h, i):
        wrap = i + 1 >= n_i
        return jnp.where(wrap, jnp.minimum(h + 1, n_hp - 1), h), jnp.where(wrap, 0, i + 1)

    def q_next_map(h, i, k):
        hn, i_n = next_pair(h, i)
        return hn, q_off + i_n

    def k_next_map(h, i, k):
        last = k + 1 >= nk
        return k_off + jnp.where(last, 0, k + 1), jnp.where(last, next_pair(h, i)[0], h)

    return pl.pallas_call(
        functools.partial(_attn_kernel, out_scale=1.0 - lam_init, key_chunk=min(tk, ATTN_KEY_CHUNK)),
        out_shape=jax.ShapeDtypeStruct((DIFF_VW, n_q), F32),
        grid=(n_hp, n_i, nk),
        in_specs=[
            pl.BlockSpec(memory_space=pltpu.SMEM),
            pl.BlockSpec((LANES, tq), lambda h, i, k: (h, q_off + i)),
            pl.BlockSpec((LANES, tq), q_next_map),
            pl.BlockSpec((tk, LANES), lambda h, i, k: (k_off, 0)),
            pl.BlockSpec((tk, LANES), k_next_map),
            pl.BlockSpec((2, ATTN_VT_ROWS, tk), lambda h, i, k: (h, 0, k_off + k)),
            pl.BlockSpec((DIFF_V, 1), lambda h, i, k: (0, 0)),
        ],
        out_specs=pl.BlockSpec((LANES, tq), lambda h, i, k: (h, i)),
        scratch_shapes=[pltpu.VMEM((4, LANES, tq), BF16),
                        pltpu.VMEM((4, 1, tq), F32),
                        pltpu.VMEM((4, ATTN_VT_ROWS, tq), F32),
                        pltpu.VMEM((2, tk, tq), F32),
                        pltpu.VMEM((SUBLANES, tq), F32),
                        pltpu.VMEM((tk, LANES), BF16)],
        compiler_params=_params(("arbitrary", "arbitrary", "arbitrary")),
        name="diff_attn",
    )(lam, dqt, dqt, dk, dk, vt_aug, subln.reshape(DIFF_V, 1))


def _merge_kernel(x_ref, mod_ref, g_ref, flat_ref, ftail_ref, of_ref, ob_ref, rg_ref, dlat_ref, dtail_ref,
                  wgt_ref, wbf_ref, wbr_ref, wbd_ref, wo_ref, bd_ref, o_ref, *, n_lat_tiles):
    x = x_ref[...]
    hb = _norm_mod(x, g_ref[...], mod_ref[0, 3:4, :], mod_ref[0, 4:5, :]).astype(BF16)
    gates = _sigmoid(_dot(hb, wgt_ref[...]))
    is_tail = pl.program_id(0) >= n_lat_tiles
    f = jnp.where(is_tail, ftail_ref[...], flat_ref[...])
    d = jnp.where(is_tail, dtail_ref[...], dlat_ref[...]).T
    r = of_ref[...] + ob_ref[...]
    rr_hi, rr_lo = _split_bf16(r * r)
    ms = _dot(rr_hi, bd_ref[...]) + _dot(rr_lo, bd_ref[...])
    rg = rg_ref[...]
    yr = r * lax.rsqrt(ms + EPS) * (rg * _sigmoid(rg))
    mixed = (gates[:, :D_MODEL] * _dot(f.astype(BF16), wbf_ref[...])
             + gates[:, D_MODEL:2 * D_MODEL] * _dot(yr.astype(BF16), wbr_ref[...])
             + gates[:, 2 * D_MODEL:] * _dot(d.astype(BF16), wbd_ref[...]))
    y = _dot(mixed.astype(BF16), wo_ref[...])
    o_ref[...] = x + mod_ref[0, 5:6, :] * y


def _merge_call(x, mods, g, f_lat, f_tail, o_f, o_b, rg, d_lat, d_tail, wgt, wbf, wbr, wbd, wo, layer,
                n_lat_tiles, n_tiles):
    tm = TOKEN_TILE
    bd = np.kron(np.eye(RET_HEADS, dtype=np.float32), np.full((RET_V, RET_V), 1.0 / RET_V, np.float32))

    def lat_spec(w):
        return pl.BlockSpec((tm, w), lambda i: (jnp.minimum(i, n_lat_tiles - 1), 0))

    return pl.pallas_call(
        functools.partial(_merge_kernel, n_lat_tiles=n_lat_tiles),
        out_shape=jax.ShapeDtypeStruct((n_tiles * tm, D_MODEL), F32),
        grid=(n_tiles,),
        in_specs=[
            _row_spec(D_MODEL), _mod_spec(n_lat_tiles), _const_spec((1, D_MODEL)),
            lat_spec(F_W), _const_spec((tm, F_W)),
            _row_spec(RET_VW), _row_spec(RET_VW), _row_spec(RET_VW),
            pl.BlockSpec((DIFF_VW, tm), lambda i: (0, jnp.minimum(i, n_lat_tiles - 1))),
            _const_spec((DIFF_VW, tm)),
            _stacked_spec((D_MODEL, GATE_W), (layer,)), _stacked_spec((F_W, D_MODEL), (layer,)),
            _stacked_spec((RET_VW, D_MODEL), (layer,)), _stacked_spec((DIFF_VW, D_MODEL), (layer,)),
            _stacked_spec((D_MODEL, D_MODEL), (layer,)), _const_spec((RET_VW, RET_VW)),
        ],
        out_specs=_row_spec(D_MODEL),
        compiler_params=_params(("arbitrary",)),
        name="merge",
    )(x, mods, g.reshape(1, D_MODEL), f_lat, f_tail, o_f, o_b, rg, d_lat, d_tail, wgt, wbf, wbr, wbd, wo,
      jnp.asarray(bd, BF16))


def _pick_tile(n, candidates):
    for c in candidates:
        if n % c == 0:
            return c
    raise ValueError(f"no tile for {n}")


def kernel(x, c, ctx, c_ctx, w_ada, b_ada, norm_g, ffn_w1, ffn_w3, ffn_w2, w_in, ret_decay_logit,
           diff_lambda, diff_subln, w_branch_f, w_branch_r, w_branch_d, w_out, final_g):
    batch, seq, d = x.shape
    ctx_len = ctx.shape[1]
    tm = TOKEN_TILE
    assert batch == 1 and d == D_MODEL
    assert seq % max(DFT_N1 * SUBLANES, tm) == 0 and seq % ctx_len == 0
    assert ctx_len % (RET_STEP_CHUNKS * RET_CHUNK) == 0 and ctx_len % ATTN_KEY_CHUNK == 0 and ctx_len <= tm
    total = seq + ctx_len
    n_lat_tiles = seq // tm
    n_tiles = n_lat_tiles + 1
    n_rows = n_tiles * tm
    n_lat_chunks = seq // RET_CHUNK
    n_ctx_chunks = ctx_len // RET_CHUNK
    n_pad_chunks = (n_rows - total) // RET_CHUNK

    cc = jnp.zeros((SUBLANES, D_MODEL), F32).at[0].set(c[0]).at[1].set(c_ctx)
    mods_all = _ada_call(cc, w_ada, b_ada)[:, :2].reshape(DEPTH, 2, N_MOD, D_MODEL)

    tables = _rope_tables(seq, n_rows)
    twc, tws = _twiddles(seq)
    log_g2_all = jax.nn.log_sigmoid(ret_decay_logit.astype(F32))
    lv = diff_lambda.astype(F32)
    w_aug_all, wgt_all = _prep_proj_weights(w_in)
    w1_all, w3_all, w2_all = ffn_w1.astype(BF16), ffn_w3.astype(BF16), ffn_w2.astype(BF16)
    wbf_all, wbr_all = w_branch_f.astype(BF16), w_branch_r.astype(BF16)
    wbd_all, wo_all = w_branch_d.astype(BF16), w_out.astype(BF16)

    tq = _pick_tile(seq, (512, 256, 128))
    tk = _pick_tile(total, (3328, 1280, 640, 256, 128))
    tail_pad = ((0, tm - ctx_len), (0, 0))

    xs = jnp.concatenate([x[0], ctx[0], jnp.zeros((n_rows - total, D_MODEL), F32)], axis=0)
    for l in range(DEPTH):
        last = l == DEPTH - 1
        lam_init = 0.8 - 0.6 * math.exp(-0.3 * l)
        mods = mods_all[l]
        lam = (jnp.exp(jnp.sum(lv[l, 0] * lv[l, 1])) - jnp.exp(jnp.sum(lv[l, 2] * lv[l, 3]))
               + lam_init).reshape(1)

        xs = _ffn_call(xs, mods, norm_g[l, 0], w1_all, w3_all, w2_all, (l, 0), 0, n_lat_tiles, n_tiles)

        uf, rq, rkt, rv, rg, dqt, dk, vt_aug = _proj_call(xs, mods, norm_g[l, 1], w_aug_all, l, tables,
                                                           n_lat_tiles, n_tiles)

        f_lat = _fourier_latent(uf, seq, twc, tws)
        o_f, o_b = _retention_call(rq, rkt, rv, log_g2_all[l], n_lat_chunks, n_ctx_chunks, n_pad_chunks)
        d_lat = _attn_call(dqt, dk, vt_aug, lam, diff_subln[l], lam_init, seq, tq, tk, total // tk)

        if last:
            f_tail = jnp.zeros((tm, F_W), F32)
            d_tail = jnp.zeros((DIFF_VW, tm), F32)
            n_out = n_lat_tiles
        else:
            f_tail = jnp.pad(_fourier_ctx(uf[seq:total]), tail_pad)
            d_ctx = _attn_call(dqt, dk, vt_aug, lam, diff_subln[l], lam_init, ctx_len, ctx_len, ctx_len, 1,
                               q_off=seq // ctx_len, k_off=seq // ctx_len)
            d_tail = jnp.pad(d_ctx, tail_pad[::-1])
            n_out = n_tiles

        xs = _merge_call(xs, mods, norm_g[l, 1], f_lat, f_tail, o_f, o_b, rg, d_lat, d_tail, wgt_all,
                         wbf_all, wbr_all, wbd_all, wo_all, l, n_lat_tiles, n_out)

        xs = _ffn_call(xs, mods, norm_g[l, 2], w1_all, w3_all, w2_all, (l, 1), 6, n_lat_tiles, n_out,
                       final_g=final_g if last else None)

    return xs.reshape(1, seq, D_MODEL)
```

```python
import functools
import math

import numpy as np
import jax
import jax.numpy as jnp
from jax import lax
from jax.experimental import pallas as pl
from jax.experimental.pallas import tpu as pltpu

D_MODEL = 1024
DEPTH = 4
GRID_W = 64
D_FF = 2816
N_MOD = 9
FOURIER_GROUPS = 4
FOURIER_CH = 64
RET_HEADS = 6
RET_QK = 32
RET_V = 64
RET_CHUNK = 128
DIFF_HEADS = 6
DIFF_QK = 32
DIFF_V = 64
ROPE_BASE = 10000.0
EPS = 1e-6
F_W = FOURIER_GROUPS * FOURIER_CH
RET_QW = RET_HEADS * RET_QK
RET_VW = RET_HEADS * RET_V
DIFF_QW = DIFF_HEADS * 2 * DIFF_QK
DIFF_VW = DIFF_HEADS * DIFF_V
GATE_W = 3 * D_MODEL

LANES = 128
SUBLANES = 8
V7X_VMEM_BYTES = 64 * 1024 * 1024
VMEM_LIMIT_BYTES = V7X_VMEM_BYTES * 7 // 8
ADA_COL_TILE = 9 * LANES

RET_QP = 2 * LANES
RET_STEP_CHUNKS = 2
TOKEN_TILE = 512
DFT_N1 = 128
ATTN_KEY_CHUNK = 256
ATTN_VT_ROWS = DIFF_V + 16

BF16 = jnp.bfloat16
F32 = jnp.float32
LOG2E = math.log2(math.e)


def _dot(a, b):
    return jnp.dot(a, b, preferred_element_type=F32)


def _split_bf16(x):
    hi = x.astype(BF16)
    lo = (x - hi.astype(F32)).astype(BF16)
    return hi, lo


def _dot3_split(a, b):
    (ah, al), (bh, bl) = a, b
    return _dot(ah, bh) + _dot(al, bh) + _dot(ah, bl)


def _dot3(a, b):
    return _dot3_split(_split_bf16(a), _split_bf16(b))


def _norm_mod(x, g, shift, scale):
    ms = jnp.mean(x * x, axis=-1, keepdims=True)
    y = x * lax.rsqrt(ms + EPS) * g
    return y * (1.0 + scale) + shift


def _sigmoid(x):
    return 1.0 / (1.0 + jnp.exp(-x))


def _const_spec(shape):
    nd = len(shape)
    return pl.BlockSpec(shape, lambda *_: (0,) * nd, pipeline_mode=pl.Buffered(1))


def _stacked_spec(shape, lead):
    block = (None,) * len(lead) + tuple(shape)
    index = tuple(lead) + (0,) * len(shape)
    return pl.BlockSpec(block, lambda *_: index, pipeline_mode=pl.Buffered(1))


def _params(sem):
    return pltpu.CompilerParams(dimension_semantics=sem, vmem_limit_bytes=VMEM_LIMIT_BYTES)


def _row_spec(width):
    return pl.BlockSpec((TOKEN_TILE, width), lambda i: (i, 0))


def _ada_kernel(cc_ref, w_ref, b_ref, o_ref):
    cc = cc_ref[...]
    s = cc * _sigmoid(cc)
    o_ref[0] = _dot3(s, w_ref[0]) + b_ref[0]


def _ada_call(cc, w_ada, b_ada):
    depth, d, n = w_ada.shape
    tn = ADA_COL_TILE
    assert n % tn == 0 and cc.shape == (SUBLANES, d)
    return pl.pallas_call(
        _ada_kernel,
        out_shape=jax.ShapeDtypeStruct((depth, SUBLANES, n), F32),
        grid=(depth, n // tn),
        in_specs=[
            pl.BlockSpec((SUBLANES, d), lambda l, j: (0, 0)),
            pl.BlockSpec((1, d, tn), lambda l, j: (l, 0, j)),
            pl.BlockSpec((1, 1, tn), lambda l, j: (l, 0, j)),
        ],
        out_specs=pl.BlockSpec((1, SUBLANES, tn), lambda l, j: (l, 0, j)),
        compiler_params=_params(("arbitrary", "arbitrary")),
        name="adaln",
    )(cc, w_ada, b_ada.reshape(depth, 1, n))


def _mod_spec(n_lat_tiles):
    return pl.BlockSpec((1, N_MOD, D_MODEL), lambda i: (jnp.where(i >= n_lat_tiles, 1, 0), 0, 0))


def _ffn_tile(x, mod_ref, g_ref, w1_ref, w3_ref, w2_ref, base):
    shift = mod_ref[0, base:base + 1, :]
    scale = mod_ref[0, base + 1:base + 2, :]
    gate = mod_ref[0, base + 2:base + 3, :]
    hb = _norm_mod(x, g_ref[...], shift, scale).astype(BF16)
    a = _dot(hb, w1_ref[...])
    b = _dot(hb, w3_ref[...])
    u = (a * _sigmoid(a) * b).astype(BF16)
    return x + (0.5 * gate) * _dot(u, w2_ref[...])


def _ffn_kernel(x_ref, mod_ref, g_ref, w1_ref, w3_ref, w2_ref, o_ref, *, base):
    o_ref[...] = _ffn_tile(x_ref[...], mod_ref, g_ref, w1_ref, w3_ref, w2_ref, base)


def _ffn_first_kernel(xlat_ref, xtail_ref, mod_ref, g_ref, w1_ref, w3_ref, w2_ref, o_ref, *, base, n_lat_tiles):
    x = jnp.where(pl.program_id(0) >= n_lat_tiles, xtail_ref[...], xlat_ref[...])
    o_ref[...] = _ffn_tile(x, mod_ref, g_ref, w1_ref, w3_ref, w2_ref, base)


def _ffn_final_kernel(x_ref, mod_ref, g_ref, w1_ref, w3_ref, w2_ref, fg_ref, o_ref, *, base):
    y = _ffn_tile(x_ref[...], mod_ref, g_ref, w1_ref, w3_ref, w2_ref, base)
    ms = jnp.mean(y * y, axis=-1, keepdims=True)
    o_ref[...] = y * lax.rsqrt(ms + EPS) * fg_ref[...]


def _ffn_call(x, mods, g, w1, w3, w2, widx, base, n_lat_tiles, n_tiles, final_g=None, x_tail=None):
    in_specs = [
        _row_spec(D_MODEL), _mod_spec(n_lat_tiles), _const_spec((1, D_MODEL)),
        _stacked_spec((D_MODEL, D_FF), widx), _stacked_spec((D_MODEL, D_FF), widx),
        _stacked_spec((D_FF, D_MODEL), widx),
    ]
    args = [x, mods, g.reshape(1, D_MODEL), w1, w3, w2]
    body = functools.partial(_ffn_kernel, base=base)
    assert final_g is None or x_tail is None
    if final_g is not None:
        in_specs.append(_const_spec((1, D_MODEL)))
        args.append(final_g.reshape(1, D_MODEL))
        body = functools.partial(_ffn_final_kernel, base=base)
    if x_tail is not None:
        in_specs[0:1] = [pl.BlockSpec((TOKEN_TILE, D_MODEL), lambda i: (jnp.minimum(i, n_lat_tiles - 1), 0)),
                         _const_spec((TOKEN_TILE, D_MODEL))]
        args[0:1] = [x, x_tail]
        body = functools.partial(_ffn_first_kernel, base=base, n_lat_tiles=n_lat_tiles)
    return pl.pallas_call(
        body,
        out_shape=jax.ShapeDtypeStruct((n_tiles * TOKEN_TILE, D_MODEL), F32),
        grid=(n_tiles,),
        in_specs=in_specs,
        out_specs=_row_spec(D_MODEL),
        compiler_params=_params(("arbitrary",)),
        name="ffn",
    )(*args)


_C_F, _C_RQ, _C_RK, _C_RV, _C_RG = 0, 256, 512, 768, 1152
_C_DQ, _C_DK, _C_DV = 1536, 1920, 2304
_C_RQR, _C_RKR, _C_DQR, _C_DKR = 2688, 2944, 3200, 3584
PROJ_W = 3968


def _rotate_half_cols(w, block):
    depth, d, width = w.shape
    w5 = w.reshape(depth, d, width // block, 2, block // 2)
    return jnp.concatenate([-w5[:, :, :, 1:2], w5[:, :, :, 0:1]], axis=3).reshape(depth, d, width)


def _prep_proj_weights(w_in):
    cuts = np.cumsum([F_W, RET_QW, RET_QW, RET_VW, RET_VW, DIFF_QW, DIFF_QW, DIFF_VW])
    wf, wrq, wrk, wrv, wrg, wdq, wdk, wdv, wgt = jnp.split(w_in, cuts, axis=2)
    z = jnp.zeros(w_in.shape[:2] + (RET_QP - RET_QW,), w_in.dtype)
    parts = [wf, wrq, z, wrk, z, wrv, wrg, wdq, wdk, wdv,
             _rotate_half_cols(wrq, RET_QK), z, _rotate_half_cols(wrk, RET_QK), z,
             _rotate_half_cols(wdq, DIFF_QK // 2), _rotate_half_cols(wdk, DIFF_QK // 2)]
    return jnp.concatenate(parts, axis=2).astype(BF16), wgt.astype(BF16)


def _rope_tables(seq, n_rows):
    pos = jnp.arange(seq, dtype=F32)
    inv_r = ROPE_BASE ** (-jnp.arange(0, RET_QK, 2, dtype=F32) / RET_QK)
    ang_r = pos[:, None] * inv_r[None, :]
    cos_r = jnp.tile(jnp.cos(ang_r), (1, 2 * LANES // RET_QK))
    sin_r = jnp.tile(jnp.sin(ang_r), (1, 2 * LANES // RET_QK))
    rows = jnp.repeat(jnp.arange(seq // GRID_W, dtype=F32), GRID_W)
    cols = jnp.tile(jnp.arange(GRID_W, dtype=F32), seq // GRID_W)
    dim = DIFF_QK // 2
    inv_d = ROPE_BASE ** (-jnp.arange(0, dim, 2, dtype=F32) / dim)
    a_row = rows[:, None] * inv_d[None, :]
    a_col = cols[:, None] * inv_d[None, :]
    cos_hm = jnp.concatenate([jnp.cos(a_row)] * 2 + [jnp.cos(a_col)] * 2, axis=1)
    sin_hm = jnp.concatenate([jnp.sin(a_row)] * 2 + [jnp.sin(a_col)] * 2, axis=1)
    cos_d = jnp.tile(cos_hm, (1, LANES // DIFF_QK))
    sin_d = jnp.tile(sin_hm, (1, LANES // DIFF_QK))

    def finish(t, fill):
        return jnp.pad(t, ((0, n_rows - seq), (0, 0)), constant_values=fill)

    return finish(cos_r, 1.0), finish(sin_r, 0.0), finish(cos_d, 1.0), finish(sin_d, 0.0)


def _proj_kernel(x_ref, mod_ref, g_ref, w_ref, cr_ref, sr_ref, cd_ref, sd_ref,
                 uf_ref, rq_ref, rkt_ref, rv_ref, rg_ref, dqt_ref, dk_ref, vta_ref):
    x = x_ref[...]
    hb = _norm_mod(x, g_ref[...], mod_ref[0, 3:4, :], mod_ref[0, 4:5, :]).astype(BF16)
    p = _dot(hb, w_ref[...])
    cr = jnp.concatenate([cr_ref[...]] * (RET_QP // LANES), axis=1)
    sr = jnp.concatenate([sr_ref[...]] * (RET_QP // LANES), axis=1)
    cd = jnp.concatenate([cd_ref[...]] * (DIFF_QW // LANES), axis=1)
    sd = jnp.concatenate([sd_ref[...]] * (DIFF_QW // LANES), axis=1)
    uf_ref[...] = p[:, _C_F:_C_F + F_W]
    rq = p[:, _C_RQ:_C_RQ + RET_QP] * cr + p[:, _C_RQR:_C_RQR + RET_QP] * sr
    rk = p[:, _C_RK:_C_RK + RET_QP] * cr + p[:, _C_RKR:_C_RKR + RET_QP] * sr
    rq_ref[...] = rq.astype(BF16)
    rkt_ref[...] = (rk * (RET_QK ** -0.5)).T.astype(BF16)
    rv_ref[...] = p[:, _C_RV:_C_RV + RET_VW].astype(BF16)
    rg_ref[...] = p[:, _C_RG:_C_RG + RET_VW]
    dq = p[:, _C_DQ:_C_DQ + DIFF_QW] * cd + p[:, _C_DQR:_C_DQR + DIFF_QW] * sd
    dk = p[:, _C_DK:_C_DK + DIFF_QW] * cd + p[:, _C_DKR:_C_DKR + DIFF_QW] * sd
    dqt_ref[...] = (dq * ((DIFF_QK ** -0.5) * LOG2E)).T.astype(BF16)
    dk_ref[...] = dk.astype(BF16)
    vt = p[:, _C_DV:_C_DV + DIFF_VW].T
    ones = jnp.ones((ATTN_VT_ROWS - DIFF_V, x.shape[0]), F32)
    for h in range(DIFF_HEADS):
        vta_ref[h] = jnp.concatenate([vt[h * DIFF_V:(h + 1) * DIFF_V], ones], axis=0).astype(BF16)


def _proj_call(x, mods, g, w_aug, layer, tables, n_lat_tiles, n_tiles):
    tm = TOKEN_TILE
    t = n_tiles * tm
    out_shape = (
        jax.ShapeDtypeStruct((t, F_W), F32),
        jax.ShapeDtypeStruct((t, RET_QP), BF16),
        jax.ShapeDtypeStruct((RET_QP, t), BF16),
        jax.ShapeDtypeStruct((t, RET_VW), BF16),
        jax.ShapeDtypeStruct((t, RET_VW), F32),
        jax.ShapeDtypeStruct((DIFF_QW, t), BF16),
        jax.ShapeDtypeStruct((t, DIFF_QW), BF16),
        jax.ShapeDtypeStruct((DIFF_HEADS, ATTN_VT_ROWS, t), BF16),
    )
    out_specs = (
        _row_spec(F_W), _row_spec(RET_QP),
        pl.BlockSpec((RET_QP, tm), lambda i: (0, i)),
        _row_spec(RET_VW), _row_spec(RET_VW),
        pl.BlockSpec((DIFF_QW, tm), lambda i: (0, i)),
        _row_spec(DIFF_QW),
        pl.BlockSpec((DIFF_HEADS, ATTN_VT_ROWS, tm), lambda i: (0, 0, i)),
    )
    return pl.pallas_call(
        _proj_kernel,
        out_shape=out_shape,
        grid=(n_tiles,),
        in_specs=[
            _row_spec(D_MODEL), _mod_spec(n_lat_tiles), _const_spec((1, D_MODEL)),
            _stacked_spec((D_MODEL, PROJ_W), (layer,)),
            _row_spec(LANES), _row_spec(LANES), _row_spec(LANES), _row_spec(LANES),
        ],
        out_specs=out_specs,
        compiler_params=_params(("arbitrary",)),
        name="mixer_proj",
    )(x, mods, g.reshape(1, D_MODEL), w_aug, *tables)


def _dft_mats(n):
    k = np.arange(n)
    ang = 2.0 * np.pi * ((k[:, None] * k[None, :]) % n) / n
    return np.cos(ang).astype(np.float32), np.sin(ang).astype(np.float32)


def _channel_dft_mats():
    c, s = _dft_mats(FOURIER_CH)
    eye = np.eye(FOURIER_GROUPS, dtype=np.float32)
    return np.kron(eye, c), np.kron(eye, s)


def _fourier_stage1_kernel(x_ref, cc_ref, sc_ref, c1_ref, s1_ref, twc_ref, tws_ref, tr_ref, ti_ref, *, nb):
    cc, sc = _split_bf16(cc_ref[...]), _split_bf16(sc_ref[...])
    zr, zi = [], []
    for j in range(nb):
        u = _split_bf16(x_ref[:, j * F_W:(j + 1) * F_W])
        zr.append(_dot3_split(u, cc))
        zi.append(-_dot3_split(u, sc))
    zr = _split_bf16(jnp.concatenate(zr, axis=1) if nb > 1 else zr[0])
    zi = _split_bf16(jnp.concatenate(zi, axis=1) if nb > 1 else zi[0])
    c1, s1 = _split_bf16(c1_ref[...]), _split_bf16(s1_ref[...])
    tr = _dot3_split(c1, zr) + _dot3_split(s1, zi)
    ti = _dot3_split(c1, zi) - _dot3_split(s1, zr)
    twc, tws = twc_ref[0], tws_ref[0]
    for j in range(nb):
        cols = slice(j * F_W, (j + 1) * F_W)
        c, s = twc[:, j:j + 1], tws[:, j:j + 1]
        tr_ref[:, cols] = tr[:, cols] * c + ti[:, cols] * s
        ti_ref[:, cols] = ti[:, cols] * c - tr[:, cols] * s


def _fourier_stage2_kernel(tr_ref, ti_ref, c2_ref, s2_ref, o_ref, *, kb):
    c2, s2 = c2_ref[...], s2_ref[...]
    for j in range(kb):
        o_ref[:, j, :] = _dot3(c2, tr_ref[j]) + _dot3(s2, ti_ref[j])


def _fourier_latent(u_all, seq, twc, tws):
    n1, n2 = DFT_N1, seq // DFT_N1
    nb = _fourier_block(n2)
    kb = SUBLANES
    cc, sc = _channel_dft_mats()
    c1, s1 = _dft_mats(n1)
    c2, s2 = _dft_mats(n2)
    assert u_all.shape[0] % n2 == 0
    x2 = u_all.reshape(u_all.shape[0] // n2, n2 * F_W)
    blk = pl.BlockSpec((n1, nb * F_W), lambda i: (0, i))
    tw_blk = pl.BlockSpec((1, n1, nb), lambda i: (i, 0, 0))
    tr, ti = pl.pallas_call(
        functools.partial(_fourier_stage1_kernel, nb=nb),
        out_shape=(jax.ShapeDtypeStruct((n1, n2 * F_W), F32),) * 2,
        grid=(n2 // nb,),
        in_specs=[blk, _const_spec((F_W, F_W)), _const_spec((F_W, F_W)),
                  _const_spec((n1, n1)), _const_spec((n1, n1)), tw_blk, tw_blk],
        out_specs=(blk, blk),
        compiler_params=_params(("arbitrary",)),
        name="fourier_stage1",
    )(x2, cc, sc, c1, s1, twc, tws)
    tr3 = tr.reshape(n1, n2, F_W)
    ti3 = ti.reshape(n1, n2, F_W)
    o3 = pl.pallas_call(
        functools.partial(_fourier_stage2_kernel, kb=kb),
        out_shape=jax.ShapeDtypeStruct((n2, n1, F_W), F32),
        grid=(n1 // kb,),
        in_specs=[pl.BlockSpec((kb, n2, F_W), lambda i: (i, 0, 0))] * 2
        + [_const_spec((n2, n2)), _const_spec((n2, n2))],
        out_specs=pl.BlockSpec((n2, kb, F_W), lambda i: (0, i, 0)),
        compiler_params=_params(("arbitrary",)),
        name="fourier_stage2",
    )(tr3, ti3, c2, s2)
    return o3.reshape(seq, F_W)


def _twiddles(seq):
    n1, n2 = DFT_N1, seq // DFT_N1
    k1 = jnp.arange(n1, dtype=jnp.int32)[:, None]
    m2 = jnp.arange(n2, dtype=jnp.int32)[None, :]
    ang = (2.0 * math.pi / seq) * ((k1 * m2) % seq).astype(F32)
    scale = 1.0 / math.sqrt(seq * FOURIER_CH)
    nb = _fourier_block(n2)

    def blocked(t):
        return t.reshape(n1, n2 // nb, nb).transpose(1, 0, 2)

    return blocked(jnp.cos(ang) * scale), blocked(jnp.sin(ang) * scale)


def _fourier_block(n2):
    return min(SUBLANES, n2)


def _fourier_ctx_kernel(u_ref, cc_ref, sc_ref, cl_ref, sl_ref, o_ref, *, scale):
    u = u_ref[...]
    a = _dot3(u, cc_ref[...])
    b = _dot3(u, sc_ref[...])
    o_ref[...] = (_dot3(cl_ref[...], a) - _dot3(sl_ref[...], b)) * scale


def _fourier_ctx(u):
    n = u.shape[0]
    cc, sc = _channel_dft_mats()
    cl, sl = _dft_mats(n)
    return pl.pallas_call(
        functools.partial(_fourier_ctx_kernel, scale=1.0 / math.sqrt(n * FOURIER_CH)),
        out_shape=jax.ShapeDtypeStruct((n, F_W), F32),
        name="fourier_ctx",
    )(u, cc, sc, cl, sl)


def _ret_chunk_local(q, kt, v, d_ref, kdec, bdmask, hmask_ref, vmask_ref):
    pieces = []
    vparts = []
    for hp in range(RET_HEADS // 2):
        kpair = jnp.concatenate([kt * hmask_ref[2 * hp], kt * hmask_ref[2 * hp + 1]], axis=1)
        pieces.append((_dot(q, kpair) * d_ref[hp]).astype(BF16))
    for h in range(RET_HEADS):
        vparts.append(v * vmask_ref[h])
    inner = jnp.concatenate(pieces, axis=1)
    vbd = jnp.concatenate(vparts, axis=0)
    kd = (kt.astype(F32) * kdec).astype(BF16)
    return _dot(inner, vbd), bdmask * _dot(kd, v)


def _ret_dir(q_ref, kt_ref, v_ref, o_ref, s_ref, d_ref, qdec, kdec, cd, bdmask, hmask_ref, vmask_ref, order):
    c = RET_CHUNK
    local = []
    for half in order:
        rows = slice(half * c, (half + 1) * c)
        q = q_ref[rows, :]
        local.append((rows, q) + _ret_chunk_local(q, kt_ref[:, rows], v_ref[rows, :], d_ref, kdec, bdmask,
                                                  hmask_ref, vmask_ref))
    s = s_ref[...]
    for rows, q, o_intra, inc in local:
        o_ref[rows, :] = o_intra + _dot(q, s.astype(BF16)) * qdec
        s = s * cd + inc
    s_ref[...] = s


def _ret_kernel(logg_ref, lgv_ref, lgk_ref, bdmask_ref, hmask_ref, vmask_ref,
                qf_ref, ktf_ref, vf_ref, qb_ref, ktb_ref, vb_ref,
                of_ref, ob_ref,
                sf_ref, sb_ref, df_ref, db_ref, qdf_ref, qdb_ref, kdf_ref, kdb_ref, cdf_ref, cdb_ref):
    c = RET_CHUNK

    @pl.when(pl.program_id(0) == 0)
    def _():
        sf_ref[...] = jnp.zeros_like(sf_ref)
        sb_ref[...] = jnp.zeros_like(sb_ref)
        ii = lax.broadcasted_iota(jnp.int32, (c, c), 0).astype(F32)
        jj = lax.broadcasted_iota(jnp.int32, (c, c), 1).astype(F32)
        for h in range(RET_HEADS):
            half = slice((h % 2) * c, (h % 2 + 1) * c)
            df_ref[h // 2, :, half] = jnp.where(ii >= jj, jnp.exp(logg_ref[0, h] * jnp.maximum(ii - jj, 0.0)), 0.0)
            db_ref[h // 2, :, half] = jnp.where(jj >= ii, jnp.exp(logg_ref[1, h] * jnp.maximum(jj - ii, 0.0)), 0.0)
        ri = lax.broadcasted_iota(jnp.int32, (c, RET_VW), 0).astype(F32)
        qdf_ref[...] = jnp.exp(lgv_ref[0] * (ri + 1.0))
        qdb_ref[...] = jnp.exp(lgv_ref[1] * (c - ri))
        cj = lax.broadcasted_iota(jnp.int32, (RET_QP, c), 1).astype(F32)
        kdf_ref[...] = jnp.exp(lgk_ref[0] * (c - 1.0 - cj))
        kdb_ref[...] = jnp.exp(lgk_ref[1] * cj)
        cdf_ref[...] = jnp.exp(lgv_ref[0] * float(c))
        cdb_ref[...] = jnp.exp(lgv_ref[1] * float(c))

    bdmask = bdmask_ref[...]
    halves = list(range(RET_STEP_CHUNKS))
    _ret_dir(qf_ref, ktf_ref, vf_ref, of_ref, sf_ref, df_ref,
             qdf_ref[...], kdf_ref[...], cdf_ref[...], bdmask, hmask_ref, vmask_ref, halves)
    _ret_dir(qb_ref, ktb_ref, vb_ref, ob_ref, sb_ref, db_ref,
             qdb_ref[...], kdb_ref[...], cdb_ref[...], bdmask, hmask_ref, vmask_ref, halves[::-1])


def _ret_masks():
    hm = np.zeros((RET_HEADS, RET_QP, RET_CHUNK), np.float32)
    vm = np.zeros((RET_HEADS, 1, RET_VW), np.float32)
    bd = np.zeros((RET_QP, RET_VW), np.float32)
    for h in range(RET_HEADS):
        hm[h, h * RET_QK:(h + 1) * RET_QK, :] = 1.0
        vm[h, 0, h * RET_V:(h + 1) * RET_V] = 1.0
        bd[h * RET_QK:(h + 1) * RET_QK, h * RET_V:(h + 1) * RET_V] = 1.0
    return jnp.asarray(bd), jnp.asarray(hm, BF16), jnp.asarray(vm, BF16)


def _retention_call(rq, rkt, rv, log_g2, n_lat, n_ctx, n_pad):
    c = RET_CHUNK
    g = RET_STEP_CHUNKS
    assert n_lat % g == 0 and n_ctx % g == 0 and n_pad % g == 0
    n_lat, n_ctx, n_pad = n_lat // g, n_ctx // g, n_pad // g
    n_real = n_lat + n_ctx
    n = n_real + n_pad
    rows = g * c

    def fwd(i):
        return jnp.where(i < n_ctx, n_lat + i, jnp.where(i < n_real, i - n_ctx, i))

    def bwd(i):
        return jnp.where(i < n_real, n_real - 1 - i, i)

    lgv = jnp.repeat(log_g2, RET_V, axis=1).reshape(2, 1, RET_VW)
    lgk = jnp.pad(jnp.repeat(log_g2, RET_QK, axis=1), ((0, 0), (0, RET_QP - RET_QW)))
    lgk = jnp.broadcast_to(lgk[:, :, None], (2, RET_QP, c))
    bd, hm, vm = _ret_masks()

    def specs(ix):
        return [pl.BlockSpec((rows, RET_QP), lambda i: (ix(i), 0)),
                pl.BlockSpec((RET_QP, rows), lambda i: (0, ix(i))),
                pl.BlockSpec((rows, RET_VW), lambda i: (ix(i), 0))]

    vmem = pltpu.VMEM
    return pl.pallas_call(
        _ret_kernel,
        out_shape=(jax.ShapeDtypeStruct((n * rows, RET_VW), F32),) * 2,
        grid=(n,),
        in_specs=[pl.BlockSpec(memory_space=pltpu.SMEM),
                  _const_spec((2, 1, RET_VW)), _const_spec((2, RET_QP, c)),
                  _const_spec((RET_QP, RET_VW)), _const_spec((RET_HEADS, RET_QP, c)),
                  _const_spec((RET_HEADS, 1, RET_VW))] + specs(fwd) + specs(bwd),
        out_specs=(pl.BlockSpec((rows, RET_VW), lambda i: (fwd(i), 0)),
                   pl.BlockSpec((rows, RET_VW), lambda i: (bwd(i), 0))),
        scratch_shapes=[vmem((RET_QP, RET_VW), F32), vmem((RET_QP, RET_VW), F32),
                        vmem((RET_HEADS // 2, c, 2 * c), F32), vmem((RET_HEADS // 2, c, 2 * c), F32),
                        vmem((c, RET_VW), F32), vmem((c, RET_VW), F32),
                        vmem((RET_QP, c), F32), vmem((RET_QP, c), F32),
                        vmem((1, RET_VW), F32), vmem((1, RET_VW), F32)],
        compiler_params=_params(("arbitrary",)),
        name="retention",
    )(log_g2, lgv, lgk, bd, hm, vm, rq, rkt, rv, rq, rkt, rv)


def _chunk_scores(k_chunk, qm, s_out_ref, rows, run_max):
    s = _dot(k_chunk, qm)
    s_out_ref[rows, :] = s
    cm = jnp.max(s.reshape(s.shape[0] // SUBLANES, SUBLANES, s.shape[1]), axis=0)
    return cm if run_max is None else jnp.maximum(run_max, cm)


def _mask_map(qt, j):
    row = lax.broadcasted_iota(jnp.int32, qt.shape, 0)
    lo = j * DIFF_QK
    return jnp.where((row >= lo) & (row < lo + DIFF_QK), qt, jnp.zeros_like(qt))


def _attn_kernel(lam_ref, qt_ref, qtn_ref, k0_ref, kn_ref, vt_ref, subln_ref, o_ref,
                 qm_ref, m_ref, acc_ref, s_ref, mx_ref, kcur_ref, *, out_scale, key_chunk):
    ki = pl.program_id(2)
    is_last = ki == pl.num_programs(2) - 1
    tk = kn_ref.shape[0]
    n_chunks = tk // key_chunk

    def chunk(c):
        return slice(c * key_chunk, (c + 1) * key_chunk)

    @pl.when(ki == 0)
    def _():
        qt = qt_ref[...]
        for j in range(4):
            qm_ref[j] = _mask_map(qt, j)
        m_ref[...] = jnp.full(m_ref.shape, -jnp.inf, F32)
        acc_ref[...] = jnp.zeros_like(acc_ref)

    @pl.when((ki == 0) & (pl.program_id(0) == 0) & (pl.program_id(1) == 0))
    def _():
        kcur_ref[...] = k0_ref[...]
        run = None
        for c in range(n_chunks):
            run = _chunk_scores(k0_ref[chunk(c), :], qm_ref[0], s_ref.at[0], chunk(c), run)
        mx_ref[...] = run

    q_ahead = jnp.where(is_last, _mask_map(qtn_ref[...], 0), qm_ref[0])

    run = mx_ref[...]
    for u in range(4):
        cur, nxt = u % 2, (u + 1) % 2
        m_prev = m_ref[u]
        m_new = jnp.maximum(m_prev, jnp.max(run, axis=0, keepdims=True))
        alpha = jnp.exp2(m_prev - m_new)
        vt = vt_ref[u // 2]
        run = None
        acc = None
        p_prev = None
        for c in range(n_chunks):
            if u < 3:
                run = _chunk_scores(kcur_ref[chunk(c), :], qm_ref[u + 1], s_ref.at[nxt], chunk(c), run)
            else:
                run = _chunk_scores(kn_ref[chunk(c), :], q_ahead, s_ref.at[nxt], chunk(c), run)
            p = jnp.exp2(s_ref[cur, chunk(c), :] - m_new).astype(BF16)
            if p_prev is not None:
                pv = _dot(vt[:, chunk(c - 1)], p_prev)
                acc = pv if acc is None else acc + pv
            p_prev = p
        pv = _dot(vt[:, chunk(n_chunks - 1)], p_prev)
        acc = pv if acc is None else acc + pv
        acc_ref[u] = alpha * acc_ref[u] + acc
        m_ref[u] = m_new
    mx_ref[...] = run
    kcur_ref[...] = kn_ref[...]

    @pl.when(is_last)
    def _():
        lam = lam_ref[0]
        outs = []
        for hl in range(2):
            a1 = acc_ref[2 * hl]
            a2 = acc_ref[2 * hl + 1]
            o = (a1[:DIFF_V] / a1[DIFF_V:DIFF_V + 1]
                 - lam * (a2[:DIFF_V] / a2[DIFF_V:DIFF_V + 1]))
            ms = jnp.mean(o * o, axis=0, keepdims=True)
            outs.append(o * lax.rsqrt(ms + EPS) * subln_ref[...] * out_scale)
        o_ref[...] = jnp.concatenate(outs, axis=0)


def _attn_call(dqt, dk, vt_aug, lam, subln, lam_init, n_q, tq, tk, nk, q_off=0, k_off=0):
    n_hp = DIFF_HEADS // 2
    n_i = n_q // tq

    def next_pair(h, i):
        wrap = i + 1 >= n_i
        return jnp.where(wrap, jnp.minimum(h + 1, n_hp - 1), h), jnp.where(wrap, 0, i + 1)

    def q_next_map(h, i, k):
        hn, i_n = next_pair(h, i)
        return hn, q_off + i_n

    def k_next_map(h, i, k):
        last = k + 1 >= nk
        return k_off + jnp.where(last, 0, k + 1), jnp.where(last, next_pair(h, i)[0], h)

    return pl.pallas_call(
        functools.partial(_attn_kernel, out_scale=1.0 - lam_init, key_chunk=min(tk, ATTN_KEY_CHUNK)),
        out_shape=jax.ShapeDtypeStruct((DIFF_VW, n_q), F32),
        grid=(n_hp, n_i, nk),
        in_specs=[
            pl.BlockSpec(memory_space=pltpu.SMEM),
            pl.BlockSpec((LANES, tq), lambda h, i, k: (h, q_off + i)),
            pl.BlockSpec((LANES, tq), q_next_map),
            pl.BlockSpec((tk, LANES), lambda h, i, k: (k_off, 0)),
            pl.BlockSpec((tk, LANES), k_next_map),
            pl.BlockSpec((2, ATTN_VT_ROWS, tk), lambda h, i, k: (h, 0, k_off + k)),
            pl.BlockSpec((DIFF_V, 1), lambda h, i, k: (0, 0)),
        ],
        out_specs=pl.BlockSpec((LANES, tq), lambda h, i, k: (h, i)),
        scratch_shapes=[pltpu.VMEM((4, LANES, tq), BF16),
                        pltpu.VMEM((4, 1, tq), F32),
                        pltpu.VMEM((4, ATTN_VT_ROWS, tq), F32),
                        pltpu.VMEM((2, tk, tq), F32),
                        pltpu.VMEM((SUBLANES, tq), F32),
                        pltpu.VMEM((tk, LANES), BF16)],
        compiler_params=_params(("arbitrary", "arbitrary", "arbitrary")),
        name="diff_attn",
    )(lam, dqt, dqt, dk, dk, vt_aug, subln.reshape(DIFF_V, 1))


def _merge_kernel(x_ref, mod_ref, g_ref, flat_ref, ftail_ref, of_ref, ob_ref, rg_ref, dlat_ref, dtail_ref,
                  wgt_ref, wbf_ref, wbr_ref, wbd_ref, wo_ref, bd_ref, o_ref, *, n_lat_tiles):
    x = x_ref[...]
    hb = _norm_mod(x, g_ref[...], mod_ref[0, 3:4, :], mod_ref[0, 4:5, :]).astype(BF16)
    gates = _sigmoid(_dot(hb, wgt_ref[...]))
    is_tail = pl.program_id(0) >= n_lat_tiles
    f = jnp.where(is_tail, ftail_ref[...], flat_ref[...])
    d = jnp.where(is_tail, dtail_ref[...], dlat_ref[...]).T
    r = of_ref[...] + ob_ref[...]
    rr_hi, rr_lo = _split_bf16(r * r)
    ms = _dot(rr_hi, bd_ref[...]) + _dot(rr_lo, bd_ref[...])
    rg = rg_ref[...]
    yr = r * lax.rsqrt(ms + EPS) * (rg * _sigmoid(rg))
    mixed = (gates[:, :D_MODEL] * _dot(f.astype(BF16), wbf_ref[...])
             + gates[:, D_MODEL:2 * D_MODEL] * _dot(yr.astype(BF16), wbr_ref[...])
             + gates[:, 2 * D_MODEL:] * _dot(d.astype(BF16), wbd_ref[...]))
    y = _dot(mixed.astype(BF16), wo_ref[...])
    o_ref[...] = x + mod_ref[0, 5:6, :] * y


def _merge_call(x, mods, g, f_lat, f_tail, o_f, o_b, rg, d_lat, d_tail, wgt, wbf, wbr, wbd, wo, layer,
                n_lat_tiles, n_tiles):
    tm = TOKEN_TILE
    bd = np.kron(np.eye(RET_HEADS, dtype=np.float32), np.full((RET_V, RET_V), 1.0 / RET_V, np.float32))

    def lat_spec(w):
        return pl.BlockSpec((tm, w), lambda i: (jnp.minimum(i, n_lat_tiles - 1), 0))

    return pl.pallas_call(
        functools.partial(_merge_kernel, n_lat_tiles=n_lat_tiles),
        out_shape=jax.ShapeDtypeStruct((n_tiles * tm, D_MODEL), F32),
        grid=(n_tiles,),
        in_specs=[
            _row_spec(D_MODEL), _mod_spec(n_lat_tiles), _const_spec((1, D_MODEL)),
            lat_spec(F_W), _const_spec((tm, F_W)),
            _row_spec(RET_VW), _row_spec(RET_VW), _row_spec(RET_VW),
            pl.BlockSpec((DIFF_VW, tm), lambda i: (0, jnp.minimum(i, n_lat_tiles - 1))),
            _const_spec((DIFF_VW, tm)),
            _stacked_spec((D_MODEL, GATE_W), (layer,)), _stacked_spec((F_W, D_MODEL), (layer,)),
            _stacked_spec((RET_VW, D_MODEL), (layer,)), _stacked_spec((DIFF_VW, D_MODEL), (layer,)),
            _stacked_spec((D_MODEL, D_MODEL), (layer,)), _const_spec((RET_VW, RET_VW)),
        ],
        out_specs=_row_spec(D_MODEL),
        compiler_params=_params(("arbitrary",)),
        name="merge",
    )(x, mods, g.reshape(1, D_MODEL), f_lat, f_tail, o_f, o_b, rg, d_lat, d_tail, wgt, wbf, wbr, wbd, wo,
      jnp.asarray(bd, BF16))


def _pick_tile(n, candidates):
    for c in candidates:
        if n % c == 0:
            return c
    raise ValueError(f"no tile for {n}")


def kernel(x, c, ctx, c_ctx, w_ada, b_ada, norm_g, ffn_w1, ffn_w3, ffn_w2, w_in, ret_decay_logit,
           diff_lambda, diff_subln, w_branch_f, w_branch_r, w_branch_d, w_out, final_g):
    batch, seq, d = x.shape
    ctx_len = ctx.shape[1]
    tm = TOKEN_TILE
    assert batch == 1 and d == D_MODEL
    assert seq % max(DFT_N1 * SUBLANES, tm) == 0 and seq % ctx_len == 0
    assert ctx_len % (RET_STEP_CHUNKS * RET_CHUNK) == 0 and ctx_len % ATTN_KEY_CHUNK == 0 and ctx_len <= tm
    total = seq + ctx_len
    n_lat_tiles = seq // tm
    n_tiles = n_lat_tiles + 1
    n_rows = n_tiles * tm
    n_lat_chunks = seq // RET_CHUNK
    n_ctx_chunks = ctx_len // RET_CHUNK
    n_pad_chunks = (n_rows - total) // RET_CHUNK

    cc = jnp.zeros((SUBLANES, D_MODEL), F32).at[0].set(c[0]).at[1].set(c_ctx)
    mods_all = _ada_call(cc, w_ada, b_ada)[:, :2].reshape(DEPTH, 2, N_MOD, D_MODEL)

    tables = _rope_tables(seq, n_rows)
    twc, tws = _twiddles(seq)
    log_g2_all = jax.nn.log_sigmoid(ret_decay_logit.astype(F32))
    lv = diff_lambda.astype(F32)
    w_aug_all, wgt_all = _prep_proj_weights(w_in)
    w1_all, w3_all, w2_all = ffn_w1.astype(BF16), ffn_w3.astype(BF16), ffn_w2.astype(BF16)
    wbf_all, wbr_all = w_branch_f.astype(BF16), w_branch_r.astype(BF16)
    wbd_all, wo_all = w_branch_d.astype(BF16), w_out.astype(BF16)

    tq = _pick_tile(seq, (512, 256, 128))
    tk = _pick_tile(total, (3328, 1280, 640, 256, 128))
    tail_pad = ((0, tm - ctx_len), (0, 0))

    xs = x[0]
    x_tail = jnp.pad(ctx[0], tail_pad)
    for l in range(DEPTH):
        last = l == DEPTH - 1
        lam_init = 0.8 - 0.6 * math.exp(-0.3 * l)
        mods = mods_all[l]
        lam = (jnp.exp(jnp.sum(lv[l, 0] * lv[l, 1])) - jnp.exp(jnp.sum(lv[l, 2] * lv[l, 3]))
               + lam_init).reshape(1)

        xs = _ffn_call(xs, mods, norm_g[l, 0], w1_all, w3_all, w2_all, (l, 0), 0, n_lat_tiles, n_tiles,
                       x_tail=x_tail if l == 0 else None)

        uf, rq, rkt, rv, rg, dqt, dk, vt_aug = _proj_call(xs, mods, norm_g[l, 1], w_aug_all, l, tables,
                                                           n_lat_tiles, n_tiles)

        f_lat = _fourier_latent(uf, seq, twc, tws)
        o_f, o_b = _retention_call(rq, rkt, rv, log_g2_all[l], n_lat_chunks, n_ctx_chunks, n_pad_chunks)
        d_lat = _attn_call(dqt, dk, vt_aug, lam, diff_subln[l], lam_init, seq, tq, tk, total // tk)

        if last:
            f_tail = jnp.zeros((tm, F_W), F32)
            d_tail = jnp.zeros((DIFF_VW, tm), F32)
            n_out = n_lat_tiles
        else:
            f_tail = jnp.pad(_fourier_ctx(uf[seq:total]), tail_pad)
            d_ctx = _attn_call(dqt, dk, vt_aug, lam, diff_subln[l], lam_init, ctx_len, ctx_len, ctx_len, 1,
                               q_off=seq // ctx_len, k_off=seq // ctx_len)
            d_tail = jnp.pad(d_ctx, tail_pad[::-1])
            n_out = n_tiles

        xs = _merge_call(xs, mods, norm_g[l, 1], f_lat, f_tail, o_f, o_b, rg, d_lat, d_tail, wgt_all,
                         wbf_all, wbr_all, wbd_all, wo_all, l, n_lat_tiles, n_out)

        xs = _ffn_call(xs, mods, norm_g[l, 2], w1_all, w3_all, w2_all, (l, 1), 6, n_lat_tiles, n_out,
                       final_g=final_g if last else None)

    return xs.reshape(1, seq, D_MODEL)
```

```python
import functools
import math

import numpy as np
import jax
import jax.numpy as jnp
from jax import lax
from jax.experimental import pallas as pl
from jax.experimental.pallas import tpu as pltpu

D_MODEL = 1024
DEPTH = 4
GRID_W = 64
D_FF = 2816
N_MOD = 9
FOURIER_GROUPS = 4
FOURIER_CH = 64
RET_HEADS = 6
RET_QK = 32
RET_V = 64
RET_CHUNK = 128
DIFF_HEADS = 6
DIFF_QK = 32
DIFF_V = 64
ROPE_BASE = 10000.0
EPS = 1e-6
F_W = FOURIER_GROUPS * FOURIER_CH
RET_QW = RET_HEADS * RET_QK
RET_VW = RET_HEADS * RET_V
DIFF_QW = DIFF_HEADS * 2 * DIFF_QK
DIFF_VW = DIFF_HEADS * DIFF_V
GATE_W = 3 * D_MODEL

LANES = 128
SUBLANES = 8
V7X_VMEM_BYTES = 64 * 1024 * 1024
VMEM_LIMIT_BYTES = V7X_VMEM_BYTES * 7 // 8
ADA_COL_TILE = 9 * LANES

RET_QP = 2 * LANES
RET_STEP_CHUNKS = 2
TOKEN_TILE = 512
DFT_N1 = 128
ATTN_KEY_CHUNK = 256
ATTN_VT_ROWS = DIFF_V + 16

BF16 = jnp.bfloat16
F32 = jnp.float32
LOG2E = math.log2(math.e)


def _dot(a, b):
    return jnp.dot(a, b, preferred_element_type=F32)


def _split_bf16(x):
    hi = x.astype(BF16)
    lo = (x - hi.astype(F32)).astype(BF16)
    return hi, lo


def _dot3_split(a, b):
    (ah, al), (bh, bl) = a, b
    return _dot(ah, bh) + _dot(al, bh) + _dot(ah, bl)


def _dot3(a, b):
    return _dot3_split(_split_bf16(a), _split_bf16(b))


def _norm_mod(x, g, shift, scale):
    ms = jnp.mean(x * x, axis=-1, keepdims=True)
    y = x * lax.rsqrt(ms + EPS) * g
    return y * (1.0 + scale) + shift


def _sigmoid(x):
    return 1.0 / (1.0 + jnp.exp(-x))


def _const_spec(shape):
    nd = len(shape)
    return pl.BlockSpec(shape, lambda *_: (0,) * nd, pipeline_mode=pl.Buffered(1))


def _stacked_spec(shape, lead):
    block = (None,) * len(lead) + tuple(shape)
    index = tuple(lead) + (0,) * len(shape)
    return pl.BlockSpec(block, lambda *_: index, pipeline_mode=pl.Buffered(1))


def _params(sem):
    return pltpu.CompilerParams(dimension_semantics=sem, vmem_limit_bytes=VMEM_LIMIT_BYTES)


def _row_spec(width):
    return pl.BlockSpec((TOKEN_TILE, width), lambda i: (i, 0))


def _ada_kernel(cc_ref, w_ref, b_ref, o_ref):
    cc = cc_ref[...]
    s = cc * _sigmoid(cc)
    o_ref[0] = _dot3(s, w_ref[0]) + b_ref[0]


def _ada_call(cc, w_ada, b_ada):
    depth, d, n = w_ada.shape
    tn = ADA_COL_TILE
    assert n % tn == 0 and cc.shape == (SUBLANES, d)
    return pl.pallas_call(
        _ada_kernel,
        out_shape=jax.ShapeDtypeStruct((depth, SUBLANES, n), F32),
        grid=(depth, n // tn),
        in_specs=[
            pl.BlockSpec((SUBLANES, d), lambda l, j: (0, 0)),
            pl.BlockSpec((1, d, tn), lambda l, j: (l, 0, j)),
            pl.BlockSpec((1, 1, tn), lambda l, j: (l, 0, j)),
        ],
        out_specs=pl.BlockSpec((1, SUBLANES, tn), lambda l, j: (l, 0, j)),
        compiler_params=_params(("arbitrary", "arbitrary")),
        name="adaln",
    )(cc, w_ada, b_ada.reshape(depth, 1, n))


def _mod_spec(n_lat_tiles):
    return pl.BlockSpec((1, N_MOD, D_MODEL), lambda i: (jnp.where(i >= n_lat_tiles, 1, 0), 0, 0))


def _ffn_tile(x, mod_ref, g_ref, w1_ref, w3_ref, w2_ref, base):
    shift = mod_ref[0, base:base + 1, :]
    scale = mod_ref[0, base + 1:base + 2, :]
    gate = mod_ref[0, base + 2:base + 3, :]
    hb = _norm_mod(x, g_ref[...], shift, scale).astype(BF16)
    a = _dot(hb, w1_ref[...])
    b = _dot(hb, w3_ref[...])
    u = (a * _sigmoid(a) * b).astype(BF16)
    return x + (0.5 * gate) * _dot(u, w2_ref[...])


def _ffn_kernel(x_ref, mod_ref, g_ref, w1_ref, w3_ref, w2_ref, o_ref, *, base):
    o_ref[...] = _ffn_tile(x_ref[...], mod_ref, g_ref, w1_ref, w3_ref, w2_ref, base)


def _ffn_first_kernel(xlat_ref, xtail_ref, mod_ref, g_ref, w1_ref, w3_ref, w2_ref, o_ref, *, base, n_lat_tiles):
    x = jnp.where(pl.program_id(0) >= n_lat_tiles, xtail_ref[...], xlat_ref[...])
    o_ref[...] = _ffn_tile(x, mod_ref, g_ref, w1_ref, w3_ref, w2_ref, base)


def _ffn_final_kernel(x_ref, mod_ref, g_ref, w1_ref, w3_ref, w2_ref, fg_ref, o_ref, *, base):
    y = _ffn_tile(x_ref[...], mod_ref, g_ref, w1_ref, w3_ref, w2_ref, base)
    ms = jnp.mean(y * y, axis=-1, keepdims=True)
    o_ref[...] = y * lax.rsqrt(ms + EPS) * fg_ref[...]


def _ffn_call(x, mods, g, w1, w3, w2, widx, base, n_lat_tiles, n_tiles, final_g=None, x_tail=None):
    in_specs = [
        _row_spec(D_MODEL), _mod_spec(n_lat_tiles), _const_spec((1, D_MODEL)),
        _stacked_spec((D_MODEL, D_FF), widx), _stacked_spec((D_MODEL, D_FF), widx),
        _stacked_spec((D_FF, D_MODEL), widx),
    ]
    args = [x, mods, g.reshape(1, D_MODEL), w1, w3, w2]
    body = functools.partial(_ffn_kernel, base=base)
    assert final_g is None or x_tail is None
    if final_g is not None:
        in_specs.append(_const_spec((1, D_MODEL)))
        args.append(final_g.reshape(1, D_MODEL))
        body = functools.partial(_ffn_final_kernel, base=base)
    if x_tail is not None:
        in_specs[0:1] = [pl.BlockSpec((TOKEN_TILE, D_MODEL), lambda i: (jnp.minimum(i, n_lat_tiles - 1), 0)),
                         _const_spec((TOKEN_TILE, D_MODEL))]
        args[0:1] = [x, x_tail]
        body = functools.partial(_ffn_first_kernel, base=base, n_lat_tiles=n_lat_tiles)
    return pl.pallas_call(
        body,
        out_shape=jax.ShapeDtypeStruct((n_tiles * TOKEN_TILE, D_MODEL), F32),
        grid=(n_tiles,),
        in_specs=in_specs,
        out_specs=_row_spec(D_MODEL),
        compiler_params=_params(("arbitrary",)),
        name="ffn",
    )(*args)


_C_F, _C_RQ, _C_RK, _C_RV, _C_RG = 0, 256, 512, 768, 1152
_C_DQ, _C_DK, _C_DV = 1536, 1920, 2304
_C_RQR, _C_RKR, _C_DQR, _C_DKR = 2688, 2944, 3200, 3584
PROJ_W = 3968


def _rotate_half_cols(w, block):
    depth, d, width = w.shape
    w5 = w.reshape(depth, d, width // block, 2, block // 2)
    return jnp.concatenate([-w5[:, :, :, 1:2], w5[:, :, :, 0:1]], axis=3).reshape(depth, d, width)


def _prep_proj_weights(w_in):
    cuts = np.cumsum([F_W, RET_QW, RET_QW, RET_VW, RET_VW, DIFF_QW, DIFF_QW, DIFF_VW])
    wf, wrq, wrk, wrv, wrg, wdq, wdk, wdv, wgt = jnp.split(w_in, cuts, axis=2)
    z = jnp.zeros(w_in.shape[:2] + (RET_QP - RET_QW,), w_in.dtype)
    parts = [wf, wrq, z, wrk, z, wrv, wrg, wdq, wdk, wdv,
             _rotate_half_cols(wrq, RET_QK), z, _rotate_half_cols(wrk, RET_QK), z,
             _rotate_half_cols(wdq, DIFF_QK // 2), _rotate_half_cols(wdk, DIFF_QK // 2)]
    return jnp.concatenate(parts, axis=2).astype(BF16), wgt.astype(BF16)


def _rope_tables(seq, n_rows):
    pos = jnp.arange(seq, dtype=F32)
    inv_r = ROPE_BASE ** (-jnp.arange(0, RET_QK, 2, dtype=F32) / RET_QK)
    ang_r = pos[:, None] * inv_r[None, :]
    cos_r = jnp.tile(jnp.cos(ang_r), (1, 2 * LANES // RET_QK))
    sin_r = jnp.tile(jnp.sin(ang_r), (1, 2 * LANES // RET_QK))
    rows = jnp.repeat(jnp.arange(seq // GRID_W, dtype=F32), GRID_W)
    cols = jnp.tile(jnp.arange(GRID_W, dtype=F32), seq // GRID_W)
    dim = DIFF_QK // 2
    inv_d = ROPE_BASE ** (-jnp.arange(0, dim, 2, dtype=F32) / dim)
    a_row = rows[:, None] * inv_d[None, :]
    a_col = cols[:, None] * inv_d[None, :]
    cos_hm = jnp.concatenate([jnp.cos(a_row)] * 2 + [jnp.cos(a_col)] * 2, axis=1)
    sin_hm = jnp.concatenate([jnp.sin(a_row)] * 2 + [jnp.sin(a_col)] * 2, axis=1)
    cos_d = jnp.tile(cos_hm, (1, LANES // DIFF_QK))
    sin_d = jnp.tile(sin_hm, (1, LANES // DIFF_QK))

    def finish(t, fill):
        return jnp.pad(t, ((0, n_rows - seq), (0, 0)), constant_values=fill)

    return finish(cos_r, 1.0), finish(sin_r, 0.0), finish(cos_d, 1.0), finish(sin_d, 0.0)


def _proj_kernel(x_ref, mod_ref, g_ref, w_ref, cr_ref, sr_ref, cd_ref, sd_ref,
                 uf_ref, rq_ref, rkt_ref, rv_ref, rg_ref, dqt_ref, dk_ref, vta_ref):
    x = x_ref[...]
    hb = _norm_mod(x, g_ref[...], mod_ref[0, 3:4, :], mod_ref[0, 4:5, :]).astype(BF16)
    p = _dot(hb, w_ref[...])
    cr = jnp.concatenate([cr_ref[...]] * (RET_QP // LANES), axis=1)
    sr = jnp.concatenate([sr_ref[...]] * (RET_QP // LANES), axis=1)
    cd = jnp.concatenate([cd_ref[...]] * (DIFF_QW // LANES), axis=1)
    sd = jnp.concatenate([sd_ref[...]] * (DIFF_QW // LANES), axis=1)
    uf_ref[...] = p[:, _C_F:_C_F + F_W]
    rq = p[:, _C_RQ:_C_RQ + RET_QP] * cr + p[:, _C_RQR:_C_RQR + RET_QP] * sr
    rk = p[:, _C_RK:_C_RK + RET_QP] * cr + p[:, _C_RKR:_C_RKR + RET_QP] * sr
    rq_ref[...] = rq.astype(BF16)
    rkt_ref[...] = (rk * (RET_QK ** -0.5)).T.astype(BF16)
    rv_ref[...] = p[:, _C_RV:_C_RV + RET_VW].astype(BF16)
    rg_ref[...] = p[:, _C_RG:_C_RG + RET_VW]
    dq = p[:, _C_DQ:_C_DQ + DIFF_QW] * cd + p[:, _C_DQR:_C_DQR + DIFF_QW] * sd
    dk = p[:, _C_DK:_C_DK + DIFF_QW] * cd + p[:, _C_DKR:_C_DKR + DIFF_QW] * sd
    dqt_ref[...] = (dq * ((DIFF_QK ** -0.5) * LOG2E)).T.astype(BF16)
    dk_ref[...] = dk.astype(BF16)
    vt = p[:, _C_DV:_C_DV + DIFF_VW].T
    ones = jnp.ones((ATTN_VT_ROWS - DIFF_V, x.shape[0]), F32)
    for h in range(DIFF_HEADS):
        vta_ref[h] = jnp.concatenate([vt[h * DIFF_V:(h + 1) * DIFF_V], ones], axis=0).astype(BF16)


def _proj_call(x, mods, g, w_aug, layer, tables, n_lat_tiles, n_tiles):
    tm = TOKEN_TILE
    t = n_tiles * tm
    out_shape = (
        jax.ShapeDtypeStruct((t, F_W), F32),
        jax.ShapeDtypeStruct((t, RET_QP), BF16),
        jax.ShapeDtypeStruct((RET_QP, t), BF16),
        jax.ShapeDtypeStruct((t, RET_VW), BF16),
        jax.ShapeDtypeStruct((t, RET_VW), F32),
        jax.ShapeDtypeStruct((DIFF_QW, t), BF16),
        jax.ShapeDtypeStruct((t, DIFF_QW), BF16),
        jax.ShapeDtypeStruct((DIFF_HEADS, ATTN_VT_ROWS, t), BF16),
    )
    out_specs = (
        _row_spec(F_W), _row_spec(RET_QP),
        pl.BlockSpec((RET_QP, tm), lambda i: (0, i)),
        _row_spec(RET_VW), _row_spec(RET_VW),
        pl.BlockSpec((DIFF_QW, tm), lambda i: (0, i)),
        _row_spec(DIFF_QW),
        pl.BlockSpec((DIFF_HEADS, ATTN_VT_ROWS, tm), lambda i: (0, 0, i)),
    )
    return pl.pallas_call(
        _proj_kernel,
        out_shape=out_shape,
        grid=(n_tiles,),
        in_specs=[
            _row_spec(D_MODEL), _mod_spec(n_lat_tiles), _const_spec((1, D_MODEL)),
            _stacked_spec((D_MODEL, PROJ_W), (layer,)),
            _row_spec(LANES), _row_spec(LANES), _row_spec(LANES), _row_spec(LANES),
        ],
        out_specs=out_specs,
        compiler_params=_params(("arbitrary",)),
        name="mixer_proj",
    )(x, mods, g.reshape(1, D_MODEL), w_aug, *tables)


def _dft_mats(n):
    k = np.arange(n)
    ang = 2.0 * np.pi * ((k[:, None] * k[None, :]) % n) / n
    return np.cos(ang).astype(np.float32), np.sin(ang).astype(np.float32)


def _channel_dft_mats():
    c, s = _dft_mats(FOURIER_CH)
    eye = np.eye(FOURIER_GROUPS, dtype=np.float32)
    return np.kron(eye, c), np.kron(eye, s)


def _fourier_stage1_kernel(x_ref, cc_ref, sc_ref, c1_ref, s1_ref, twc_ref, tws_ref, tr_ref, ti_ref, *, nb):
    cc, sc = _split_bf16(cc_ref[...]), _split_bf16(sc_ref[...])
    zr, zi = [], []
    for j in range(nb):
        u = _split_bf16(x_ref[:, j * F_W:(j + 1) * F_W])
        zr.append(_dot3_split(u, cc))
        zi.append(-_dot3_split(u, sc))
    zr = _split_bf16(jnp.concatenate(zr, axis=1) if nb > 1 else zr[0])
    zi = _split_bf16(jnp.concatenate(zi, axis=1) if nb > 1 else zi[0])
    c1, s1 = _split_bf16(c1_ref[...]), _split_bf16(s1_ref[...])
    tr = _dot3_split(c1, zr) + _dot3_split(s1, zi)
    ti = _dot3_split(c1, zi) - _dot3_split(s1, zr)
    twc, tws = twc_ref[0], tws_ref[0]
    for j in range(nb):
        cols = slice(j * F_W, (j + 1) * F_W)
        c, s = twc[:, j:j + 1], tws[:, j:j + 1]
        tr_ref[:, cols] = tr[:, cols] * c + ti[:, cols] * s
        ti_ref[:, cols] = ti[:, cols] * c - tr[:, cols] * s


def _fourier_stage2_kernel(tr_ref, ti_ref, c2_ref, s2_ref, o_ref, *, kb):
    c2, s2 = c2_ref[...], s2_ref[...]
    for j in range(kb):
        o_ref[:, j, :] = _dot3(c2, tr_ref[j]) + _dot3(s2, ti_ref[j])


def _fourier_latent(u_all, seq, twc, tws):
    n1, n2 = DFT_N1, seq // DFT_N1
    nb = _fourier_block(n2)
    kb = SUBLANES
    cc, sc = _channel_dft_mats()
    c1, s1 = _dft_mats(n1)
    c2, s2 = _dft_mats(n2)
    assert u_all.shape[0] % n2 == 0
    x2 = u_all.reshape(u_all.shape[0] // n2, n2 * F_W)
    blk = pl.BlockSpec((n1, nb * F_W), lambda i: (0, i))
    tw_blk = pl.BlockSpec((1, n1, nb), lambda i: (i, 0, 0))
    tr, ti = pl.pallas_call(
        functools.partial(_fourier_stage1_kernel, nb=nb),
        out_shape=(jax.ShapeDtypeStruct((n1, n2 * F_W), F32),) * 2,
        grid=(n2 // nb,),
        in_specs=[blk, _const_spec((F_W, F_W)), _const_spec((F_W, F_W)),
                  _const_spec((n1, n1)), _const_spec((n1, n1)), tw_blk, tw_blk],
        out_specs=(blk, blk),
        compiler_params=_params(("arbitrary",)),
        name="fourier_stage1",
    )(x2, cc, sc, c1, s1, twc, tws)
    tr3 = tr.reshape(n1, n2, F_W)
    ti3 = ti.reshape(n1, n2, F_W)
    o3 = pl.pallas_call(
        functools.partial(_fourier_stage2_kernel, kb=kb),
        out_shape=jax.ShapeDtypeStruct((n2, n1, F_W), F32),
        grid=(n1 // kb,),
        in_specs=[pl.BlockSpec((kb, n2, F_W), lambda i: (i, 0, 0))] * 2
        + [_const_spec((n2, n2)), _const_spec((n2, n2))],
        out_specs=pl.BlockSpec((n2, kb, F_W), lambda i: (0, i, 0)),
        compiler_params=_params(("arbitrary",)),
        name="fourier_stage2",
    )(tr3, ti3, c2, s2)
    return o3.reshape(seq, F_W)


def _twiddles(seq):
    n1, n2 = DFT_N1, seq // DFT_N1
    k1 = jnp.arange(n1, dtype=jnp.int32)[:, None]
    m2 = jnp.arange(n2, dtype=jnp.int32)[None, :]
    ang = (2.0 * math.pi / seq) * ((k1 * m2) % seq).astype(F32)
    scale = 1.0 / math.sqrt(seq * FOURIER_CH)
    nb = _fourier_block(n2)

    def blocked(t):
        return t.reshape(n1, n2 // nb, nb).transpose(1, 0, 2)

    return blocked(jnp.cos(ang) * scale), blocked(jnp.sin(ang) * scale)


def _fourier_block(n2):
    return min(SUBLANES, n2)


def _fourier_ctx_kernel(u_ref, cc_ref, sc_ref, cl_ref, sl_ref, o_ref, *, scale):
    u = u_ref[...]
    a = _dot3(u, cc_ref[...])
    b = _dot3(u, sc_ref[...])
    o_ref[...] = (_dot3(cl_ref[...], a) - _dot3(sl_ref[...], b)) * scale


def _fourier_ctx(u):
    n = u.shape[0]
    cc, sc = _channel_dft_mats()
    cl, sl = _dft_mats(n)
    return pl.pallas_call(
        functools.partial(_fourier_ctx_kernel, scale=1.0 / math.sqrt(n * FOURIER_CH)),
        out_shape=jax.ShapeDtypeStruct((n, F_W), F32),
        name="fourier_ctx",
    )(u, cc, sc, cl, sl)


def _ret_chunk_local(q, kt, v, d_ref, kdec, bdmask, hmask_ref, vmask_ref):
    pieces = []
    vparts = []
    for hp in range(RET_HEADS // 2):
        kpair = jnp.concatenate([kt * hmask_ref[2 * hp], kt * hmask_ref[2 * hp + 1]], axis=1)
        pieces.append((_dot(q, kpair) * d_ref[hp]).astype(BF16))
    for h in range(RET_HEADS):
        vparts.append(v * vmask_ref[h])
    inner = jnp.concatenate(pieces, axis=1)
    vbd = jnp.concatenate(vparts, axis=0)
    kd = (kt.astype(F32) * kdec).astype(BF16)
    return _dot(inner, vbd), bdmask * _dot(kd, v)


def _ret_dir(q_ref, kt_ref, v_ref, o_ref, s_ref, d_ref, qdec, kdec, cd, bdmask, hmask_ref, vmask_ref, order):
    c = RET_CHUNK
    local = []
    for half in order:
        rows = slice(half * c, (half + 1) * c)
        q = q_ref[rows, :]
        local.append((rows, q) + _ret_chunk_local(q, kt_ref[:, rows], v_ref[rows, :], d_ref, kdec, bdmask,
                                                  hmask_ref, vmask_ref))
    s = s_ref[...]
    for rows, q, o_intra, inc in local:
        o_ref[rows, :] = o_intra + _dot(q, s.astype(BF16)) * qdec
        s = s * cd + inc
    s_ref[...] = s


def _ret_kernel(logg_ref, lgv_ref, lgk_ref, bdmask_ref, hmask_ref, vmask_ref,
                qf_ref, ktf_ref, vf_ref, qb_ref, ktb_ref, vb_ref,
                of_ref, ob_ref,
                sf_ref, sb_ref, df_ref, db_ref, qdf_ref, qdb_ref, kdf_ref, kdb_ref, cdf_ref, cdb_ref):
    c = RET_CHUNK

    @pl.when(pl.program_id(0) == 0)
    def _():
        sf_ref[...] = jnp.zeros_like(sf_ref)
        sb_ref[...] = jnp.zeros_like(sb_ref)
        ii = lax.broadcasted_iota(jnp.int32, (c, c), 0).astype(F32)
        jj = lax.broadcasted_iota(jnp.int32, (c, c), 1).astype(F32)
        for h in range(RET_HEADS):
            half = slice((h % 2) * c, (h % 2 + 1) * c)
            df_ref[h // 2, :, half] = jnp.where(ii >= jj, jnp.exp(logg_ref[0, h] * jnp.maximum(ii - jj, 0.0)), 0.0)
            db_ref[h // 2, :, half] = jnp.where(jj >= ii, jnp.exp(logg_ref[1, h] * jnp.maximum(jj - ii, 0.0)), 0.0)
        ri = lax.broadcasted_iota(jnp.int32, (c, RET_VW), 0).astype(F32)
        qdf_ref[...] = jnp.exp(lgv_ref[0] * (ri + 1.0))
        qdb_ref[...] = jnp.exp(lgv_ref[1] * (c - ri))
        cj = lax.broadcasted_iota(jnp.int32, (RET_QP, c), 1).astype(F32)
        kdf_ref[...] = jnp.exp(lgk_ref[0] * (c - 1.0 - cj))
        kdb_ref[...] = jnp.exp(lgk_ref[1] * cj)
        cdf_ref[...] = jnp.exp(lgv_ref[0] * float(c))
        cdb_ref[...] = jnp.exp(lgv_ref[1] * float(c))

    bdmask = bdmask_ref[...]
    halves = list(range(RET_STEP_CHUNKS))
    _ret_dir(qf_ref, ktf_ref, vf_ref, of_ref, sf_ref, df_ref,
             qdf_ref[...], kdf_ref[...], cdf_ref[...], bdmask, hmask_ref, vmask_ref, halves)
    _ret_dir(qb_ref, ktb_ref, vb_ref, ob_ref, sb_ref, db_ref,
             qdb_ref[...], kdb_ref[...], cdb_ref[...], bdmask, hmask_ref, vmask_ref, halves[::-1])


def _ret_masks():
    hm = np.zeros((RET_HEADS, RET_QP, RET_CHUNK), np.float32)
    vm = np.zeros((RET_HEADS, 1, RET_VW), np.float32)
    bd = np.zeros((RET_QP, RET_VW), np.float32)
    for h in range(RET_HEADS):
        hm[h, h * RET_QK:(h + 1) * RET_QK, :] = 1.0
        vm[h, 0, h * RET_V:(h + 1) * RET_V] = 1.0
        bd[h * RET_QK:(h + 1) * RET_QK, h * RET_V:(h + 1) * RET_V] = 1.0
    return jnp.asarray(bd), jnp.asarray(hm, BF16), jnp.asarray(vm, BF16)


def _retention_call(rq, rkt, rv, log_g2, n_lat, n_ctx, n_pad):
    c = RET_CHUNK
    g = RET_STEP_CHUNKS
    assert n_lat % g == 0 and n_ctx % g == 0 and n_pad % g == 0
    n_lat, n_ctx, n_pad = n_lat // g, n_ctx // g, n_pad // g
    n_real = n_lat + n_ctx
    n = n_real + n_pad
    rows = g * c

    def fwd(i):
        return jnp.where(i < n_ctx, n_lat + i, jnp.where(i < n_real, i - n_ctx, i))

    def bwd(i):
        return jnp.where(i < n_real, n_real - 1 - i, i)

    lgv = jnp.repeat(log_g2, RET_V, axis=1).reshape(2, 1, RET_VW)
    lgk = jnp.pad(jnp.repeat(log_g2, RET_QK, axis=1), ((0, 0), (0, RET_QP - RET_QW)))
    lgk = jnp.broadcast_to(lgk[:, :, None], (2, RET_QP, c))
    bd, hm, vm = _ret_masks()

    def specs(ix):
        return [pl.BlockSpec((rows, RET_QP), lambda i: (ix(i), 0)),
                pl.BlockSpec((RET_QP, rows), lambda i: (0, ix(i))),
                pl.BlockSpec((rows, RET_VW), lambda i: (ix(i), 0))]

    vmem = pltpu.VMEM
    return pl.pallas_call(
        _ret_kernel,
        out_shape=(jax.ShapeDtypeStruct((n * rows, RET_VW), F32),) * 2,
        grid=(n,),
        in_specs=[pl.BlockSpec(memory_space=pltpu.SMEM),
                  _const_spec((2, 1, RET_VW)), _const_spec((2, RET_QP, c)),
                  _const_spec((RET_QP, RET_VW)), _const_spec((RET_HEADS, RET_QP, c)),
                  _const_spec((RET_HEADS, 1, RET_VW))] + specs(fwd) + specs(bwd),
        out_specs=(pl.BlockSpec((rows, RET_VW), lambda i: (fwd(i), 0)),
                   pl.BlockSpec((rows, RET_VW), lambda i: (bwd(i), 0))),
        scratch_shapes=[vmem((RET_QP, RET_VW), F32), vmem((RET_QP, RET_VW), F32),
                        vmem((RET_HEADS // 2, c, 2 * c), F32), vmem((RET_HEADS // 2, c, 2 * c), F32),
                        vmem((c, RET_VW), F32), vmem((c, RET_VW), F32),
                        vmem((RET_QP, c), F32), vmem((RET_QP, c), F32),
                        vmem((1, RET_VW), F32), vmem((1, RET_VW), F32)],
        compiler_params=_params(("arbitrary",)),
        name="retention",
    )(log_g2, lgv, lgk, bd, hm, vm, rq, rkt, rv, rq, rkt, rv)


def _chunk_scores(k_chunk, qm, s_out_ref, rows, run_max):
    s = _dot(k_chunk, qm)
    s_out_ref[rows, :] = s
    cm = jnp.max(s.reshape(s.shape[0] // SUBLANES, SUBLANES, s.shape[1]), axis=0)
    return cm if run_max is None else jnp.maximum(run_max, cm)


def _mask_map(qt, j):
    row = lax.broadcasted_iota(jnp.int32, qt.shape, 0)
    lo = j * DIFF_QK
    return jnp.where((row >= lo) & (row < lo + DIFF_QK), qt, jnp.zeros_like(qt))


def _attn_kernel(lam_ref, qt_ref, qtn_ref, k0_ref, kn_ref, vt_ref, subln_ref, o_ref,
                 qm_ref, m_ref, acc_ref, s_ref, mx_ref, kcur_ref, *, out_scale, key_chunk):
    ki = pl.program_id(2)
    is_last = ki == pl.num_programs(2) - 1
    tk = kn_ref.shape[0]
    n_chunks = tk // key_chunk

    def chunk(c):
        return slice(c * key_chunk, (c + 1) * key_chunk)

    @pl.when(ki == 0)
    def _():
        qt = qt_ref[...]
        for j in range(4):
            qm_ref[j] = _mask_map(qt, j)
        m_ref[...] = jnp.full(m_ref.shape, -jnp.inf, F32)
        acc_ref[...] = jnp.zeros_like(acc_ref)

    @pl.when((ki == 0) & (pl.program_id(0) == 0) & (pl.program_id(1) == 0))
    def _():
        kcur_ref[...] = k0_ref[...]
        run = None
        for c in range(n_chunks):
            run = _chunk_scores(k0_ref[chunk(c), :], qm_ref[0], s_ref.at[0], chunk(c), run)
        mx_ref[...] = run

    q_ahead = jnp.where(is_last, _mask_map(qtn_ref[...], 0), qm_ref[0])

    run = mx_ref[...]
    for u in range(4):
        cur, nxt = u % 2, (u + 1) % 2
        m_prev = m_ref[u]
        m_new = jnp.maximum(m_prev, jnp.max(run, axis=0, keepdims=True))
        alpha = jnp.exp2(m_prev - m_new)
        vt = vt_ref[u // 2]
        run = None
        acc = None
        p_prev = None
        for c in range(n_chunks):
            if u < 3:
                run = _chunk_scores(kcur_ref[chunk(c), :], qm_ref[u + 1], s_ref.at[nxt], chunk(c), run)
            else:
                run = _chunk_scores(kn_ref[chunk(c), :], q_ahead, s_ref.at[nxt], chunk(c), run)
            p = jnp.exp2(s_ref[cur, chunk(c), :] - m_new).astype(BF16)
            if p_prev is not None:
                pv = _dot(vt[:, chunk(c - 1)], p_prev)
                acc = pv if acc is None else acc + pv
            p_prev = p
        pv = _dot(vt[:, chunk(n_chunks - 1)], p_prev)
        acc = pv if acc is None else acc + pv
        acc_ref[u] = alpha * acc_ref[u] + acc
        m_ref[u] = m_new
    mx_ref[...] = run
    kcur_ref[...] = kn_ref[...]

    @pl.when(is_last)
    def _():
        lam = lam_ref[0]
        outs = []
        for hl in range(2):
            a1 = acc_ref[2 * hl]
            a2 = acc_ref[2 * hl + 1]
            o = (a1[:DIFF_V] / a1[DIFF_V:DIFF_V + 1]
                 - lam * (a2[:DIFF_V] / a2[DIFF_V:DIFF_V + 1]))
            ms = jnp.mean(o * o, axis=0, keepdims=True)
            outs.append(o * lax.rsqrt(ms + EPS) * subln_ref[...] * out_scale)
        o_ref[...] = jnp.concatenate(outs, axis=0)


def _attn_call(dqt, dk, vt_aug, lam, subln, lam_init, n_q, tq, tk, nk, q_off=0, k_off=0):
    n_hp = DIFF_HEADS // 2
    n_i = n_q // tq

    def next_pair(h, i):
        wrap = i + 1 >= n_i
        return jnp.where(wrap, jnp.minimum(h + 1, n_hp - 1), h), jnp.where(wrap, 0, i + 1)

    def q_next_map(h, i, k):
        hn, i_n = next_pair(h, i)
        return hn, q_off + i_n

    def k_next_map(h, i, k):
        last = k + 1 >= nk
        return k_off + jnp.where(last, 0, k + 1), jnp.where(last, next_pair(h, i)[0], h)

    return pl.pallas_call(
        functools.partial(_attn_kernel, out_scale=1.0 - lam_init, key_chunk=min(tk, ATTN_KEY_CHUNK)),
        out_shape=jax.ShapeDtypeStruct((DIFF_VW, n_q), F32),
        grid=(n_hp, n_i, nk),
        in_specs=[
            pl.BlockSpec(memory_space=pltpu.SMEM),
            pl.BlockSpec((LANES, tq), lambda h, i, k: (h, q_off + i)),
            pl.BlockSpec((LANES, tq), q_next_map),
            pl.BlockSpec((tk, LANES), lambda h, i, k: (k_off, 0)),
            pl.BlockSpec((tk, LANES), k_next_map),
            pl.BlockSpec((2, ATTN_VT_ROWS, tk), lambda h, i, k: (h, 0, k_off + k)),
            pl.BlockSpec((DIFF_V, 1), lambda h, i, k: (0, 0)),
        ],
        out_specs=pl.BlockSpec((LANES, tq), lambda h, i, k: (h, i)),
        scratch_shapes=[pltpu.VMEM((4, LANES, tq), BF16),
                        pltpu.VMEM((4, 1, tq), F32),
                        pltpu.VMEM((4, ATTN_VT_ROWS, tq), F32),
                        pltpu.VMEM((2, tk, tq), F32),
                        pltpu.VMEM((SUBLANES, tq), F32),
                        pltpu.VMEM((tk, LANES), BF16)],
        compiler_params=_params(("arbitrary", "arbitrary", "arbitrary")),
        name="diff_attn",
    )(lam, dqt, dqt, dk, dk, vt_aug, subln.reshape(DIFF_V, 1))


def _merge_ffn_kernel(x_ref, mod_ref, g_ref, flat_ref, ftail_ref, of_ref, ob_ref, rg_ref, dlat_ref, dtail_ref,
                      wgt_ref, wbf_ref, wbr_ref, wbd_ref, wo_ref, bd_ref, g2_ref, w1_ref, w3_ref, w2_ref, *rest,
                      n_lat_tiles, final):
    x2 = _merge_tile(x_ref, mod_ref, g_ref, flat_ref, ftail_ref, of_ref, ob_ref, rg_ref, dlat_ref, dtail_ref,
                     wgt_ref, wbf_ref, wbr_ref, wbd_ref, wo_ref, bd_ref, n_lat_tiles)
    y = _ffn_tile(x2, mod_ref, g2_ref, w1_ref, w3_ref, w2_ref, 6)
    if final:
        fg_ref, o_ref = rest
        ms = jnp.mean(y * y, axis=-1, keepdims=True)
        o_ref[...] = y * lax.rsqrt(ms + EPS) * fg_ref[...]
    else:
        (o_ref,) = rest
        o_ref[...] = y


def _merge_tile(x_ref, mod_ref, g_ref, flat_ref, ftail_ref, of_ref, ob_ref, rg_ref, dlat_ref, dtail_ref,
                wgt_ref, wbf_ref, wbr_ref, wbd_ref, wo_ref, bd_ref, n_lat_tiles):
    x = x_ref[...]
    hb = _norm_mod(x, g_ref[...], mod_ref[0, 3:4, :], mod_ref[0, 4:5, :]).astype(BF16)
    gates = _sigmoid(_dot(hb, wgt_ref[...]))
    is_tail = pl.program_id(0) >= n_lat_tiles
    f = jnp.where(is_tail, ftail_ref[...], flat_ref[...])
    d = jnp.where(is_tail, dtail_ref[...], dlat_ref[...]).T
    r = of_ref[...] + ob_ref[...]
    rr_hi, rr_lo = _split_bf16(r * r)
    ms = _dot(rr_hi, bd_ref[...]) + _dot(rr_lo, bd_ref[...])
    rg = rg_ref[...]
    yr = r * lax.rsqrt(ms + EPS) * (rg * _sigmoid(rg))
    mixed = (gates[:, :D_MODEL] * _dot(f.astype(BF16), wbf_ref[...])
             + gates[:, D_MODEL:2 * D_MODEL] * _dot(yr.astype(BF16), wbr_ref[...])
             + gates[:, 2 * D_MODEL:] * _dot(d.astype(BF16), wbd_ref[...]))
    y = _dot(mixed.astype(BF16), wo_ref[...])
    return x + mod_ref[0, 5:6, :] * y


def _merge_call(x, mods, g, f_lat, f_tail, o_f, o_b, rg, d_lat, d_tail, wgt, wbf, wbr, wbd, wo, layer,
                g2, w1, w3, w2, n_lat_tiles, n_tiles, final_g=None):
    tm = TOKEN_TILE
    widx = (layer, 1)
    extra_specs = [_const_spec((1, D_MODEL)), _stacked_spec((D_MODEL, D_FF), widx),
                   _stacked_spec((D_MODEL, D_FF), widx), _stacked_spec((D_FF, D_MODEL), widx)]
    extra_args = [g2.reshape(1, D_MODEL), w1, w3, w2]
    if final_g is not None:
        extra_specs.append(_const_spec((1, D_MODEL)))
        extra_args.append(final_g.reshape(1, D_MODEL))
    bd = np.kron(np.eye(RET_HEADS, dtype=np.float32), np.full((RET_V, RET_V), 1.0 / RET_V, np.float32))

    def lat_spec(w):
        return pl.BlockSpec((tm, w), lambda i: (jnp.minimum(i, n_lat_tiles - 1), 0))

    return pl.pallas_call(
        functools.partial(_merge_ffn_kernel, n_lat_tiles=n_lat_tiles, final=final_g is not None),
        out_shape=jax.ShapeDtypeStruct((n_tiles * tm, D_MODEL), F32),
        grid=(n_tiles,),
        in_specs=[
            _row_spec(D_MODEL), _mod_spec(n_lat_tiles), _const_spec((1, D_MODEL)),
            lat_spec(F_W), _const_spec((tm, F_W)),
            _row_spec(RET_VW), _row_spec(RET_VW), _row_spec(RET_VW),
            pl.BlockSpec((DIFF_VW, tm), lambda i: (0, jnp.minimum(i, n_lat_tiles - 1))),
            _const_spec((DIFF_VW, tm)),
            _stacked_spec((D_MODEL, GATE_W), (layer,)), _stacked_spec((F_W, D_MODEL), (layer,)),
            _stacked_spec((RET_VW, D_MODEL), (layer,)), _stacked_spec((DIFF_VW, D_MODEL), (layer,)),
            _stacked_spec((D_MODEL, D_MODEL), (layer,)), _const_spec((RET_VW, RET_VW)),
        ] + extra_specs,
        out_specs=_row_spec(D_MODEL),
        compiler_params=_params(("arbitrary",)),
        name="merge_ffn",
    )(x, mods, g.reshape(1, D_MODEL), f_lat, f_tail, o_f, o_b, rg, d_lat, d_tail, wgt, wbf, wbr, wbd, wo,
      jnp.asarray(bd, BF16), *extra_args)


def _pick_tile(n, candidates):
    for c in candidates:
        if n % c == 0:
            return c
    raise ValueError(f"no tile for {n}")


def kernel(x, c, ctx, c_ctx, w_ada, b_ada, norm_g, ffn_w1, ffn_w3, ffn_w2, w_in, ret_decay_logit,
           diff_lambda, diff_subln, w_branch_f, w_branch_r, w_branch_d, w_out, final_g):
    batch, seq, d = x.shape
    ctx_len = ctx.shape[1]
    tm = TOKEN_TILE
    assert batch == 1 and d == D_MODEL
    assert seq % max(DFT_N1 * SUBLANES, tm) == 0 and seq % ctx_len == 0
    assert ctx_len % (RET_STEP_CHUNKS * RET_CHUNK) == 0 and ctx_len % ATTN_KEY_CHUNK == 0 and ctx_len <= tm
    total = seq + ctx_len
    n_lat_tiles = seq // tm
    n_tiles = n_lat_tiles + 1
    n_rows = n_tiles * tm
    n_lat_chunks = seq // RET_CHUNK
    n_ctx_chunks = ctx_len // RET_CHUNK
    n_pad_chunks = (n_rows - total) // RET_CHUNK

    cc = jnp.zeros((SUBLANES, D_MODEL), F32).at[0].set(c[0]).at[1].set(c_ctx)
    mods_all = _ada_call(cc, w_ada, b_ada)[:, :2].reshape(DEPTH, 2, N_MOD, D_MODEL)

    tables = _rope_tables(seq, n_rows)
    twc, tws = _twiddles(seq)
    log_g2_all = jax.nn.log_sigmoid(ret_decay_logit.astype(F32))
    lv = diff_lambda.astype(F32)
    w_aug_all, wgt_all = _prep_proj_weights(w_in)
    w1_all, w3_all, w2_all = ffn_w1.astype(BF16), ffn_w3.astype(BF16), ffn_w2.astype(BF16)
    wbf_all, wbr_all = w_branch_f.astype(BF16), w_branch_r.astype(BF16)
    wbd_all, wo_all = w_branch_d.astype(BF16), w_out.astype(BF16)

    tq = _pick_tile(seq, (512, 256, 128))
    tk = _pick_tile(total, (3328, 1280, 640, 256, 128))
    tail_pad = ((0, tm - ctx_len), (0, 0))

    xs = x[0]
    x_tail = jnp.pad(ctx[0], tail_pad)
    for l in range(DEPTH):
        last = l == DEPTH - 1
        lam_init = 0.8 - 0.6 * math.exp(-0.3 * l)
        mods = mods_all[l]
        lam = (jnp.exp(jnp.sum(lv[l, 0] * lv[l, 1])) - jnp.exp(jnp.sum(lv[l, 2] * lv[l, 3]))
               + lam_init).reshape(1)

        xs = _ffn_call(xs, mods, norm_g[l, 0], w1_all, w3_all, w2_all, (l, 0), 0, n_lat_tiles, n_tiles,
                       x_tail=x_tail if l == 0 else None)

        uf, rq, rkt, rv, rg, dqt, dk, vt_aug = _proj_call(xs, mods, norm_g[l, 1], w_aug_all, l, tables,
                                                           n_lat_tiles, n_tiles)

        f_lat = _fourier_latent(uf, seq, twc, tws)
        o_f, o_b = _retention_call(rq, rkt, rv, log_g2_all[l], n_lat_chunks, n_ctx_chunks, n_pad_chunks)
        d_lat = _attn_call(dqt, dk, vt_aug, lam, diff_subln[l], lam_init, seq, tq, tk, total // tk)

        if last:
            f_tail = jnp.zeros((tm, F_W), F32)
            d_tail = jnp.zeros((DIFF_VW, tm), F32)
            n_out = n_lat_tiles
        else:
            f_tail = jnp.pad(_fourier_ctx(uf[seq:total]), tail_pad)
            d_ctx = _attn_call(dqt, dk, vt_aug, lam, diff_subln[l], lam_init, ctx_len, ctx_len, ctx_len, 1,
                               q_off=seq // ctx_len, k_off=seq // ctx_len)
            d_tail = jnp.pad(d_ctx, tail_pad[::-1])
            n_out = n_tiles

        xs = _merge_call(xs, mods, norm_g[l, 1], f_lat, f_tail, o_f, o_b, rg, d_lat, d_tail, wgt_all,
                         wbf_all, wbr_all, wbd_all, wo_all, l, norm_g[l, 2], w1_all, w3_all, w2_all,
                         n_lat_tiles, n_out, final_g=final_g if last else None)

    return xs.reshape(1, seq, D_MODEL)
```

```python
import functools
import math

import numpy as np
import jax
import jax.numpy as jnp
from jax import lax
from jax.experimental import pallas as pl
from jax.experimental.pallas import tpu as pltpu

D_MODEL = 1024
DEPTH = 4
GRID_W = 64
D_FF = 2816
N_MOD = 9
FOURIER_GROUPS = 4
FOURIER_CH = 64
RET_HEADS = 6
RET_QK = 32
RET_V = 64
RET_CHUNK = 128
DIFF_HEADS = 6
DIFF_QK = 32
DIFF_V = 64
ROPE_BASE = 10000.0
EPS = 1e-6
F_W = FOURIER_GROUPS * FOURIER_CH
RET_QW = RET_HEADS * RET_QK
RET_VW = RET_HEADS * RET_V
DIFF_QW = DIFF_HEADS * 2 * DIFF_QK
DIFF_VW = DIFF_HEADS * DIFF_V
GATE_W = 3 * D_MODEL

LANES = 128
SUBLANES = 8
V7X_VMEM_BYTES = 64 * 1024 * 1024
VMEM_LIMIT_BYTES = V7X_VMEM_BYTES * 7 // 8
ADA_COL_TILE = 9 * LANES

RET_QP = 2 * LANES
RET_STEP_CHUNKS = 2
TOKEN_TILE = 512
DFT_N1 = 128
ATTN_KEY_CHUNK = 256
ATTN_VT_ROWS = DIFF_V + 16

BF16 = jnp.bfloat16
F32 = jnp.float32
LOG2E = math.log2(math.e)


def _dot(a, b):
    return jnp.dot(a, b, preferred_element_type=F32)


def _split_bf16(x):
    hi = x.astype(BF16)
    lo = (x - hi.astype(F32)).astype(BF16)
    return hi, lo


def _dot3_split(a, b):
    (ah, al), (bh, bl) = a, b
    return _dot(ah, bh) + _dot(al, bh) + _dot(ah, bl)


def _dot3(a, b):
    return _dot3_split(_split_bf16(a), _split_bf16(b))


def _norm_mod(x, g, shift, scale):
    ms = jnp.mean(x * x, axis=-1, keepdims=True)
    y = x * lax.rsqrt(ms + EPS) * g
    return y * (1.0 + scale) + shift


def _sigmoid(x):
    return 1.0 / (1.0 + jnp.exp(-x))


def _const_spec(shape):
    nd = len(shape)
    return pl.BlockSpec(shape, lambda *_: (0,) * nd, pipeline_mode=pl.Buffered(1))


def _stacked_spec(shape, lead):
    block = (None,) * len(lead) + tuple(shape)
    index = tuple(lead) + (0,) * len(shape)
    return pl.BlockSpec(block, lambda *_: index, pipeline_mode=pl.Buffered(1))


def _params(sem):
    return pltpu.CompilerParams(dimension_semantics=sem, vmem_limit_bytes=VMEM_LIMIT_BYTES)


def _row_spec(width):
    return pl.BlockSpec((TOKEN_TILE, width), lambda i: (i, 0))


def _ada_kernel(cc_ref, w_ref, b_ref, o_ref):
    cc = cc_ref[...]
    s = cc * _sigmoid(cc)
    o_ref[0] = _dot3(s, w_ref[0]) + b_ref[0]


def _ada_call(cc, w_ada, b_ada):
    depth, d, n = w_ada.shape
    tn = ADA_COL_TILE
    assert n % tn == 0 and cc.shape == (SUBLANES, d)
    return pl.pallas_call(
        _ada_kernel,
        out_shape=jax.ShapeDtypeStruct((depth, SUBLANES, n), F32),
        grid=(depth, n // tn),
        in_specs=[
            pl.BlockSpec((SUBLANES, d), lambda l, j: (0, 0)),
            pl.BlockSpec((1, d, tn), lambda l, j: (l, 0, j)),
            pl.BlockSpec((1, 1, tn), lambda l, j: (l, 0, j)),
        ],
        out_specs=pl.BlockSpec((1, SUBLANES, tn), lambda l, j: (l, 0, j)),
        compiler_params=_params(("arbitrary", "arbitrary")),
        name="adaln",
    )(cc, w_ada, b_ada.reshape(depth, 1, n))


def _mod_spec(n_lat_tiles):
    return pl.BlockSpec((1, N_MOD, D_MODEL), lambda i: (jnp.where(i >= n_lat_tiles, 1, 0), 0, 0))


def _ffn_tile(x, mod_ref, g_ref, w1_ref, w3_ref, w2_ref, base):
    shift = mod_ref[0, base:base + 1, :]
    scale = mod_ref[0, base + 1:base + 2, :]
    gate = mod_ref[0, base + 2:base + 3, :]
    hb = _norm_mod(x, g_ref[...], shift, scale).astype(BF16)
    a = _dot(hb, w1_ref[...])
    b = _dot(hb, w3_ref[...])
    u = (a * _sigmoid(a) * b).astype(BF16)
    return x + (0.5 * gate) * _dot(u, w2_ref[...])


def _ffn_kernel(x_ref, mod_ref, g_ref, w1_ref, w3_ref, w2_ref, o_ref, *, base):
    o_ref[...] = _ffn_tile(x_ref[...], mod_ref, g_ref, w1_ref, w3_ref, w2_ref, base)


def _ffn_first_kernel(xlat_ref, xtail_ref, mod_ref, g_ref, w1_ref, w3_ref, w2_ref, o_ref, *, base, n_lat_tiles):
    x = jnp.where(pl.program_id(0) >= n_lat_tiles, xtail_ref[...], xlat_ref[...])
    o_ref[...] = _ffn_tile(x, mod_ref, g_ref, w1_ref, w3_ref, w2_ref, base)


def _ffn_final_kernel(x_ref, mod_ref, g_ref, w1_ref, w3_ref, w2_ref, fg_ref, o_ref, *, base):
    y = _ffn_tile(x_ref[...], mod_ref, g_ref, w1_ref, w3_ref, w2_ref, base)
    ms = jnp.mean(y * y, axis=-1, keepdims=True)
    o_ref[...] = y * lax.rsqrt(ms + EPS) * fg_ref[...]


def _ffn_call(x, mods, g, w1, w3, w2, widx, base, n_lat_tiles, n_tiles, final_g=None, x_tail=None):
    in_specs = [
        _row_spec(D_MODEL), _mod_spec(n_lat_tiles), _const_spec((1, D_MODEL)),
        _stacked_spec((D_MODEL, D_FF), widx), _stacked_spec((D_MODEL, D_FF), widx),
        _stacked_spec((D_FF, D_MODEL), widx),
    ]
    args = [x, mods, g.reshape(1, D_MODEL), w1, w3, w2]
    body = functools.partial(_ffn_kernel, base=base)
    assert final_g is None or x_tail is None
    if final_g is not None:
        in_specs.append(_const_spec((1, D_MODEL)))
        args.append(final_g.reshape(1, D_MODEL))
        body = functools.partial(_ffn_final_kernel, base=base)
    if x_tail is not None:
        in_specs[0:1] = [pl.BlockSpec((TOKEN_TILE, D_MODEL), lambda i: (jnp.minimum(i, n_lat_tiles - 1), 0)),
                         _const_spec((TOKEN_TILE, D_MODEL))]
        args[0:1] = [x, x_tail]
        body = functools.partial(_ffn_first_kernel, base=base, n_lat_tiles=n_lat_tiles)
    return pl.pallas_call(
        body,
        out_shape=jax.ShapeDtypeStruct((n_tiles * TOKEN_TILE, D_MODEL), F32),
        grid=(n_tiles,),
        in_specs=in_specs,
        out_specs=_row_spec(D_MODEL),
        compiler_params=_params(("arbitrary",)),
        name="ffn",
    )(*args)


_C_F, _C_RQ, _C_RK, _C_RV, _C_RG = 0, 256, 512, 768, 1152
_C_DQ, _C_DK, _C_DV = 1536, 1920, 2304
_C_RQR, _C_RKR, _C_DQR, _C_DKR = 2688, 2944, 3200, 3584
PROJ_W = 3968


def _rotate_half_cols(w, block):
    depth, d, width = w.shape
    w5 = w.reshape(depth, d, width // block, 2, block // 2)
    return jnp.concatenate([-w5[:, :, :, 1:2], w5[:, :, :, 0:1]], axis=3).reshape(depth, d, width)


def _prep_proj_weights(w_in):
    cuts = np.cumsum([F_W, RET_QW, RET_QW, RET_VW, RET_VW, DIFF_QW, DIFF_QW, DIFF_VW])
    wf, wrq, wrk, wrv, wrg, wdq, wdk, wdv, wgt = jnp.split(w_in, cuts, axis=2)
    z = jnp.zeros(w_in.shape[:2] + (RET_QP - RET_QW,), w_in.dtype)
    parts = [wf, wrq, z, wrk, z, wrv, wrg, wdq, wdk, wdv,
             _rotate_half_cols(wrq, RET_QK), z, _rotate_half_cols(wrk, RET_QK), z,
             _rotate_half_cols(wdq, DIFF_QK // 2), _rotate_half_cols(wdk, DIFF_QK // 2)]
    return jnp.concatenate(parts, axis=2).astype(BF16), wgt.astype(BF16)


def _rope_tables(seq, n_rows):
    pos = jnp.arange(seq, dtype=F32)
    inv_r = ROPE_BASE ** (-jnp.arange(0, RET_QK, 2, dtype=F32) / RET_QK)
    ang_r = pos[:, None] * inv_r[None, :]
    cos_r = jnp.tile(jnp.cos(ang_r), (1, 2 * LANES // RET_QK))
    sin_r = jnp.tile(jnp.sin(ang_r), (1, 2 * LANES // RET_QK))
    rows = jnp.repeat(jnp.arange(seq // GRID_W, dtype=F32), GRID_W)
    cols = jnp.tile(jnp.arange(GRID_W, dtype=F32), seq // GRID_W)
    dim = DIFF_QK // 2
    inv_d = ROPE_BASE ** (-jnp.arange(0, dim, 2, dtype=F32) / dim)
    a_row = rows[:, None] * inv_d[None, :]
    a_col = cols[:, None] * inv_d[None, :]
    cos_hm = jnp.concatenate([jnp.cos(a_row)] * 2 + [jnp.cos(a_col)] * 2, axis=1)
    sin_hm = jnp.concatenate([jnp.sin(a_row)] * 2 + [jnp.sin(a_col)] * 2, axis=1)
    cos_d = jnp.tile(cos_hm, (1, LANES // DIFF_QK))
    sin_d = jnp.tile(sin_hm, (1, LANES // DIFF_QK))

    def finish(t, fill):
        return jnp.pad(t, ((0, n_rows - seq), (0, 0)), constant_values=fill)

    return finish(cos_r, 1.0), finish(sin_r, 0.0), finish(cos_d, 1.0), finish(sin_d, 0.0)


def _proj_kernel(x_ref, mod_ref, g_ref, w_ref, cr_ref, sr_ref, cd_ref, sd_ref,
                 uf_ref, rq_ref, rkt_ref, rv_ref, rg_ref, dqt_ref, dk_ref, vta_ref):
    x = x_ref[...]
    hb = _norm_mod(x, g_ref[...], mod_ref[0, 3:4, :], mod_ref[0, 4:5, :]).astype(BF16)
    p = _dot(hb, w_ref[...])
    cr = jnp.concatenate([cr_ref[...]] * (RET_QP // LANES), axis=1)
    sr = jnp.concatenate([sr_ref[...]] * (RET_QP // LANES), axis=1)
    cd = jnp.concatenate([cd_ref[...]] * (DIFF_QW // LANES), axis=1)
    sd = jnp.concatenate([sd_ref[...]] * (DIFF_QW // LANES), axis=1)
    uf_ref[...] = p[:, _C_F:_C_F + F_W]
    rq = p[:, _C_RQ:_C_RQ + RET_QP] * cr + p[:, _C_RQR:_C_RQR + RET_QP] * sr
    rk = p[:, _C_RK:_C_RK + RET_QP] * cr + p[:, _C_RKR:_C_RKR + RET_QP] * sr
    rq_ref[...] = rq.astype(BF16)
    rkt_ref[...] = (rk * (RET_QK ** -0.5)).T.astype(BF16)
    rv_ref[...] = p[:, _C_RV:_C_RV + RET_VW].astype(BF16)
    rg_ref[...] = p[:, _C_RG:_C_RG + RET_VW]
    dq = p[:, _C_DQ:_C_DQ + DIFF_QW] * cd + p[:, _C_DQR:_C_DQR + DIFF_QW] * sd
    dk = p[:, _C_DK:_C_DK + DIFF_QW] * cd + p[:, _C_DKR:_C_DKR + DIFF_QW] * sd
    dqt_ref[...] = (dq * ((DIFF_QK ** -0.5) * LOG2E)).T.astype(BF16)
    dk_ref[...] = dk.astype(BF16)
    vt = p[:, _C_DV:_C_DV + DIFF_VW].T
    ones = jnp.ones((ATTN_VT_ROWS - DIFF_V, x.shape[0]), F32)
    for h in range(DIFF_HEADS):
        vta_ref[h] = jnp.concatenate([vt[h * DIFF_V:(h + 1) * DIFF_V], ones], axis=0).astype(BF16)


def _proj_call(x, mods, g, w_aug, layer, tables, n_lat_tiles, n_tiles, ffn=None):
    tm = TOKEN_TILE
    t = n_tiles * tm
    out_shape = (
        jax.ShapeDtypeStruct((t, F_W), F32),
        jax.ShapeDtypeStruct((t, RET_QP), BF16),
        jax.ShapeDtypeStruct((RET_QP, t), BF16),
        jax.ShapeDtypeStruct((t, RET_VW), BF16),
        jax.ShapeDtypeStruct((t, RET_VW), F32),
        jax.ShapeDtypeStruct((DIFF_QW, t), BF16),
        jax.ShapeDtypeStruct((t, DIFF_QW), BF16),
        jax.ShapeDtypeStruct((DIFF_HEADS, ATTN_VT_ROWS, t), BF16),
    )
    out_specs = (
        _row_spec(F_W), _row_spec(RET_QP),
        pl.BlockSpec((RET_QP, tm), lambda i: (0, i)),
        _row_spec(RET_VW), _row_spec(RET_VW),
        pl.BlockSpec((DIFF_QW, tm), lambda i: (0, i)),
        _row_spec(DIFF_QW),
        pl.BlockSpec((DIFF_HEADS, ATTN_VT_ROWS, tm), lambda i: (0, 0, i)),
    )
    in_specs = [
        _row_spec(D_MODEL), _mod_spec(n_lat_tiles), _const_spec((1, D_MODEL)),
        _stacked_spec((D_MODEL, PROJ_W), (layer,)),
        _row_spec(LANES), _row_spec(LANES), _row_spec(LANES), _row_spec(LANES),
    ]
    args = [x, mods, g.reshape(1, D_MODEL), w_aug, *tables]
    body = _proj_kernel
    if ffn is not None:
        g0, w1, w3, w2 = ffn
        widx = (layer, 0)
        in_specs[2:2] = [_const_spec((1, D_MODEL)), _stacked_spec((D_MODEL, D_FF), widx),
                         _stacked_spec((D_MODEL, D_FF), widx), _stacked_spec((D_FF, D_MODEL), widx)]
        args[2:2] = [g0.reshape(1, D_MODEL), w1, w3, w2]
        out_shape = (jax.ShapeDtypeStruct((t, D_MODEL), F32),) + out_shape
        out_specs = (_row_spec(D_MODEL),) + out_specs
        body = _ffn_proj_kernel
    return pl.pallas_call(
        body,
        out_shape=out_shape,
        grid=(n_tiles,),
        in_specs=in_specs,
        out_specs=out_specs,
        compiler_params=_params(("arbitrary",)),
        name="mixer_proj",
    )(*args)


def _ffn_proj_kernel(x_ref, mod_ref, g0_ref, w1_ref, w3_ref, w2_ref, g_ref, w_ref, cr_ref, sr_ref, cd_ref, sd_ref,
                     x1_ref, *outs):
    x1_ref[...] = _ffn_tile(x_ref[...], mod_ref, g0_ref, w1_ref, w3_ref, w2_ref, 0)
    _proj_kernel(x1_ref, mod_ref, g_ref, w_ref, cr_ref, sr_ref, cd_ref, sd_ref, *outs)


def _dft_mats(n):
    k = np.arange(n)
    ang = 2.0 * np.pi * ((k[:, None] * k[None, :]) % n) / n
    return np.cos(ang).astype(np.float32), np.sin(ang).astype(np.float32)


def _channel_dft_mats():
    c, s = _dft_mats(FOURIER_CH)
    eye = np.eye(FOURIER_GROUPS, dtype=np.float32)
    return np.kron(eye, c), np.kron(eye, s)


def _fourier_stage1_kernel(x_ref, cc_ref, sc_ref, c1_ref, s1_ref, twc_ref, tws_ref, tr_ref, ti_ref, *, nb):
    cc, sc = _split_bf16(cc_ref[...]), _split_bf16(sc_ref[...])
    zr, zi = [], []
    for j in range(nb):
        u = _split_bf16(x_ref[:, j * F_W:(j + 1) * F_W])
        zr.append(_dot3_split(u, cc))
        zi.append(-_dot3_split(u, sc))
    zr = _split_bf16(jnp.concatenate(zr, axis=1) if nb > 1 else zr[0])
    zi = _split_bf16(jnp.concatenate(zi, axis=1) if nb > 1 else zi[0])
    c1, s1 = _split_bf16(c1_ref[...]), _split_bf16(s1_ref[...])
    tr = _dot3_split(c1, zr) + _dot3_split(s1, zi)
    ti = _dot3_split(c1, zi) - _dot3_split(s1, zr)
    twc, tws = twc_ref[0], tws_ref[0]
    for j in range(nb):
        cols = slice(j * F_W, (j + 1) * F_W)
        c, s = twc[:, j:j + 1], tws[:, j:j + 1]
        tr_ref[:, cols] = tr[:, cols] * c + ti[:, cols] * s
        ti_ref[:, cols] = ti[:, cols] * c - tr[:, cols] * s


def _fourier_stage2_kernel(tr_ref, ti_ref, c2_ref, s2_ref, o_ref, *, kb):
    c2, s2 = c2_ref[...], s2_ref[...]
    for j in range(kb):
        o_ref[:, j, :] = _dot3(c2, tr_ref[j]) + _dot3(s2, ti_ref[j])


def _fourier_latent(u_all, seq, twc, tws):
    n1, n2 = DFT_N1, seq // DFT_N1
    nb = _fourier_block(n2)
    kb = SUBLANES
    cc, sc = _channel_dft_mats()
    c1, s1 = _dft_mats(n1)
    c2, s2 = _dft_mats(n2)
    assert u_all.shape[0] % n2 == 0
    x2 = u_all.reshape(u_all.shape[0] // n2, n2 * F_W)
    blk = pl.BlockSpec((n1, nb * F_W), lambda i: (0, i))
    tw_blk = pl.BlockSpec((1, n1, nb), lambda i: (i, 0, 0))
    tr, ti = pl.pallas_call(
        functools.partial(_fourier_stage1_kernel, nb=nb),
        out_shape=(jax.ShapeDtypeStruct((n1, n2 * F_W), F32),) * 2,
        grid=(n2 // nb,),
        in_specs=[blk, _const_spec((F_W, F_W)), _const_spec((F_W, F_W)),
                  _const_spec((n1, n1)), _const_spec((n1, n1)), tw_blk, tw_blk],
        out_specs=(blk, blk),
        compiler_params=_params(("arbitrary",)),
        name="fourier_stage1",
    )(x2, cc, sc, c1, s1, twc, tws)
    tr3 = tr.reshape(n1, n2, F_W)
    ti3 = ti.reshape(n1, n2, F_W)
    o3 = pl.pallas_call(
        functools.partial(_fourier_stage2_kernel, kb=kb),
        out_shape=jax.ShapeDtypeStruct((n2, n1, F_W), F32),
        grid=(n1 // kb,),
        in_specs=[pl.BlockSpec((kb, n2, F_W), lambda i: (i, 0, 0))] * 2
        + [_const_spec((n2, n2)), _const_spec((n2, n2))],
        out_specs=pl.BlockSpec((n2, kb, F_W), lambda i: (0, i, 0)),
        compiler_params=_params(("arbitrary",)),
        name="fourier_stage2",
    )(tr3, ti3, c2, s2)
    return o3.reshape(seq, F_W)


def _twiddles(seq):
    n1, n2 = DFT_N1, seq // DFT_N1
    k1 = jnp.arange(n1, dtype=jnp.int32)[:, None]
    m2 = jnp.arange(n2, dtype=jnp.int32)[None, :]
    ang = (2.0 * math.pi / seq) * ((k1 * m2) % seq).astype(F32)
    scale = 1.0 / math.sqrt(seq * FOURIER_CH)
    nb = _fourier_block(n2)

    def blocked(t):
        return t.reshape(n1, n2 // nb, nb).transpose(1, 0, 2)

    return blocked(jnp.cos(ang) * scale), blocked(jnp.sin(ang) * scale)


def _fourier_block(n2):
    return min(SUBLANES, n2)


def _fourier_ctx_kernel(u_ref, cc_ref, sc_ref, cl_ref, sl_ref, o_ref, *, scale):
    u = u_ref[...]
    a = _dot3(u, cc_ref[...])
    b = _dot3(u, sc_ref[...])
    o_ref[...] = (_dot3(cl_ref[...], a) - _dot3(sl_ref[...], b)) * scale


def _fourier_ctx(u):
    n = u.shape[0]
    cc, sc = _channel_dft_mats()
    cl, sl = _dft_mats(n)
    return pl.pallas_call(
        functools.partial(_fourier_ctx_kernel, scale=1.0 / math.sqrt(n * FOURIER_CH)),
        out_shape=jax.ShapeDtypeStruct((n, F_W), F32),
        name="fourier_ctx",
    )(u, cc, sc, cl, sl)


def _ret_chunk_local(q, kt, v, d_ref, kdec, bdmask, hmask_ref, vmask_ref):
    pieces = []
    vparts = []
    for hp in range(RET_HEADS // 2):
        kpair = jnp.concatenate([kt * hmask_ref[2 * hp], kt * hmask_ref[2 * hp + 1]], axis=1)
        pieces.append((_dot(q, kpair) * d_ref[hp]).astype(BF16))
    for h in range(RET_HEADS):
        vparts.append(v * vmask_ref[h])
    inner = jnp.concatenate(pieces, axis=1)
    vbd = jnp.concatenate(vparts, axis=0)
    kd = (kt.astype(F32) * kdec).astype(BF16)
    return _dot(inner, vbd), bdmask * _dot(kd, v)


def _ret_dir(q_ref, kt_ref, v_ref, o_ref, s_ref, d_ref, qdec, kdec, cd, bdmask, hmask_ref, vmask_ref, order):
    c = RET_CHUNK
    local = []
    for half in order:
        rows = slice(half * c, (half + 1) * c)
        q = q_ref[rows, :]
        local.append((rows, q) + _ret_chunk_local(q, kt_ref[:, rows], v_ref[rows, :], d_ref, kdec, bdmask,
                                                  hmask_ref, vmask_ref))
    s = s_ref[...]
    for rows, q, o_intra, inc in local:
        o_ref[rows, :] = o_intra + _dot(q, s.astype(BF16)) * qdec
        s = s * cd + inc
    s_ref[...] = s


def _ret_kernel(logg_ref, lgv_ref, lgk_ref, bdmask_ref, hmask_ref, vmask_ref,
                qf_ref, ktf_ref, vf_ref, qb_ref, ktb_ref, vb_ref,
                of_ref, ob_ref,
                sf_ref, sb_ref, df_ref, db_ref, qdf_ref, qdb_ref, kdf_ref, kdb_ref, cdf_ref, cdb_ref):
    c = RET_CHUNK

    @pl.when(pl.program_id(0) == 0)
    def _():
        sf_ref[...] = jnp.zeros_like(sf_ref)
        sb_ref[...] = jnp.zeros_like(sb_ref)
        ii = lax.broadcasted_iota(jnp.int32, (c, c), 0).astype(F32)
        jj = lax.broadcasted_iota(jnp.int32, (c, c), 1).astype(F32)
        for h in range(RET_HEADS):
            half = slice((h % 2) * c, (h % 2 + 1) * c)
            df_ref[h // 2, :, half] = jnp.where(ii >= jj, jnp.exp(logg_ref[0, h] * jnp.maximum(ii - jj, 0.0)), 0.0)
            db_ref[h // 2, :, half] = jnp.where(jj >= ii, jnp.exp(logg_ref[1, h] * jnp.maximum(jj - ii, 0.0)), 0.0)
        ri = lax.broadcasted_iota(jnp.int32, (c, RET_VW), 0).astype(F32)
        qdf_ref[...] = jnp.exp(lgv_ref[0] * (ri + 1.0))
        qdb_ref[...] = jnp.exp(lgv_ref[1] * (c - ri))
        cj = lax.broadcasted_iota(jnp.int32, (RET_QP, c), 1).astype(F32)
        kdf_ref[...] = jnp.exp(lgk_ref[0] * (c - 1.0 - cj))
        kdb_ref[...] = jnp.exp(lgk_ref[1] * cj)
        cdf_ref[...] = jnp.exp(lgv_ref[0] * float(c))
        cdb_ref[...] = jnp.exp(lgv_ref[1] * float(c))

    bdmask = bdmask_ref[...]
    halves = list(range(RET_STEP_CHUNKS))
    _ret_dir(qf_ref, ktf_ref, vf_ref, of_ref, sf_ref, df_ref,
             qdf_ref[...], kdf_ref[...], cdf_ref[...], bdmask, hmask_ref, vmask_ref, halves)
    _ret_dir(qb_ref, ktb_ref, vb_ref, ob_ref, sb_ref, db_ref,
             qdb_ref[...], kdb_ref[...], cdb_ref[...], bdmask, hmask_ref, vmask_ref, halves[::-1])


def _ret_masks():
    hm = np.zeros((RET_HEADS, RET_QP, RET_CHUNK), np.float32)
    vm = np.zeros((RET_HEADS, 1, RET_VW), np.float32)
    bd = np.zeros((RET_QP, RET_VW), np.float32)
    for h in range(RET_HEADS):
        hm[h, h * RET_QK:(h + 1) * RET_QK, :] = 1.0
        vm[h, 0, h * RET_V:(h + 1) * RET_V] = 1.0
        bd[h * RET_QK:(h + 1) * RET_QK, h * RET_V:(h + 1) * RET_V] = 1.0
    return jnp.asarray(bd), jnp.asarray(hm, BF16), jnp.asarray(vm, BF16)


def _retention_call(rq, rkt, rv, log_g2, n_lat, n_ctx, n_pad):
    c = RET_CHUNK
    g = RET_STEP_CHUNKS
    assert n_lat % g == 0 and n_ctx % g == 0 and n_pad % g == 0
    n_lat, n_ctx, n_pad = n_lat // g, n_ctx // g, n_pad // g
    n_real = n_lat + n_ctx
    n = n_real + n_pad
    rows = g * c

    def fwd(i):
        return jnp.where(i < n_ctx, n_lat + i, jnp.where(i < n_real, i - n_ctx, i))

    def bwd(i):
        return jnp.where(i < n_real, n_real - 1 - i, i)

    lgv = jnp.repeat(log_g2, RET_V, axis=1).reshape(2, 1, RET_VW)
    lgk = jnp.pad(jnp.repeat(log_g2, RET_QK, axis=1), ((0, 0), (0, RET_QP - RET_QW)))
    lgk = jnp.broadcast_to(lgk[:, :, None], (2, RET_QP, c))
    bd, hm, vm = _ret_masks()

    def specs(ix):
        return [pl.BlockSpec((rows, RET_QP), lambda i: (ix(i), 0)),
                pl.BlockSpec((RET_QP, rows), lambda i: (0, ix(i))),
                pl.BlockSpec((rows, RET_VW), lambda i: (ix(i), 0))]

    vmem = pltpu.VMEM
    return pl.pallas_call(
        _ret_kernel,
        out_shape=(jax.ShapeDtypeStruct((n * rows, RET_VW), F32),) * 2,
        grid=(n,),
        in_specs=[pl.BlockSpec(memory_space=pltpu.SMEM),
                  _const_spec((2, 1, RET_VW)), _const_spec((2, RET_QP, c)),
                  _const_spec((RET_QP, RET_VW)), _const_spec((RET_HEADS, RET_QP, c)),
                  _const_spec((RET_HEADS, 1, RET_VW))] + specs(fwd) + specs(bwd),
        out_specs=(pl.BlockSpec((rows, RET_VW), lambda i: (fwd(i), 0)),
                   pl.BlockSpec((rows, RET_VW), lambda i: (bwd(i), 0))),
        scratch_shapes=[vmem((RET_QP, RET_VW), F32), vmem((RET_QP, RET_VW), F32),
                        vmem((RET_HEADS // 2, c, 2 * c), F32), vmem((RET_HEADS // 2, c, 2 * c), F32),
                        vmem((c, RET_VW), F32), vmem((c, RET_VW), F32),
                        vmem((RET_QP, c), F32), vmem((RET_QP, c), F32),
                        vmem((1, RET_VW), F32), vmem((1, RET_VW), F32)],
        compiler_params=_params(("arbitrary",)),
        name="retention",
    )(log_g2, lgv, lgk, bd, hm, vm, rq, rkt, rv, rq, rkt, rv)


def _chunk_scores(k_chunk, qm, s_out_ref, rows, run_max):
    s = _dot(k_chunk, qm)
    s_out_ref[rows, :] = s
    cm = jnp.max(s.reshape(s.shape[0] // SUBLANES, SUBLANES, s.shape[1]), axis=0)
    return cm if run_max is None else jnp.maximum(run_max, cm)


def _mask_map(qt, j):
    row = lax.broadcasted_iota(jnp.int32, qt.shape, 0)
    lo = j * DIFF_QK
    return jnp.where((row >= lo) & (row < lo + DIFF_QK), qt, jnp.zeros_like(qt))


def _attn_kernel(lam_ref, qt_ref, qtn_ref, k0_ref, kn_ref, vt_ref, subln_ref, o_ref,
                 qm_ref, m_ref, acc_ref, s_ref, mx_ref, kcur_ref, *, out_scale, key_chunk):
    ki = pl.program_id(2)
    is_last = ki == pl.num_programs(2) - 1
    tk = kn_ref.shape[0]
    n_chunks = tk // key_chunk

    def chunk(c):
        return slice(c * key_chunk, (c + 1) * key_chunk)

    @pl.when(ki == 0)
    def _():
        qt = qt_ref[...]
        for j in range(4):
            qm_ref[j] = _mask_map(qt, j)
        m_ref[...] = jnp.full(m_ref.shape, -jnp.inf, F32)
        acc_ref[...] = jnp.zeros_like(acc_ref)

    @pl.when((ki == 0) & (pl.program_id(0) == 0) & (pl.program_id(1) == 0))
    def _():
        kcur_ref[...] = k0_ref[...]
        run = None
        for c in range(n_chunks):
            run = _chunk_scores(k0_ref[chunk(c), :], qm_ref[0], s_ref.at[0], chunk(c), run)
        mx_ref[...] = run

    q_ahead = jnp.where(is_last, _mask_map(qtn_ref[...], 0), qm_ref[0])

    run = mx_ref[...]
    for u in range(4):
        cur, nxt = u % 2, (u + 1) % 2
        m_prev = m_ref[u]
        m_new = jnp.maximum(m_prev, jnp.max(run, axis=0, keepdims=True))
        alpha = jnp.exp2(m_prev - m_new)
        vt = vt_ref[u // 2]
        run = None
        acc = None
        p_prev = None
        for c in range(n_chunks):
            if u < 3:
                run = _chunk_scores(kcur_ref[chunk(c), :], qm_ref[u + 1], s_ref.at[nxt], chunk(c), run)
            else:
                run = _chunk_scores(kn_ref[chunk(c), :], q_ahead, s_ref.at[nxt], chunk(c), run)
            p = jnp.exp2(s_ref[cur, chunk(c), :] - m_new).astype(BF16)
            if p_prev is not None:
                pv = _dot(vt[:, chunk(c - 1)], p_prev)
                acc = pv if acc is None else acc + pv
            p_prev = p
        pv = _dot(vt[:, chunk(n_chunks - 1)], p_prev)
        acc = pv if acc is None else acc + pv
        acc_ref[u] = alpha * acc_ref[u] + acc
        m_ref[u] = m_new
    mx_ref[...] = run
    kcur_ref[...] = kn_ref[...]

    @pl.when(is_last)
    def _():
        lam = lam_ref[0]
        outs = []
        for hl in range(2):
            a1 = acc_ref[2 * hl]
            a2 = acc_ref[2 * hl + 1]
            o = (a1[:DIFF_V] / a1[DIFF_V:DIFF_V + 1]
                 - lam * (a2[:DIFF_V] / a2[DIFF_V:DIFF_V + 1]))
            ms = jnp.mean(o * o, axis=0, keepdims=True)
            outs.append(o * lax.rsqrt(ms + EPS) * subln_ref[...] * out_scale)
        o_ref[...] = jnp.concatenate(outs, axis=0)


def _attn_call(dqt, dk, vt_aug, lam, subln, lam_init, n_q, tq, tk, nk, q_off=0, k_off=0):
    n_hp = DIFF_HEADS // 2
    n_i = n_q // tq

    def next_pair(h, i):
        wrap = i + 1 >= n_i
        return jnp.where(wrap, jnp.minimum(h + 1, n_hp - 1), h), jnp.where(wrap, 0, i + 1)

    def q_next_map(h, i, k):
        hn, i_n = next_pair(h, i)
        return hn, q_off + i_n

    def k_next_map(h, i, k):
        last = k + 1 >= nk
        return k_off + jnp.where(last, 0, k + 1), jnp.where(last, next_pair(h, i)[0], h)

    return pl.pallas_call(
        functools.partial(_attn_kernel, out_scale=1.0 - lam_init, key_chunk=min(tk, ATTN_KEY_CHUNK)),
        out_shape=jax.ShapeDtypeStruct((DIFF_VW, n_q), F32),
        grid=(n_hp, n_i, nk),
        in_specs=[
            pl.BlockSpec(memory_space=pltpu.SMEM),
            pl.BlockSpec((LANES, tq), lambda h, i, k: (h, q_off + i)),
            pl.BlockSpec((LANES, tq), q_next_map),
            pl.BlockSpec((tk, LANES), lambda h, i, k: (k_off, 0)),
            pl.BlockSpec((tk, LANES), k_next_map),
            pl.BlockSpec((2, ATTN_VT_ROWS, tk), lambda h, i, k: (h, 0, k_off + k)),
            pl.BlockSpec((DIFF_V, 1), lambda h, i, k: (0, 0)),
        ],
        out_specs=pl.BlockSpec((LANES, tq), lambda h, i, k: (h, i)),
        scratch_shapes=[pltpu.VMEM((4, LANES, tq), BF16),
                        pltpu.VMEM((4, 1, tq), F32),
                        pltpu.VMEM((4, ATTN_VT_ROWS, tq), F32),
                        pltpu.VMEM((2, tk, tq), F32),
                        pltpu.VMEM((SUBLANES, tq), F32),
                        pltpu.VMEM((tk, LANES), BF16)],
        compiler_params=_params(("arbitrary", "arbitrary", "arbitrary")),
        name="diff_attn",
    )(lam, dqt, dqt, dk, dk, vt_aug, subln.reshape(DIFF_V, 1))


def _merge_ffn_kernel(x_ref, mod_ref, g_ref, flat_ref, ftail_ref, of_ref, ob_ref, rg_ref, dlat_ref, dtail_ref,
                      wgt_ref, wbf_ref, wbr_ref, wbd_ref, wo_ref, bd_ref, g2_ref, w1_ref, w3_ref, w2_ref, *rest,
                      n_lat_tiles, final):
    x2 = _merge_tile(x_ref, mod_ref, g_ref, flat_ref, ftail_ref, of_ref, ob_ref, rg_ref, dlat_ref, dtail_ref,
                     wgt_ref, wbf_ref, wbr_ref, wbd_ref, wo_ref, bd_ref, n_lat_tiles)
    y = _ffn_tile(x2, mod_ref, g2_ref, w1_ref, w3_ref, w2_ref, 6)
    if final:
        fg_ref, o_ref = rest
        ms = jnp.mean(y * y, axis=-1, keepdims=True)
        o_ref[...] = y * lax.rsqrt(ms + EPS) * fg_ref[...]
    else:
        (o_ref,) = rest
        o_ref[...] = y


def _merge_tile(x_ref, mod_ref, g_ref, flat_ref, ftail_ref, of_ref, ob_ref, rg_ref, dlat_ref, dtail_ref,
                wgt_ref, wbf_ref, wbr_ref, wbd_ref, wo_ref, bd_ref, n_lat_tiles):
    x = x_ref[...]
    hb = _norm_mod(x, g_ref[...], mod_ref[0, 3:4, :], mod_ref[0, 4:5, :]).astype(BF16)
    gates = _sigmoid(_dot(hb, wgt_ref[...]))
    is_tail = pl.program_id(0) >= n_lat_tiles
    f = jnp.where(is_tail, ftail_ref[...], flat_ref[...])
    d = jnp.where(is_tail, dtail_ref[...], dlat_ref[...]).T
    r = of_ref[...] + ob_ref[...]
    rr_hi, rr_lo = _split_bf16(r * r)
    ms = _dot(rr_hi, bd_ref[...]) + _dot(rr_lo, bd_ref[...])
    rg = rg_ref[...]
    yr = r * lax.rsqrt(ms + EPS) * (rg * _sigmoid(rg))
    mixed = (gates[:, :D_MODEL] * _dot(f.astype(BF16), wbf_ref[...])
             + gates[:, D_MODEL:2 * D_MODEL] * _dot(yr.astype(BF16), wbr_ref[...])
             + gates[:, 2 * D_MODEL:] * _dot(d.astype(BF16), wbd_ref[...]))
    y = _dot(mixed.astype(BF16), wo_ref[...])
    return x + mod_ref[0, 5:6, :] * y


def _merge_call(x, mods, g, f_lat, f_tail, o_f, o_b, rg, d_lat, d_tail, wgt, wbf, wbr, wbd, wo, layer,
                g2, w1, w3, w2, n_lat_tiles, n_tiles, final_g=None):
    tm = TOKEN_TILE
    widx = (layer, 1)
    extra_specs = [_const_spec((1, D_MODEL)), _stacked_spec((D_MODEL, D_FF), widx),
                   _stacked_spec((D_MODEL, D_FF), widx), _stacked_spec((D_FF, D_MODEL), widx)]
    extra_args = [g2.reshape(1, D_MODEL), w1, w3, w2]
    if final_g is not None:
        extra_specs.append(_const_spec((1, D_MODEL)))
        extra_args.append(final_g.reshape(1, D_MODEL))
    bd = np.kron(np.eye(RET_HEADS, dtype=np.float32), np.full((RET_V, RET_V), 1.0 / RET_V, np.float32))

    def lat_spec(w):
        return pl.BlockSpec((tm, w), lambda i: (jnp.minimum(i, n_lat_tiles - 1), 0))

    return pl.pallas_call(
        functools.partial(_merge_ffn_kernel, n_lat_tiles=n_lat_tiles, final=final_g is not None),
        out_shape=jax.ShapeDtypeStruct((n_tiles * tm, D_MODEL), F32),
        grid=(n_tiles,),
        in_specs=[
            _row_spec(D_MODEL), _mod_spec(n_lat_tiles), _const_spec((1, D_MODEL)),
            lat_spec(F_W), _const_spec((tm, F_W)),
            _row_spec(RET_VW), _row_spec(RET_VW), _row_spec(RET_VW),
            pl.BlockSpec((DIFF_VW, tm), lambda i: (0, jnp.minimum(i, n_lat_tiles - 1))),
            _const_spec((DIFF_VW, tm)),
            _stacked_spec((D_MODEL, GATE_W), (layer,)), _stacked_spec((F_W, D_MODEL), (layer,)),
            _stacked_spec((RET_VW, D_MODEL), (layer,)), _stacked_spec((DIFF_VW, D_MODEL), (layer,)),
            _stacked_spec((D_MODEL, D_MODEL), (layer,)), _const_spec((RET_VW, RET_VW)),
        ] + extra_specs,
        out_specs=_row_spec(D_MODEL),
        compiler_params=_params(("arbitrary",)),
        name="merge_ffn",
    )(x, mods, g.reshape(1, D_MODEL), f_lat, f_tail, o_f, o_b, rg, d_lat, d_tail, wgt, wbf, wbr, wbd, wo,
      jnp.asarray(bd, BF16), *extra_args)


def _pick_tile(n, candidates):
    for c in candidates:
        if n % c == 0:
            return c
    raise ValueError(f"no tile for {n}")


def kernel(x, c, ctx, c_ctx, w_ada, b_ada, norm_g, ffn_w1, ffn_w3, ffn_w2, w_in, ret_decay_logit,
           diff_lambda, diff_subln, w_branch_f, w_branch_r, w_branch_d, w_out, final_g):
    batch, seq, d = x.shape
    ctx_len = ctx.shape[1]
    tm = TOKEN_TILE
    assert batch == 1 and d == D_MODEL
    assert seq % max(DFT_N1 * SUBLANES, tm) == 0 and seq % ctx_len == 0
    assert ctx_len % (RET_STEP_CHUNKS * RET_CHUNK) == 0 and ctx_len % ATTN_KEY_CHUNK == 0 and ctx_len <= tm
    total = seq + ctx_len
    n_lat_tiles = seq // tm
    n_tiles = n_lat_tiles + 1
    n_rows = n_tiles * tm
    n_lat_chunks = seq // RET_CHUNK
    n_ctx_chunks = ctx_len // RET_CHUNK
    n_pad_chunks = (n_rows - total) // RET_CHUNK

    cc = jnp.zeros((SUBLANES, D_MODEL), F32).at[0].set(c[0]).at[1].set(c_ctx)
    mods_all = _ada_call(cc, w_ada, b_ada)[:, :2].reshape(DEPTH, 2, N_MOD, D_MODEL)

    tables = _rope_tables(seq, n_rows)
    twc, tws = _twiddles(seq)
    log_g2_all = jax.nn.log_sigmoid(ret_decay_logit.astype(F32))
    lv = diff_lambda.astype(F32)
    w_aug_all, wgt_all = _prep_proj_weights(w_in)
    w1_all, w3_all, w2_all = ffn_w1.astype(BF16), ffn_w3.astype(BF16), ffn_w2.astype(BF16)
    wbf_all, wbr_all = w_branch_f.astype(BF16), w_branch_r.astype(BF16)
    wbd_all, wo_all = w_branch_d.astype(BF16), w_out.astype(BF16)

    tq = _pick_tile(seq, (512, 256, 128))
    tk = _pick_tile(total, (3328, 1280, 640, 256, 128))
    tail_pad = ((0, tm - ctx_len), (0, 0))

    xs = x[0]
    x_tail = jnp.pad(ctx[0], tail_pad)
    for l in range(DEPTH):
        last = l == DEPTH - 1
        lam_init = 0.8 - 0.6 * math.exp(-0.3 * l)
        mods = mods_all[l]
        lam = (jnp.exp(jnp.sum(lv[l, 0] * lv[l, 1])) - jnp.exp(jnp.sum(lv[l, 2] * lv[l, 3]))
               + lam_init).reshape(1)

        if l == 0:
            xs = _ffn_call(xs, mods, norm_g[l, 0], w1_all, w3_all, w2_all, (l, 0), 0, n_lat_tiles, n_tiles,
                           x_tail=x_tail)
            uf, rq, rkt, rv, rg, dqt, dk, vt_aug = _proj_call(xs, mods, norm_g[l, 1], w_aug_all, l, tables,
                                                               n_lat_tiles, n_tiles)
        else:
            xs, uf, rq, rkt, rv, rg, dqt, dk, vt_aug = _proj_call(
                xs, mods, norm_g[l, 1], w_aug_all, l, tables, n_lat_tiles, n_tiles,
                ffn=(norm_g[l, 0], w1_all, w3_all, w2_all))

        f_lat = _fourier_latent(uf, seq, twc, tws)
        o_f, o_b = _retention_call(rq, rkt, rv, log_g2_all[l], n_lat_chunks, n_ctx_chunks, n_pad_chunks)
        d_lat = _attn_call(dqt, dk, vt_aug, lam, diff_subln[l], lam_init, seq, tq, tk, total // tk)

        if last:
            f_tail = jnp.zeros((tm, F_W), F32)
            d_tail = jnp.zeros((DIFF_VW, tm), F32)
            n_out = n_lat_tiles
        else:
            f_tail = jnp.pad(_fourier_ctx(uf[seq:total]), tail_pad)
            d_ctx = _attn_call(dqt, dk, vt_aug, lam, diff_subln[l], lam_init, ctx_len, ctx_len, ctx_len, 1,
                               q_off=seq // ctx_len, k_off=seq // ctx_len)
            d_tail = jnp.pad(d_ctx, tail_pad[::-1])
            n_out = n_tiles

        xs = _merge_call(xs, mods, norm_g[l, 1], f_lat, f_tail, o_f, o_b, rg, d_lat, d_tail, wgt_all,
                         wbf_all, wbr_all, wbd_all, wo_all, l, norm_g[l, 2], w1_all, w3_all, w2_all,
                         n_lat_tiles, n_out, final_g=final_g if last else None)

    return xs.reshape(1, seq, D_MODEL)
```
